```python
import math
import jax, jax.numpy as jnp
from jax import lax
import numpy as np

D_MODEL = 1024
BATCH = 8
SEQ = 2048
DEPTH = 2

GRID_W = 64
CTX_LEN = 256

FOURIER_GROUPS = 4
FOURIER_GROUP_DIM = 64
FOURIER_WIDTH = FOURIER_GROUPS * FOURIER_GROUP_DIM
CONV_WIDTH = 256
CONV_K = 3
ATTN_HEADS = 4
QK_DIM = 64
V_DIM = 2 * QK_DIM
ATTN_QK_WIDTH = ATTN_HEADS * 2 * QK_DIM
ATTN_V_WIDTH = ATTN_HEADS * V_DIM
Q_BLOCK = 128
ROPE_BASE = 10000.0
ROPE_AXIS_DIM = QK_DIM // 2
N_BRANCH = 3
D_FF = -(-8 * D_MODEL // (3 * 256)) * 256
EPS = 1e-6

F_OFF = 0
CB_OFF = F_OFF + FOURIER_WIDTH
CC_OFF = CB_OFF + CONV_WIDTH
CX_OFF = CC_OFF + CONV_WIDTH
Q_OFF = CX_OFF + CONV_WIDTH
K_OFF = Q_OFF + ATTN_QK_WIDTH
V_OFF = K_OFF + ATTN_QK_WIDTH
G_OFF = V_OFF + ATTN_V_WIDTH
D_IN = G_OFF + N_BRANCH * D_MODEL

kernel_name = "hybrid_fourier_conv_diffattn_dit_block"


def rmsnorm(x, g):
    xf = x.astype(jnp.float32)
    y = xf * lax.rsqrt(jnp.mean(xf * xf, axis=-1, keepdims=True) + EPS)
    return (y * g.astype(jnp.float32)).astype(x.dtype)


def modulate(h, shift, scale):
    return h * (1 + scale[:, None, :]) + shift[:, None, :]


def rot_half(x, cos, sin):
    n = x.shape[-1] // 2
    x1, x2 = x[..., :n], x[..., n:]
    return jnp.concatenate([x1 * cos - x2 * sin, x2 * cos + x1 * sin], axis=-1)


def axial_rope(x, cos_r, sin_r, cos_c, sin_c):
    a = ROPE_AXIS_DIM
    return jnp.concatenate([rot_half(x[..., :a], cos_r, sin_r),
                            rot_half(x[..., a:], cos_c, sin_c)], axis=-1)


def rope_tables(pos, dtype):
    freqs = ROPE_BASE ** (-jnp.arange(0, ROPE_AXIS_DIM, 2, dtype=jnp.float32) / ROPE_AXIS_DIM)
    ang = pos.astype(jnp.float32)[:, None] * freqs[None, :]
    ang = ang[:, None, None, :]
    return jnp.cos(ang).astype(dtype), jnp.sin(ang).astype(dtype)


def split_qkv(p):
    B, L, _ = p.shape
    q = p[..., Q_OFF:K_OFF].reshape(B, L, ATTN_HEADS, 2, QK_DIM)
    k = p[..., K_OFF:V_OFF].reshape(B, L, ATTN_HEADS, 2, QK_DIM)
    v = p[..., V_OFF:G_OFF].reshape(B, L, ATTN_HEADS, V_DIM)
    return q, k, v


def to_heads_qk(t):
    return t.transpose(0, 2, 3, 1, 4)


def diff_attn_core(q, k, v, lam):
    s = jnp.einsum('bhcqd,bhckd->bhcqk', q, k).astype(jnp.float32) * (QK_DIM ** -0.5)
    pr = jax.nn.softmax(s, axis=-1)
    a = pr[:, :, 0] - lam * pr[:, :, 1]
    return jnp.einsum('bhqk,bhkv->bhqv', a.astype(v.dtype), v)


def diff_attn_blocked(q, k, v, lam):
    B, H, _, L, d = q.shape
    nb = L // Q_BLOCK
    qb = jnp.moveaxis(q.reshape(B, H, 2, nb, Q_BLOCK, d), 3, 0)
    ob = lax.map(lambda qi: diff_attn_core(qi, k, v, lam), qb)
    return jnp.moveaxis(ob, 0, 2).reshape(B, H, L, V_DIM)


def fourier_mix(u):
    B, L, _ = u.shape
    ug = u.astype(jnp.float32).reshape(B, L, FOURIER_GROUPS, FOURIER_GROUP_DIM)
    y = jnp.fft.fft2(ug, axes=(1, 3), norm='ortho').real
    return y.reshape(B, L, FOURIER_WIDTH).astype(u.dtype)


def short_gated_conv(bg, cg, xin, w, b):
    z = cg * xin
    zp = jnp.pad(z, ((0, 0), (1, 1), (0, 0)))
    conv = zp[:, :-2] * w[0] + zp[:, 1:-1] * w[1] + zp[:, 2:] * w[2] + b
    return bg * conv


def merge_branches(p, attn_o, lambda_init, conv_w, conv_b, w_fourier_out, w_conv_out,
                   w_attn_out, subln_g, w_out):
    B, L, _ = p.shape
    y_f = fourier_mix(p[..., F_OFF:CB_OFF]) @ w_fourier_out
    y_c = short_gated_conv(p[..., CB_OFF:CC_OFF], p[..., CC_OFF:CX_OFF], p[..., CX_OFF:Q_OFF],
                           conv_w, conv_b) @ w_conv_out
    o = rmsnorm(attn_o.transpose(0, 2, 1, 3), subln_g) * (1.0 - lambda_init)
    y_a = o.reshape(B, L, ATTN_V_WIDTH) @ w_attn_out
    gates = jax.nn.sigmoid(p[..., G_OFF:].reshape(B, L, N_BRANCH, D_MODEL))
    y = gates[:, :, 0] * y_f + gates[:, :, 1] * y_c + gates[:, :, 2] * y_a
    return y @ w_out


def swiglu(h, wg, wu, wd):
    return (jax.nn.silu(h @ wg) * (h @ wu)) @ wd


def setup_inputs(seed: int = 0) -> dict:
    key = jax.random.key(seed)
    ks = jax.random.split(key, 25)
    n = lambda k, shape, s: jax.random.normal(k, shape, jnp.float32) * s
    D = D_MODEL
    return {
        "x": n(ks[0], (BATCH, SEQ, D), 1.0),
        "c": n(ks[1], (BATCH, D), 1.0),
        "ctx": n(ks[2], (BATCH, CTX_LEN, D), 1.0),
        "c_ctx": n(ks[3], (D,), 1.0),
        "w_ada": n(ks[4], (DEPTH, D, 6 * D), 0.2 * D ** -0.5),
        "b_ada": n(ks[5], (DEPTH, 6 * D), 0.01),
        "norm1_g": 1.0 + n(ks[6], (DEPTH, D), 0.02),
        "norm2_g": 1.0 + n(ks[7], (DEPTH, D), 0.02),
        "w_in": n(ks[8], (DEPTH, D, D_IN), D ** -0.5),
        "b_in": n(ks[9], (DEPTH, D_IN), 0.01),
        "conv_w": n(ks[10], (DEPTH, CONV_K, CONV_WIDTH), CONV_K ** -0.5),
        "conv_b": n(ks[11], (DEPTH, CONV_WIDTH), 0.01),
        "w_fourier_out": n(ks[12], (DEPTH, FOURIER_WIDTH, D), FOURIER_WIDTH ** -0.5),
        "w_conv_out": n(ks[13], (DEPTH, CONV_WIDTH, D), CONV_WIDTH ** -0.5),
        "w_attn_out": n(ks[14], (DEPTH, ATTN_V_WIDTH, D), ATTN_V_WIDTH ** -0.5),
        "lambda_q1": n(ks[15], (DEPTH, QK_DIM), 0.1),
        "lambda_k1": n(ks[16], (DEPTH, QK_DIM), 0.1),
        "lambda_q2": n(ks[17], (DEPTH, QK_DIM), 0.1),
        "lambda_k2": n(ks[18], (DEPTH, QK_DIM), 0.1),
        "subln_g": 1.0 + n(ks[19], (DEPTH, V_DIM), 0.02),
        "w_out": n(ks[20], (DEPTH, D, D), D ** -0.5),
        "w_ffn_gate": n(ks[21], (DEPTH, D, D_FF), D ** -0.5),
        "w_ffn_up": n(ks[22], (DEPTH, D, D_FF), D ** -0.5),
        "w_ffn_down": n(ks[23], (DEPTH, D_FF, D), D_FF ** -0.5),
        "final_g": 1.0 + n(ks[24], (D,), 0.02),
    }


def reference(x, c, ctx, c_ctx, w_ada, b_ada, norm1_g, norm2_g, w_in, b_in, conv_w, conv_b,
              w_fourier_out, w_conv_out, w_attn_out, lambda_q1, lambda_k1, lambda_q2, lambda_k2,
              subln_g, w_out, w_ffn_gate, w_ffn_up, w_ffn_down, final_g):
    L = x.shape[1]
    n_rows = L // GRID_W
    row = jnp.repeat(jnp.arange(n_rows), GRID_W)
    col = jnp.tile(jnp.arange(GRID_W), n_rows)
    cos_r, sin_r = rope_tables(row, x.dtype)
    cos_c, sin_c = rope_tables(col, x.dtype)

    s_c = jax.nn.silu(c)
    s_cc = jax.nn.silu(c_ctx)[None, :]
    xl, xc = x, ctx
    for l in range(DEPTH):
        last = l == DEPTH - 1
        lambda_init = 0.8 - 0.6 * math.exp(-0.3 * l)
        mod = s_c @ w_ada[l] + b_ada[l]
        modc = s_cc @ w_ada[l] + b_ada[l]
        sh1, sc1, g1, sh2, sc2, g2 = jnp.split(mod, 6, axis=-1)
        csh1, csc1, cg1, csh2, csc2, cg2 = jnp.split(modc, 6, axis=-1)

        h = modulate(rmsnorm(xl, norm1_g[l]), sh1, sc1)
        hc = modulate(rmsnorm(xc, norm1_g[l]), csh1, csc1)
        p = h @ w_in[l] + b_in[l]
        pc = hc @ w_in[l] + b_in[l]

        q, k, v = split_qkv(p)
        q = axial_rope(q, cos_r, sin_r, cos_c, sin_c)
        k = axial_rope(k, cos_r, sin_r, cos_c, sin_c)
        qc, kc, vc = split_qkv(pc)
        q, k, qc, kc = to_heads_qk(q), to_heads_qk(k), to_heads_qk(qc), to_heads_qk(kc)
        v, vc = v.transpose(0, 2, 1, 3), vc.transpose(0, 2, 1, 3)

        lam = (jnp.exp(jnp.sum(lambda_q1[l].astype(jnp.float32) * lambda_k1[l].astype(jnp.float32)))
               - jnp.exp(jnp.sum(lambda_q2[l].astype(jnp.float32) * lambda_k2[l].astype(jnp.float32)))
               + lambda_init)

        k_all = jnp.concatenate([k, kc], axis=3)
        v_all = jnp.concatenate([v, vc], axis=2)
        o = diff_attn_blocked(q, k_all, v_all, lam)

        xl = xl + g1[:, None, :] * merge_branches(p, o, lambda_init, conv_w[l], conv_b[l],
                                                  w_fourier_out[l], w_conv_out[l], w_attn_out[l],
                                                  subln_g[l], w_out[l])
        h2 = modulate(rmsnorm(xl, norm2_g[l]), sh2, sc2)
        xl = xl + g2[:, None, :] * swiglu(h2, w_ffn_gate[l], w_ffn_up[l], w_ffn_down[l])

        if not last:
            oc = diff_attn_core(qc, kc, vc, lam)
            xc = xc + cg1[:, None, :] * merge_branches(pc, oc, lambda_init, conv_w[l], conv_b[l],
                                                       w_fourier_out[l], w_conv_out[l],
                                                       w_attn_out[l], subln_g[l], w_out[l])
            hc2 = modulate(rmsnorm(xc, norm2_g[l]), csh2, csc2)
            xc = xc + cg2[:, None, :] * swiglu(hc2, w_ffn_gate[l], w_ffn_up[l], w_ffn_down[l])

    return rmsnorm(xl, final_g)
```

```python
import functools
import math

import numpy as np
import jax
import jax.numpy as jnp
from jax import lax
from jax.experimental import pallas as pl
from jax.experimental.pallas import tpu as pltpu

D_MODEL = 1024
BATCH = 8
SEQ = 2048
DEPTH = 2
GRID_W = 64
CTX_LEN = 256
FOURIER_GROUPS = 4
FOURIER_GROUP_DIM = 64
FOURIER_WIDTH = FOURIER_GROUPS * FOURIER_GROUP_DIM
CONV_WIDTH = 256
ATTN_HEADS = 4
QK_DIM = 64
V_DIM = 2 * QK_DIM
ATTN_QK_WIDTH = ATTN_HEADS * 2 * QK_DIM
ATTN_V_WIDTH = ATTN_HEADS * V_DIM
ROPE_BASE = 10000.0
ROPE_AXIS_DIM = QK_DIM // 2
N_BRANCH = 3
D_FF = 2816
EPS = 1e-6

F_OFF = 0
CB_OFF = F_OFF + FOURIER_WIDTH
CC_OFF = CB_OFF + CONV_WIDTH
CX_OFF = CC_OFF + CONV_WIDTH
Q_OFF = CX_OFF + CONV_WIDTH
K_OFF = Q_OFF + ATTN_QK_WIDTH
V_OFF = K_OFF + ATTN_QK_WIDTH
G_OFF = V_OFF + ATTN_V_WIDTH
D_IN = G_OFF + N_BRANCH * D_MODEL

S_ALL = SEQ + CTX_LEN
TM = 256
NT_ALL = S_ALL // TM
NT_LAT = SEQ // TM
MOD_ROWS = 16
CTX_MOD_ROW = BATCH
ADA_TN = 1536
SUBLANES = 8
LANES = 128
V7X_VMEM_LIMIT = 56 * 1024 * 1024

BF16 = jnp.bfloat16
F32 = jnp.float32


def _dot(a, b):
    return jnp.dot(a, b, preferred_element_type=F32)


def _rms(x, g):
    return x * lax.rsqrt(jnp.mean(x * x, axis=-1, keepdims=True) + EPS) * g


def _norm_mod(x, g, shift, scale):
    return _rms(x, g) * (1.0 + scale) + shift


def _resident(shape):
    nd = len(shape)
    return pl.BlockSpec(shape, lambda *_: (0,) * nd, pipeline_mode=pl.Buffered(1))


def _params(n_axes):
    return pltpu.CompilerParams(dimension_semantics=("arbitrary",) * n_axes,
                                vmem_limit_bytes=V7X_VMEM_LIMIT)


def _ada_kernel(cc_ref, w_ref, b_ref, o_ref):
    s = jax.nn.silu(cc_ref[...]).astype(BF16)
    o_ref[0] = _dot(s, w_ref[0].astype(BF16)) + b_ref[0]


def _ada(cc, w_ada, b_ada):
    n_col = 6 * D_MODEL
    return pl.pallas_call(
        _ada_kernel,
        grid=(DEPTH, n_col // ADA_TN),
        in_specs=[
            pl.BlockSpec((MOD_ROWS, D_MODEL), lambda l, j: (0, 0)),
            pl.BlockSpec((1, D_MODEL, ADA_TN), lambda l, j: (l, 0, j)),
            pl.BlockSpec((1, 1, ADA_TN), lambda l, j: (l, 0, j)),
        ],
        out_specs=pl.BlockSpec((1, MOD_ROWS, ADA_TN), lambda l, j: (l, 0, j)),
        out_shape=jax.ShapeDtypeStruct((DEPTH, MOD_ROWS, n_col), F32),
        compiler_params=_params(2),
        name="ada",
    )(cc, w_ada, b_ada.reshape(DEPTH, 1, n_col))


def _mod_spec():
    return pl.BlockSpec((1, 1, 6 * D_MODEL),
                        lambda b, i: (jnp.where(i == NT_LAT, CTX_MOD_ROW, b), 0, 0))


def _rope(t, cos, sin_signed):
    lane = lax.broadcasted_iota(jnp.int32, t.shape, 1)
    first = (lane % (ROPE_AXIS_DIM)) < (ROPE_AXIS_DIM // 2)
    half = ROPE_AXIS_DIM // 2
    partner = jnp.where(first, pltpu.roll(t, LANES - half, axis=1), pltpu.roll(t, half, axis=1))
    return t * cos + partner * sin_signed


def _in_proj_kernel(x_ref, mod_ref, g_ref, w_ref, b_ref, cos_ref, sin_ref,
                    u_ref, bg_ref, z_ref, q_ref, k_ref, v_ref):
    mod = mod_ref[0]
    h = _norm_mod(x_ref[0], g_ref[...], mod[:, 0:D_MODEL], mod[:, D_MODEL:2 * D_MODEL]).astype(BF16)

    def proj(lo, hi):
        return _dot(h, w_ref[:, lo:hi]) + b_ref[:, lo:hi]

    u_ref[0] = proj(F_OFF, CB_OFF)
    bg_ref[0] = proj(CB_OFF, CC_OFF)
    z_ref[0] = proj(CC_OFF, CX_OFF) * proj(CX_OFF, Q_OFF)
    cos = cos_ref[...]
    sin = sin_ref[...]
    for hd in range(ATTN_HEADS):
        lo = hd * V_DIM
        q = proj(Q_OFF + lo, Q_OFF + lo + V_DIM)
        q_ref[0, :, lo:lo + V_DIM] = (_rope(q, cos, sin) * (QK_DIM ** -0.5)).astype(BF16)
        k = proj(K_OFF + lo, K_OFF + lo + V_DIM)
        k_ref[0, :, lo:lo + V_DIM] = _rope(k, cos, sin).astype(BF16)
    v_ref[0] = proj(V_OFF, G_OFF).astype(BF16)


def _in_proj(xcat, mod, g, w, b, cos_t, sin_t):
    tok = lambda width: pl.BlockSpec((1, TM, width), lambda bb, i: (bb, i, 0))
    out = lambda width, dt: jax.ShapeDtypeStruct((BATCH, S_ALL, width), dt)
    return pl.pallas_call(
        _in_proj_kernel,
        grid=(BATCH, NT_ALL),
        in_specs=[
            tok(D_MODEL), _mod_spec(), _resident((1, D_MODEL)), _resident((D_MODEL, G_OFF)),
            _resident((1, G_OFF)),
            pl.BlockSpec((TM, LANES), lambda bb, i: (i, 0)),
            pl.BlockSpec((TM, LANES), lambda bb, i: (i, 0)),
        ],
        out_specs=[tok(FOURIER_WIDTH), tok(CONV_WIDTH), tok(CONV_WIDTH),
                   tok(ATTN_QK_WIDTH), tok(ATTN_QK_WIDTH), tok(ATTN_V_WIDTH)],
        out_shape=[out(FOURIER_WIDTH, F32), out(CONV_WIDTH, F32), out(CONV_WIDTH, F32),
                   out(ATTN_QK_WIDTH, BF16), out(ATTN_QK_WIDTH, BF16), out(ATTN_V_WIDTH, BF16)],
        compiler_params=_params(2),
        name="in_proj",
    )(xcat, mod, g, w, b, cos_t, sin_t)


def _softmax_parts(s):
    e = jnp.exp(s - jnp.max(s, axis=-1, keepdims=True))
    return e, 1.0 / jnp.sum(e, axis=-1, keepdims=True)


def _attn_kernel(lamv_ref, sub_ref, q_ref, k_ref, v_ref, o_ref, *, lambda_init, with_ctx):
    lv = lamv_ref[...]
    lam = (jnp.exp(jnp.sum(lv[0:1] * lv[1:2], axis=-1, keepdims=True))
           - jnp.exp(jnp.sum(lv[2:3] * lv[3:4], axis=-1, keepdims=True)) + lambda_init)

    def attend(k, v):
        q = q_ref[0]
        lane = lax.broadcasted_iota(jnp.int32, q.shape, 1)
        zero = jnp.zeros_like(q)
        nt = (((1,), (1,)), ((), ()))
        s1 = lax.dot_general(jnp.where(lane < QK_DIM, q, zero), k, nt, preferred_element_type=F32)
        s2 = lax.dot_general(jnp.where(lane >= QK_DIM, q, zero), k, nt, preferred_element_type=F32)
        e1, r1 = _softmax_parts(s1)
        e2, r2 = _softmax_parts(s2)
        a = (e1 * r1 - e2 * (lam * r2)).astype(BF16)
        o = _dot(a, v)
        o_ref[0] = (_rms(o, sub_ref[...]) * (1.0 - lambda_init)).astype(o_ref.dtype)

    if with_ctx:
        is_ctx = pl.program_id(2) == NT_LAT

        @pl.when(jnp.logical_not(is_ctx))
        def _():
            attend(k_ref[0], v_ref[0])

        @pl.when(is_ctx)
        def _():
            attend(k_ref[0, SEQ:S_ALL, :], v_ref[0, SEQ:S_ALL, :])
    else:
        attend(k_ref[0], v_ref[0])


def _attention(lamv, subln_g, q, k, v, *, lambda_init, with_ctx):
    n_q = NT_ALL if with_ctx else NT_LAT
    kv = pl.BlockSpec((1, S_ALL, V_DIM), lambda b, h, i: (b, 0, h))
    qo = pl.BlockSpec((1, TM, V_DIM), lambda b, h, i: (b, i, h))
    return pl.pallas_call(
        functools.partial(_attn_kernel, lambda_init=lambda_init, with_ctx=with_ctx),
        grid=(BATCH, ATTN_HEADS, n_q),
        in_specs=[_resident((4, QK_DIM)), _resident((1, V_DIM)), qo, kv, kv],
        out_specs=qo,
        out_shape=jax.ShapeDtypeStruct((BATCH, n_q * TM, ATTN_V_WIDTH), BF16),
        compiler_params=_params(3),
        name="attention",
    )(lamv, subln_g, q, k, v)


def _dft_cos_sin(n):
    j = np.arange(n, dtype=np.int64)
    ang = 2.0 * np.pi * ((j[:, None] * j[None, :]) % n).astype(np.float64) / n
    return np.cos(ang).astype(np.float32), np.sin(ang).astype(np.float32)


def _fourier_kernel(u_ref, bdc_ref, bds_ref, cl_ref, sl_ref, cc_ref, sc_ref, y_ref, *, with_ctx):
    ub = u_ref[0].astype(BF16)
    tc = _dot(ub, bdc_ref[...]).astype(BF16)
    ts = _dot(ub, bds_ref[...]).astype(BF16)
    y = _dot(cl_ref[...], tc[0:SEQ]) - _dot(sl_ref[...], ts[0:SEQ])
    y_ref[0, 0:SEQ, :] = y * (1.0 / math.sqrt(SEQ * FOURIER_GROUP_DIM))
    if with_ctx:
        yc = _dot(cc_ref[...], tc[SEQ:S_ALL]) - _dot(sc_ref[...], ts[SEQ:S_ALL])
        y_ref[0, SEQ:S_ALL, :] = yc * (1.0 / math.sqrt(CTX_LEN * FOURIER_GROUP_DIM))


def _fourier(u, mats, *, with_ctx):
    rows = S_ALL if with_ctx else SEQ
    return pl.pallas_call(
        functools.partial(_fourier_kernel, with_ctx=with_ctx),
        grid=(BATCH,),
        in_specs=[pl.BlockSpec((1, S_ALL, FOURIER_WIDTH), lambda b: (b, 0, 0))]
                 + [_resident(m.shape) for m in mats],
        out_specs=pl.BlockSpec((1, rows, FOURIER_WIDTH), lambda b: (b, 0, 0)),
        out_shape=jax.ShapeDtypeStruct((BATCH, rows, FOURIER_WIDTH), F32),
        compiler_params=_params(1),
        name="fourier",
    )(u, *mats)


def _merge_kernel(x_ref, mod_ref, g_ref, yf_ref, bg_ref, z_ref, zp_ref, zn_ref, o_ref,
                  cw_ref, cb_ref, wfo_ref, wco_ref, wao_ref, wg_ref, bgate_ref, wout_ref, out_ref):
    i = pl.program_id(1)
    mod = mod_ref[0]
    x = x_ref[0]
    z = z_ref[0]
    row = lax.broadcasted_iota(jnp.int32, z.shape, 0)
    has_prev = jnp.logical_and(i >= 1, i < NT_LAT)
    has_next = i < NT_LAT - 1
    prev_row = jnp.where(has_prev, zp_ref[0, SUBLANES - 1:SUBLANES, :], 0.0)
    next_row = jnp.where(has_next, zn_ref[0, 0:1, :], 0.0)
    z_prev = jnp.where(row == 0, prev_row, pltpu.roll(z, 1, axis=0))
    z_next = jnp.where(row == TM - 1, next_row, pltpu.roll(z, TM - 1, axis=0))
    cw = cw_ref[...]
    conv = z_prev * cw[0:1] + z * cw[1:2] + z_next * cw[2:3] + cb_ref[...]
    y_c = _dot((bg_ref[0] * conv).astype(BF16), wco_ref[...])
    y_f = _dot(yf_ref[0].astype(BF16), wfo_ref[...])
    y_a = _dot(o_ref[0], wao_ref[...])

    h = _norm_mod(x, g_ref[...], mod[:, 0:D_MODEL], mod[:, D_MODEL:2 * D_MODEL]).astype(BF16)

    def gate(n):
        lo = n * D_MODEL
        return jax.nn.sigmoid(_dot(h, wg_ref[:, lo:lo + D_MODEL]) + bgate_ref[:, lo:lo + D_MODEL])

    y = gate(0) * y_f + gate(1) * y_c + gate(2) * y_a
    out_ref[0] = x + mod[:, 2 * D_MODEL:3 * D_MODEL] * _dot(y.astype(BF16), wout_ref[...])


def _merge(xcat, mod, g, yf, bg, z, o, cw, cb, wfo, wco, wao, wg, bgate, wout, *, with_ctx):
    n_t = NT_ALL if with_ctx else NT_LAT
    tok = lambda width: pl.BlockSpec((1, TM, width), lambda b, i: (b, i, 0))
    per8 = TM // SUBLANES
    halo_prev = pl.BlockSpec((1, SUBLANES, CONV_WIDTH),
                             lambda b, i: (b, jnp.maximum(i * per8 - 1, 0), 0))
    halo_next = pl.BlockSpec((1, SUBLANES, CONV_WIDTH),
                             lambda b, i: (b, jnp.minimum((i + 1) * per8, S_ALL // SUBLANES - 1), 0))
    weights = [cw, cb, wfo, wco, wao, wg, bgate, wout]
    return pl.pallas_call(
        _merge_kernel,
        grid=(BATCH, n_t),
        in_specs=[tok(D_MODEL), _mod_spec(), _resident((1, D_MODEL)), tok(FOURIER_WIDTH),
                  tok(CONV_WIDTH), tok(CONV_WIDTH), halo_prev, halo_next, tok(ATTN_V_WIDTH)]
                 + [_resident(w.shape) for w in weights],
        out_specs=tok(D_MODEL),
        out_shape=jax.ShapeDtypeStruct((BATCH, n_t * TM, D_MODEL), F32),
        compiler_params=_params(2),
        name="merge",
    )(xcat, mod, g, yf, bg, z, z, z, o, *weights)


def _ffn_kernel(x_ref, mod_ref, g_ref, wg_ref, wu_ref, wd_ref, fg_ref, out_ref, *, last):
    mod = mod_ref[0]
    x = x_ref[0]
    h = _norm_mod(x, g_ref[...], mod[:, 3 * D_MODEL:4 * D_MODEL],
                  mod[:, 4 * D_MODEL:5 * D_MODEL]).astype(BF16)
    a = (jax.nn.silu(_dot(h, wg_ref[...])) * _dot(h, wu_ref[...])).astype(BF16)
    x2 = x + mod[:, 5 * D_MODEL:6 * D_MODEL] * _dot(a, wd_ref[...])
    out_ref[0] = _rms(x2, fg_ref[...]) if last else x2


def _ffn(x1, mod, g, wg, wu, wd, fg, *, last):
    n_t = x1.shape[1] // TM
    tok = pl.BlockSpec((1, TM, D_MODEL), lambda b, i: (b, i, 0))
    return pl.pallas_call(
        functools.partial(_ffn_kernel, last=last),
        grid=(BATCH, n_t),
        in_specs=[tok, _mod_spec(), _resident((1, D_MODEL)), _resident(wg.shape), _resident(wu.shape),
                  _resident(wd.shape), _resident((1, D_MODEL))],
        out_specs=tok,
        out_shape=jax.ShapeDtypeStruct(x1.shape, F32),
        compiler_params=_params(2),
        name="ffn",
    )(x1, mod, g, wg, wu, wd, fg)


def _rope_tables():
    pos = jnp.arange(SEQ)
    freqs = ROPE_BASE ** (-jnp.arange(0, ROPE_AXIS_DIM, 2, dtype=F32) / ROPE_AXIS_DIM)
    ang_r = (pos // GRID_W).astype(F32)[:, None] * freqs[None, :]
    ang_c = (pos % GRID_W).astype(F32)[:, None] * freqs[None, :]
    cos64 = jnp.concatenate([jnp.cos(ang_r)] * 2 + [jnp.cos(ang_c)] * 2, axis=-1)
    sin64 = jnp.concatenate([-jnp.sin(ang_r), jnp.sin(ang_r), -jnp.sin(ang_c), jnp.sin(ang_c)], axis=-1)
    cos_t = jnp.concatenate([jnp.tile(cos64, (1, 2)), jnp.ones((CTX_LEN, LANES), F32)], axis=0)
    sin_t = jnp.concatenate([jnp.tile(sin64, (1, 2)), jnp.zeros((CTX_LEN, LANES), F32)], axis=0)
    return cos_t, sin_t


def _fourier_mats():
    cg, sg = _dft_cos_sin(FOURIER_GROUP_DIM)
    eye = np.eye(FOURIER_GROUPS, dtype=np.float32)
    mats = [np.kron(eye, cg), np.kron(eye, sg), *_dft_cos_sin(SEQ), *_dft_cos_sin(CTX_LEN)]
    return [jnp.asarray(m).astype(BF16) for m in mats]


def kernel(x, c, ctx, c_ctx, w_ada, b_ada, norm1_g, norm2_g, w_in, b_in, conv_w, conv_b, w_fourier_out,
           w_conv_out, w_attn_out, lambda_q1, lambda_k1, lambda_q2, lambda_k2, subln_g, w_out, w_ffn_gate,
           w_ffn_up, w_ffn_down, final_g):
    cc = jnp.concatenate([c, c_ctx[None, :], jnp.zeros((MOD_ROWS - BATCH - 1, D_MODEL), F32)], axis=0)
    mod_all = _ada(cc, w_ada, b_ada)
    cos_t, sin_t = _rope_tables()
    mats = _fourier_mats()
    xcat = jnp.concatenate([x, ctx], axis=1)
    row = lambda a: a.reshape(1, -1)

    for l in range(DEPTH):
        last = l == DEPTH - 1
        with_ctx = not last
        lambda_init = 0.8 - 0.6 * math.exp(-0.3 * l)
        mod = mod_all[l].reshape(MOD_ROWS, 1, 6 * D_MODEL)
        w_l = w_in[l].astype(BF16)
        g1 = row(norm1_g[l])
        u, bg, z, q, k, v = _in_proj(xcat, mod, g1, w_l[:, :G_OFF], row(b_in[l, :G_OFF]), cos_t, sin_t)
        lamv = jnp.stack([lambda_q1[l], lambda_k1[l], lambda_q2[l], lambda_k2[l]]).astype(F32)
        o = _attention(lamv, row(subln_g[l]), q, k, v, lambda_init=lambda_init, with_ctx=with_ctx)
        yf = _fourier(u, mats, with_ctx=with_ctx)
        x1 = _merge(xcat, mod, g1, yf, bg, z, o, conv_w[l], row(conv_b[l]),
                    w_fourier_out[l].astype(BF16), w_conv_out[l].astype(BF16), w_attn_out[l].astype(BF16),
                    w_l[:, G_OFF:], row(b_in[l, G_OFF:]), w_out[l].astype(BF16), with_ctx=with_ctx)
        xcat = _ffn(x1, mod, row(norm2_g[l]), w_ffn_gate[l].astype(BF16), w_ffn_up[l].astype(BF16),
                    w_ffn_down[l].astype(BF16), row(final_g), last=last)
    return xcat
```

```python
import functools
import math

import numpy as np
import jax
import jax.numpy as jnp
from jax import lax
from jax.experimental import pallas as pl
from jax.experimental.pallas import tpu as pltpu

D_MODEL = 1024
BATCH = 8
SEQ = 2048
DEPTH = 2
GRID_W = 64
CTX_LEN = 256
FOURIER_GROUPS = 4
FOURIER_GROUP_DIM = 64
FOURIER_WIDTH = FOURIER_GROUPS * FOURIER_GROUP_DIM
CONV_WIDTH = 256
ATTN_HEADS = 4
QK_DIM = 64
V_DIM = 2 * QK_DIM
ATTN_QK_WIDTH = ATTN_HEADS * 2 * QK_DIM
ATTN_V_WIDTH = ATTN_HEADS * V_DIM
ROPE_BASE = 10000.0
ROPE_AXIS_DIM = QK_DIM // 2
N_BRANCH = 3
D_FF = 2816
EPS = 1e-6

F_OFF = 0
CB_OFF = F_OFF + FOURIER_WIDTH
CC_OFF = CB_OFF + CONV_WIDTH
CX_OFF = CC_OFF + CONV_WIDTH
Q_OFF = CX_OFF + CONV_WIDTH
K_OFF = Q_OFF + ATTN_QK_WIDTH
V_OFF = K_OFF + ATTN_QK_WIDTH
G_OFF = V_OFF + ATTN_V_WIDTH
D_IN = G_OFF + N_BRANCH * D_MODEL

S_ALL = SEQ + CTX_LEN
TM = 256
NT_ALL = S_ALL // TM
NT_LAT = SEQ // TM
MOD_ROWS = 16
CTX_MOD_ROW = BATCH
ADA_TN = 1536
SUBLANES = 8
LANES = 128
V7X_VMEM_LIMIT = 56 * 1024 * 1024
Q_SCALE = (QK_DIM ** -0.5) * math.log2(math.e)
BF16_SUBLANES = 16

BF16 = jnp.bfloat16
F32 = jnp.float32


def _dot(a, b):
    return jnp.dot(a, b, preferred_element_type=F32)


def _rms(x, g):
    return x * lax.rsqrt(jnp.mean(x * x, axis=-1, keepdims=True) + EPS) * g


def _norm_mod(x, g, shift, scale):
    return _rms(x, g) * (1.0 + scale) + shift


def _resident(shape):
    nd = len(shape)
    return pl.BlockSpec(shape, lambda *_: (0,) * nd, pipeline_mode=pl.Buffered(1))


def _params(n_axes):
    return pltpu.CompilerParams(dimension_semantics=("arbitrary",) * n_axes,
                                vmem_limit_bytes=V7X_VMEM_LIMIT)


def _ada_kernel(cc_ref, w_ref, b_ref, o_ref):
    s = jax.nn.silu(cc_ref[...]).astype(BF16)
    o_ref[0] = _dot(s, w_ref[0].astype(BF16)) + b_ref[0]


def _ada(cc, w_ada, b_ada):
    n_col = 6 * D_MODEL
    return pl.pallas_call(
        _ada_kernel,
        grid=(DEPTH, n_col // ADA_TN),
        in_specs=[
            pl.BlockSpec((MOD_ROWS, D_MODEL), lambda l, j: (0, 0)),
            pl.BlockSpec((1, D_MODEL, ADA_TN), lambda l, j: (l, 0, j)),
            pl.BlockSpec((1, 1, ADA_TN), lambda l, j: (l, 0, j)),
        ],
        out_specs=pl.BlockSpec((1, MOD_ROWS, ADA_TN), lambda l, j: (l, 0, j)),
        out_shape=jax.ShapeDtypeStruct((DEPTH, MOD_ROWS, n_col), F32),
        compiler_params=_params(2),
        name="ada",
    )(cc, w_ada, b_ada.reshape(DEPTH, 1, n_col))


def _mod_spec():
    return pl.BlockSpec((1, 1, 6 * D_MODEL),
                        lambda b, i: (jnp.where(i == NT_LAT, CTX_MOD_ROW, b), 0, 0))


def _rope(t, cos, sin_signed):
    lane = lax.broadcasted_iota(jnp.int32, t.shape, 1)
    first = (lane % (ROPE_AXIS_DIM)) < (ROPE_AXIS_DIM // 2)
    half = ROPE_AXIS_DIM // 2
    partner = jnp.where(first, pltpu.roll(t, LANES - half, axis=1), pltpu.roll(t, half, axis=1))
    return t * cos + partner * sin_signed


def _in_proj_kernel(x_ref, mod_ref, g_ref, w_ref, b_ref, cos_ref, sin_ref,
                    u_ref, bg_ref, z_ref, q_ref, k_ref, v_ref):
    mod = mod_ref[0]
    h = _norm_mod(x_ref[0], g_ref[...], mod[:, 0:D_MODEL], mod[:, D_MODEL:2 * D_MODEL]).astype(BF16)

    def proj(lo, hi):
        return _dot(h, w_ref[:, lo:hi]) + b_ref[:, lo:hi]

    u_ref[0] = proj(F_OFF, CB_OFF).astype(BF16)
    bg_ref[0] = proj(CB_OFF, CC_OFF)
    z_ref[0] = proj(CC_OFF, CX_OFF) * proj(CX_OFF, Q_OFF)
    cos = cos_ref[...]
    sin = sin_ref[...]
    q = proj(Q_OFF, K_OFF)
    k = proj(K_OFF, V_OFF)
    for hd in range(ATTN_HEADS):
        sl = slice(hd * V_DIM, (hd + 1) * V_DIM)
        q_ref[0, :, sl] = (_rope(q[:, sl], cos, sin) * Q_SCALE).astype(BF16)
        k_ref[0, :, sl] = _rope(k[:, sl], cos, sin).astype(BF16)
    v_ref[0] = proj(V_OFF, G_OFF).astype(BF16)


def _in_proj(xcat, mod, g, w, b, cos_t, sin_t):
    tok = lambda width: pl.BlockSpec((1, TM, width), lambda bb, i: (bb, i, 0))
    out = lambda width, dt: jax.ShapeDtypeStruct((BATCH, S_ALL, width), dt)
    return pl.pallas_call(
        _in_proj_kernel,
        grid=(BATCH, NT_ALL),
        in_specs=[
            tok(D_MODEL), _mod_spec(), _resident((1, D_MODEL)), _resident((D_MODEL, G_OFF)),
            _resident((1, G_OFF)),
            pl.BlockSpec((TM, LANES), lambda bb, i: (i, 0)),
            pl.BlockSpec((TM, LANES), lambda bb, i: (i, 0)),
        ],
        out_specs=[tok(FOURIER_WIDTH), tok(CONV_WIDTH), tok(CONV_WIDTH),
                   tok(ATTN_QK_WIDTH), tok(ATTN_QK_WIDTH), tok(ATTN_V_WIDTH)],
        out_shape=[out(FOURIER_WIDTH, BF16), out(CONV_WIDTH, F32), out(CONV_WIDTH, F32),
                   out(ATTN_QK_WIDTH, BF16), out(ATTN_QK_WIDTH, BF16), out(ATTN_V_WIDTH, BF16)],
        compiler_params=_params(2),
        name="in_proj",
    )(xcat, mod, g, w, b, cos_t, sin_t)


VT_ROWS = V_DIM + BF16_SUBLANES


def _attn_kernel(lamv_ref, sub_ref, q_ref, k_ref, v_ref, o_ref, vt_ref, st0_ref, st1_ref, e0_ref, e1_ref, *,
                 lambda_init, with_ctx):
    e_refs = (e0_ref, e1_ref)
    vt_ref[0:V_DIM, :] = v_ref[0].astype(F32).T.astype(BF16)
    r = lax.broadcasted_iota(jnp.int32, (BF16_SUBLANES, S_ALL), 0)
    vt_ref[V_DIM:VT_ROWS, :] = jnp.where(r == 0, 1.0, 0.0).astype(BF16)

    lv = lamv_ref[...]
    lam = (jnp.exp(jnp.sum(lv[0:1] * lv[1:2], axis=-1, keepdims=True))
           - jnp.exp(jnp.sum(lv[2:3] * lv[3:4], axis=-1, keepdims=True)) + lambda_init)
    st_refs = (st0_ref, st1_ref)
    groups = TM // SUBLANES
    col_max = {}

    def scores(t, chunks):
        qt = q_ref[0, t * TM:(t + 1) * TM, :].astype(F32).T
        row = lax.broadcasted_iota(jnp.int32, qt.shape, 0)
        zero = jnp.zeros_like(qt)
        qq = jnp.concatenate([jnp.where(row < QK_DIM, qt, zero), jnp.where(row >= QK_DIM, qt, zero)],
                             axis=1).astype(BF16)
        m8 = None
        for j in chunks:
            s = _dot(k_ref[0, j * TM:(j + 1) * TM, :], qq)
            st_refs[t % 2][j * TM:(j + 1) * TM, :] = s
            c = jnp.max(s.reshape(groups, SUBLANES, 2 * TM), axis=0)
            m8 = c if m8 is None else jnp.maximum(m8, c)
            yield
        col_max[t] = m8

    def exps(t, chunks):
        mb = jnp.broadcast_to(jnp.max(col_max.pop(t), axis=0, keepdims=True), (SUBLANES, 2 * TM))
        for j in chunks:
            s = st_refs[t % 2][j * TM:(j + 1) * TM, :]
            e = jnp.exp2(s.reshape(groups, SUBLANES, 2 * TM) - mb[None]).reshape(TM, 2 * TM)
            e_refs[t % 2][j * TM:(j + 1) * TM, :] = e.astype(BF16)
            yield

    def values(t, chunks):
        lo, hi = chunks[0] * TM, (chunks[-1] + 1) * TM
        acc = _dot(vt_ref[:, lo:hi], e_refs[t % 2][lo:hi, :])
        on = acc[0:V_DIM] * (1.0 / acc[V_DIM:V_DIM + 1])
        o_t = on[:, 0:TM] - lam * on[:, TM:2 * TM]
        o_ref[0, t * TM:(t + 1) * TM, :] = (_rms(o_t.T, sub_ref[...]) * (1.0 - lambda_init)).astype(o_ref.dtype)

    all_chunks = list(range(NT_ALL))
    tiles = [(t, all_chunks) for t in range(NT_LAT)] + ([(NT_LAT, [NT_LAT])] if with_ctx else [])
    n = len(tiles)
    for step in range(n + 2):
        if step >= 2:
            values(*tiles[step - 2])
        a = scores(*tiles[step]) if step < n else iter(())
        b = exps(*tiles[step - 1]) if 1 <= step <= n else iter(())
        while True:
            done_a = next(a, "end") == "end"
            done_b = next(b, "end") == "end"
            if done_a and done_b:
                break


def _attention(lamv, subln_g, q, k, v, *, lambda_init, with_ctx):
    n_q = NT_ALL if with_ctx else NT_LAT
    kv = pl.BlockSpec((1, S_ALL, V_DIM), lambda b, h: (b, 0, h))
    return pl.pallas_call(
        functools.partial(_attn_kernel, lambda_init=lambda_init, with_ctx=with_ctx),
        grid=(BATCH, ATTN_HEADS),
        in_specs=[_resident((4, QK_DIM)), _resident((1, V_DIM)), kv, kv, kv],
        out_specs=pl.BlockSpec((1, n_q * TM, V_DIM), lambda b, h: (b, 0, h)),
        out_shape=jax.ShapeDtypeStruct((BATCH, n_q * TM, ATTN_V_WIDTH), BF16),
        scratch_shapes=[pltpu.VMEM((VT_ROWS, S_ALL), BF16), pltpu.VMEM((S_ALL, 2 * TM), F32),
                        pltpu.VMEM((S_ALL, 2 * TM), F32), pltpu.VMEM((S_ALL, 2 * TM), BF16),
                        pltpu.VMEM((S_ALL, 2 * TM), BF16)],
        compiler_params=_params(2),
        name="attention",
    )(lamv, subln_g, q, k, v)


def _dft_cos_sin(n):
    j = np.arange(n, dtype=np.int64)
    ang = 2.0 * np.pi * ((j[:, None] * j[None, :]) % n).astype(np.float64) / n
    return np.cos(ang).astype(np.float32), np.sin(ang).astype(np.float32)


def _fourier_kernel(u_ref, bdc_ref, bds_ref, cl_ref, sl_ref, cc_ref, sc_ref, y_ref, *, with_ctx):
    ub = u_ref[0]
    tc = _dot(ub, bdc_ref[...]).astype(BF16)
    ts = _dot(ub, bds_ref[...]).astype(BF16)
    y = _dot(cl_ref[...], tc[0:SEQ]) - _dot(sl_ref[...], ts[0:SEQ])
    y_ref[0, 0:SEQ, :] = (y * (1.0 / math.sqrt(SEQ * FOURIER_GROUP_DIM))).astype(BF16)
    if with_ctx:
        yc = _dot(cc_ref[...], tc[SEQ:S_ALL]) - _dot(sc_ref[...], ts[SEQ:S_ALL])
        y_ref[0, SEQ:S_ALL, :] = (yc * (1.0 / math.sqrt(CTX_LEN * FOURIER_GROUP_DIM))).astype(BF16)


def _fourier(u, mats, *, with_ctx):
    rows = S_ALL if with_ctx else SEQ
    return pl.pallas_call(
        functools.partial(_fourier_kernel, with_ctx=with_ctx),
        grid=(BATCH,),
        in_specs=[pl.BlockSpec((1, S_ALL, FOURIER_WIDTH), lambda b: (b, 0, 0))]
                 + [_resident(m.shape) for m in mats],
        out_specs=pl.BlockSpec((1, rows, FOURIER_WIDTH), lambda b: (b, 0, 0)),
        out_shape=jax.ShapeDtypeStruct((BATCH, rows, FOURIER_WIDTH), BF16),
        compiler_params=_params(1),
        name="fourier",
    )(u, *mats)


def _merge_kernel(x_ref, mod_ref, g_ref, yf_ref, bg_ref, z_ref, zp_ref, zn_ref, o_ref,
                  cw_ref, cb_ref, wfo_ref, wco_ref, wao_ref, wg_ref, bgate_ref, wout_ref, out_ref):
    i = pl.program_id(1)
    mod = mod_ref[0]
    x = x_ref[0]
    z = z_ref[0]
    row = lax.broadcasted_iota(jnp.int32, z.shape, 0)
    has_prev = jnp.logical_and(i >= 1, i < NT_LAT)
    has_next = i < NT_LAT - 1
    prev_row = jnp.where(has_prev, zp_ref[0, SUBLANES - 1:SUBLANES, :], 0.0)
    next_row = jnp.where(has_next, zn_ref[0, 0:1, :], 0.0)
    z_prev = jnp.where(row == 0, prev_row, pltpu.roll(z, 1, axis=0))
    z_next = jnp.where(row == TM - 1, next_row, pltpu.roll(z, TM - 1, axis=0))
    cw = cw_ref[...]
    conv = z_prev * cw[0:1] + z * cw[1:2] + z_next * cw[2:3] + cb_ref[...]
    y_c = _dot((bg_ref[0] * conv).astype(BF16), wco_ref[...])
    y_f = _dot(yf_ref[0], wfo_ref[...])
    y_a = _dot(o_ref[0], wao_ref[...])

    h = _norm_mod(x, g_ref[...], mod[:, 0:D_MODEL], mod[:, D_MODEL:2 * D_MODEL]).astype(BF16)

    def gate(n):
        lo = n * D_MODEL
        return jax.nn.sigmoid(_dot(h, wg_ref[:, lo:lo + D_MODEL]) + bgate_ref[:, lo:lo + D_MODEL])

    y = gate(0) * y_f + gate(1) * y_c + gate(2) * y_a
    out_ref[0] = x + mod[:, 2 * D_MODEL:3 * D_MODEL] * _dot(y.astype(BF16), wout_ref[...])


def _merge(xcat, mod, g, yf, bg, z, o, cw, cb, wfo, wco, wao, wg, bgate, wout, *, with_ctx):
    n_t = NT_ALL if with_ctx else NT_LAT
    tok = lambda width: pl.BlockSpec((1, TM, width), lambda b, i: (b, i, 0))
    per8 = TM // SUBLANES
    halo_prev = pl.BlockSpec((1, SUBLANES, CONV_WIDTH),
                             lambda b, i: (b, jnp.maximum(i * per8 - 1, 0), 0))
    halo_next = pl.BlockSpec((1, SUBLANES, CONV_WIDTH),
                             lambda b, i: (b, jnp.minimum((i + 1) * per8, S_ALL // SUBLANES - 1), 0))
    weights = [cw, cb, wfo, wco, wao, wg, bgate, wout]
    return pl.pallas_call(
        _merge_kernel,
        grid=(BATCH, n_t),
        in_specs=[tok(D_MODEL), _mod_spec(), _resident((1, D_MODEL)), tok(FOURIER_WIDTH),
                  tok(CONV_WIDTH), tok(CONV_WIDTH), halo_prev, halo_next, tok(ATTN_V_WIDTH)]
                 + [_resident(w.shape) for w in weights],
        out_specs=tok(D_MODEL),
        out_shape=jax.ShapeDtypeStruct((BATCH, n_t * TM, D_MODEL), F32),
        compiler_params=_params(2),
        name="merge",
    )(xcat, mod, g, yf, bg, z, z, z, o, *weights)


def _ffn_kernel(x_ref, mod_ref, g_ref, wg_ref, wu_ref, wd_ref, fg_ref, out_ref, *, last):
    mod = mod_ref[0]
    x = x_ref[0]
    h = _norm_mod(x, g_ref[...], mod[:, 3 * D_MODEL:4 * D_MODEL],
                  mod[:, 4 * D_MODEL:5 * D_MODEL]).astype(BF16)
    a = (jax.nn.silu(_dot(h, wg_ref[...])) * _dot(h, wu_ref[...])).astype(BF16)
    x2 = x + mod[:, 5 * D_MODEL:6 * D_MODEL] * _dot(a, wd_ref[...])
    out_ref[0] = _rms(x2, fg_ref[...]) if last else x2


def _ffn(x1, mod, g, wg, wu, wd, fg, *, last):
    n_t = x1.shape[1] // TM
    tok = pl.BlockSpec((1, TM, D_MODEL), lambda b, i: (b, i, 0))
    return pl.pallas_call(
        functools.partial(_ffn_kernel, last=last),
        grid=(BATCH, n_t),
        in_specs=[tok, _mod_spec(), _resident((1, D_MODEL)), _resident(wg.shape), _resident(wu.shape),
                  _resident(wd.shape), _resident((1, D_MODEL))],
        out_specs=tok,
        out_shape=jax.ShapeDtypeStruct(x1.shape, F32),
        compiler_params=_params(2),
        name="ffn",
    )(x1, mod, g, wg, wu, wd, fg)


def _rope_tables():
    pos = jnp.arange(SEQ)
    freqs = ROPE_BASE ** (-jnp.arange(0, ROPE_AXIS_DIM, 2, dtype=F32) / ROPE_AXIS_DIM)
    ang_r = (pos // GRID_W).astype(F32)[:, None] * freqs[None, :]
    ang_c = (pos % GRID_W).astype(F32)[:, None] * freqs[None, :]
    cos64 = jnp.concatenate([jnp.cos(ang_r)] * 2 + [jnp.cos(ang_c)] * 2, axis=-1)
    sin64 = jnp.concatenate([-jnp.sin(ang_r), jnp.sin(ang_r), -jnp.sin(ang_c), jnp.sin(ang_c)], axis=-1)
    cos_t = jnp.concatenate([jnp.tile(cos64, (1, 2)), jnp.ones((CTX_LEN, LANES), F32)], axis=0)
    sin_t = jnp.concatenate([jnp.tile(sin64, (1, 2)), jnp.zeros((CTX_LEN, LANES), F32)], axis=0)
    return cos_t, sin_t


def _fourier_mats():
    cg, sg = _dft_cos_sin(FOURIER_GROUP_DIM)
    eye = np.eye(FOURIER_GROUPS, dtype=np.float32)
    mats = [np.kron(eye, cg), np.kron(eye, sg), *_dft_cos_sin(SEQ), *_dft_cos_sin(CTX_LEN)]
    return [jnp.asarray(m).astype(BF16) for m in mats]


def kernel(x, c, ctx, c_ctx, w_ada, b_ada, norm1_g, norm2_g, w_in, b_in, conv_w, conv_b, w_fourier_out,
           w_conv_out, w_attn_out, lambda_q1, lambda_k1, lambda_q2, lambda_k2, subln_g, w_out, w_ffn_gate,
           w_ffn_up, w_ffn_down, final_g):
    cc = jnp.concatenate([c, c_ctx[None, :], jnp.zeros((MOD_ROWS - BATCH - 1, D_MODEL), F32)], axis=0)
    mod_all = _ada(cc, w_ada, b_ada)
    cos_t, sin_t = _rope_tables()
    mats = _fourier_mats()
    xcat = jnp.concatenate([x, ctx], axis=1)
    row = lambda a: a.reshape(1, -1)

    for l in range(DEPTH):
        last = l == DEPTH - 1
        with_ctx = not last
        lambda_init = 0.8 - 0.6 * math.exp(-0.3 * l)
        mod = mod_all[l].reshape(MOD_ROWS, 1, 6 * D_MODEL)
        w_l = w_in[l].astype(BF16)
        g1 = row(norm1_g[l])
        u, bg, z, q, k, v = _in_proj(xcat, mod, g1, w_l[:, :G_OFF], row(b_in[l, :G_OFF]), cos_t, sin_t)
        lamv = jnp.stack([lambda_q1[l], lambda_k1[l], lambda_q2[l], lambda_k2[l]]).astype(F32)
        o = _attention(lamv, row(subln_g[l]), q, k, v, lambda_init=lambda_init, with_ctx=with_ctx)
        yf = _fourier(u, mats, with_ctx=with_ctx)
        x1 = _merge(xcat, mod, g1, yf, bg, z, o, conv_w[l], row(conv_b[l]),
                    w_fourier_out[l].astype(BF16), w_conv_out[l].astype(BF16), w_attn_out[l].astype(BF16),
                    w_l[:, G_OFF:], row(b_in[l, G_OFF:]), w_out[l].astype(BF16), with_ctx=with_ctx)
        xcat = _ffn(x1, mod, row(norm2_g[l]), w_ffn_gate[l].astype(BF16), w_ffn_up[l].astype(BF16),
                    w_ffn_down[l].astype(BF16), row(final_g), last=last)
    return xcat
```

```python
import functools
import math

import numpy as np
import jax
import jax.numpy as jnp
from jax import lax
from jax.experimental import pallas as pl
from jax.experimental.pallas import tpu as pltpu

D_MODEL = 1024
BATCH = 8
SEQ = 2048
DEPTH = 2
GRID_W = 64
CTX_LEN = 256
FOURIER_GROUPS = 4
FOURIER_GROUP_DIM = 64
FOURIER_WIDTH = FOURIER_GROUPS * FOURIER_GROUP_DIM
CONV_WIDTH = 256
ATTN_HEADS = 4
QK_DIM = 64
V_DIM = 2 * QK_DIM
ATTN_QK_WIDTH = ATTN_HEADS * 2 * QK_DIM
ATTN_V_WIDTH = ATTN_HEADS * V_DIM
ROPE_BASE = 10000.0
ROPE_AXIS_DIM = QK_DIM // 2
N_BRANCH = 3
D_FF = 2816
EPS = 1e-6

F_OFF = 0
CB_OFF = F_OFF + FOURIER_WIDTH
CC_OFF = CB_OFF + CONV_WIDTH
CX_OFF = CC_OFF + CONV_WIDTH
Q_OFF = CX_OFF + CONV_WIDTH
K_OFF = Q_OFF + ATTN_QK_WIDTH
V_OFF = K_OFF + ATTN_QK_WIDTH
G_OFF = V_OFF + ATTN_V_WIDTH
D_IN = G_OFF + N_BRANCH * D_MODEL

S_ALL = SEQ + CTX_LEN
TM = 256
NT_ALL = S_ALL // TM
NT_LAT = SEQ // TM
MOD_ROWS = 16
CTX_MOD_ROW = BATCH
ADA_TN = 1536
SUBLANES = 8
LANES = 128
V7X_VMEM_LIMIT = 56 * 1024 * 1024
Q_SCALE = (QK_DIM ** -0.5) * math.log2(math.e)
BF16_SUBLANES = 16

BF16 = jnp.bfloat16
F32 = jnp.float32


def _dot(a, b):
    return jnp.dot(a, b, preferred_element_type=F32)


def _rms(x, g):
    return x * lax.rsqrt(jnp.mean(x * x, axis=-1, keepdims=True) + EPS) * g


def _norm_mod(x, g, shift, scale):
    return _rms(x, g) * (1.0 + scale) + shift


def _resident(shape):
    nd = len(shape)
    return pl.BlockSpec(shape, lambda *_: (0,) * nd, pipeline_mode=pl.Buffered(1))


def _params(n_axes):
    return pltpu.CompilerParams(dimension_semantics=("arbitrary",) * n_axes,
                                vmem_limit_bytes=V7X_VMEM_LIMIT)


def _ada_kernel(cc_ref, w_ref, b_ref, o_ref):
    s = jax.nn.silu(cc_ref[...]).astype(BF16)
    o_ref[0] = _dot(s, w_ref[0].astype(BF16)) + b_ref[0]


def _ada(cc, w_ada, b_ada):
    n_col = 6 * D_MODEL
    return pl.pallas_call(
        _ada_kernel,
        grid=(DEPTH, n_col // ADA_TN),
        in_specs=[
            pl.BlockSpec((MOD_ROWS, D_MODEL), lambda l, j: (0, 0)),
            pl.BlockSpec((1, D_MODEL, ADA_TN), lambda l, j: (l, 0, j)),
            pl.BlockSpec((1, 1, ADA_TN), lambda l, j: (l, 0, j)),
        ],
        out_specs=pl.BlockSpec((1, MOD_ROWS, ADA_TN), lambda l, j: (l, 0, j)),
        out_shape=jax.ShapeDtypeStruct((DEPTH, MOD_ROWS, n_col), F32),
        compiler_params=_params(2),
        name="ada",
    )(cc, w_ada, b_ada.reshape(DEPTH, 1, n_col))


def _mod_spec():
    return pl.BlockSpec((1, 1, 6 * D_MODEL),
                        lambda b, i: (jnp.where(i == NT_LAT, CTX_MOD_ROW, b), 0, 0))


def _token_specs(split):
    if not split:
        return [pl.BlockSpec((1, TM, D_MODEL), lambda b, i: (b, i, 0))]
    return [pl.BlockSpec((1, TM, D_MODEL), lambda b, i: (b, jnp.minimum(i, NT_LAT - 1), 0)),
            pl.BlockSpec((1, CTX_LEN, D_MODEL), lambda b, i: (b, 0, 0))]


def _load_tokens(tok_refs):
    if len(tok_refs) == 1:
        return tok_refs[0][0]
    x_ref, ctx_ref = tok_refs
    return jnp.where(pl.program_id(1) == NT_LAT, ctx_ref[0], x_ref[0])


def _sigmoid(t):
    return 0.5 * jnp.tanh(0.5 * t) + 0.5


def _rope(t, cos, sin_signed):
    lane = lax.broadcasted_iota(jnp.int32, t.shape, 1)
    first = (lane % (ROPE_AXIS_DIM)) < (ROPE_AXIS_DIM // 2)
    half = ROPE_AXIS_DIM // 2
    partner = jnp.where(first, pltpu.roll(t, LANES - half, axis=1), pltpu.roll(t, half, axis=1))
    return t * cos + partner * sin_signed


def _in_proj_kernel(*refs, n_tok):
    tok_refs = refs[:n_tok]
    mod_ref, g_ref, w_ref, b_ref, cos_ref, sin_ref, u_ref, bg_ref, z_ref, q_ref, k_ref, v_ref = refs[n_tok:]
    mod = mod_ref[0]
    h = _norm_mod(_load_tokens(tok_refs), g_ref[...], mod[:, 0:D_MODEL],
                  mod[:, D_MODEL:2 * D_MODEL]).astype(BF16)

    def proj(lo, hi):
        return _dot(h, w_ref[:, lo:hi]) + b_ref[:, lo:hi]

    u_ref[0] = proj(F_OFF, CB_OFF).astype(BF16)
    bg_ref[0] = proj(CB_OFF, CC_OFF)
    z_ref[0] = proj(CC_OFF, CX_OFF) * proj(CX_OFF, Q_OFF)
    cos = cos_ref[...]
    sin = sin_ref[...]
    q = proj(Q_OFF, K_OFF)
    k = proj(K_OFF, V_OFF)
    for hd in range(ATTN_HEADS):
        sl = slice(hd * V_DIM, (hd + 1) * V_DIM)
        q_ref[0, :, sl] = (_rope(q[:, sl], cos, sin) * Q_SCALE).astype(BF16)
        k_ref[0, :, sl] = _rope(k[:, sl], cos, sin).astype(BF16)
    v_ref[0] = proj(V_OFF, G_OFF).astype(BF16)


def _in_proj(tokens, mod, g, w, b, cos_t, sin_t):
    tok = lambda width: pl.BlockSpec((1, TM, width), lambda bb, i: (bb, i, 0))
    out = lambda width, dt: jax.ShapeDtypeStruct((BATCH, S_ALL, width), dt)
    return pl.pallas_call(
        functools.partial(_in_proj_kernel, n_tok=len(tokens)),
        grid=(BATCH, NT_ALL),
        in_specs=_token_specs(len(tokens) == 2) + [
            _mod_spec(), _resident((1, D_MODEL)), _resident((D_MODEL, D_IN)), _resident((1, D_IN)),
            pl.BlockSpec((TM, LANES), lambda bb, i: (i, 0)),
            pl.BlockSpec((TM, LANES), lambda bb, i: (i, 0)),
        ],
        out_specs=[tok(FOURIER_WIDTH), tok(CONV_WIDTH), tok(CONV_WIDTH),
                   tok(ATTN_QK_WIDTH), tok(ATTN_QK_WIDTH), tok(ATTN_V_WIDTH)],
        out_shape=[out(FOURIER_WIDTH, BF16), out(CONV_WIDTH, F32), out(CONV_WIDTH, F32),
                   out(ATTN_QK_WIDTH, BF16), out(ATTN_QK_WIDTH, BF16), out(ATTN_V_WIDTH, BF16)],
        compiler_params=_params(2),
        name="in_proj",
    )(*tokens, mod, g, w, b, cos_t, sin_t)


VT_ROWS = V_DIM + BF16_SUBLANES
TK = 128


def _attn_kernel(lamv_ref, sub_ref, q_ref, k_ref, v_ref, o_ref, vt_ref, st0_ref, st1_ref, e0_ref, e1_ref, *,
                 lambda_init, with_ctx):
    st_refs = (st0_ref, st1_ref)
    e_refs = (e0_ref, e1_ref)
    vt_ref[0:V_DIM, :] = v_ref[0].astype(F32).T.astype(BF16)
    r = lax.broadcasted_iota(jnp.int32, (BF16_SUBLANES, S_ALL), 0)
    vt_ref[V_DIM:VT_ROWS, :] = jnp.where(r == 0, 1.0, 0.0).astype(BF16)

    lv = lamv_ref[...]
    lam = (jnp.exp(jnp.sum(lv[0:1] * lv[1:2], axis=-1, keepdims=True))
           - jnp.exp(jnp.sum(lv[2:3] * lv[3:4], axis=-1, keepdims=True)) + lambda_init)
    groups = TK // SUBLANES
    all_chunks = tuple(range(S_ALL // TK))
    ctx_chunks = tuple(range(SEQ // TK, S_ALL // TK))

    def rows(t):
        return pl.ds(pl.multiple_of(t * TM, TM), TM)

    def scores(t, slot, chunks, col_max):
        qt = q_ref[0, rows(t), :].astype(F32).T
        row = lax.broadcasted_iota(jnp.int32, qt.shape, 0)
        zero = jnp.zeros_like(qt)
        qq = jnp.concatenate([jnp.where(row < QK_DIM, qt, zero), jnp.where(row >= QK_DIM, qt, zero)],
                             axis=1).astype(BF16)
        m8 = None
        for j in chunks:
            s = _dot(k_ref[0, j * TK:(j + 1) * TK, :], qq)
            st_refs[slot][j * TK:(j + 1) * TK, :] = s
            c = jnp.max(s.reshape(groups, SUBLANES, 2 * TM), axis=0)
            m8 = c if m8 is None else jnp.maximum(m8, c)
            yield
        col_max.append(m8)

    def exps(slot, chunks, m8):
        mb = jnp.broadcast_to(jnp.max(m8, axis=0, keepdims=True), (SUBLANES, 2 * TM))
        for j in chunks:
            s = st_refs[slot][j * TK:(j + 1) * TK, :]
            e = jnp.exp2(s.reshape(groups, SUBLANES, 2 * TM) - mb[None]).reshape(TK, 2 * TM)
            e_refs[slot][j * TK:(j + 1) * TK, :] = e.astype(BF16)
            yield

    def values(t, slot, chunks):
        lo, hi = chunks[0] * TK, (chunks[-1] + 1) * TK
        acc = _dot(vt_ref[:, lo:hi], e_refs[slot][lo:hi, :])
        on = acc[0:V_DIM] * (1.0 / acc[V_DIM:V_DIM + 1])
        o_t = on[:, 0:TM] - lam * on[:, TM:2 * TM]
        o_ref[0, rows(t), :] = (_rms(o_t.T, sub_ref[...]) * (1.0 - lambda_init)).astype(o_ref.dtype)

    def run(*gens):
        gens = list(gens)
        while gens:
            gens = [g for g in gens if next(g, "end") != "end"]

    m = []
    run(scores(0, 0, all_chunks, m))
    run(scores(1, 1, all_chunks, m), exps(0, all_chunks, m[0]))
    values(0, 0, all_chunks)
    run(scores(2, 0, all_chunks, m), exps(1, all_chunks, m[1]))
    run(scores(3, 1, all_chunks, m), exps(0, all_chunks, m[2]))

    def pair(i, m_odd):
        t0 = 2 * i
        values(t0 - 3, 1, all_chunks)
        values(t0 - 2, 0, all_chunks)
        mm = []
        run(scores(t0, 0, all_chunks, mm), exps(1, all_chunks, m_odd))
        run(scores(t0 + 1, 1, all_chunks, mm), exps(0, all_chunks, mm[0]))
        return mm[1]

    m_last = lax.fori_loop(2, NT_LAT // 2, pair, m[3])
    values(NT_LAT - 3, 1, all_chunks)
    values(NT_LAT - 2, 0, all_chunks)
    if with_ctx:
        mc = []
        run(scores(NT_LAT, 0, ctx_chunks, mc), exps(1, all_chunks, m_last))
        values(NT_LAT - 1, 1, all_chunks)
        run(exps(0, ctx_chunks, mc[0]))
        values(NT_LAT, 0, ctx_chunks)
    else:
        run(exps(1, all_chunks, m_last))
        values(NT_LAT - 1, 1, all_chunks)


def _attention(lamv, subln_g, q, k, v, *, lambda_init, with_ctx):
    n_q = NT_ALL if with_ctx else NT_LAT
    kv = pl.BlockSpec((1, S_ALL, V_DIM), lambda b, h: (b, 0, h))
    return pl.pallas_call(
        functools.partial(_attn_kernel, lambda_init=lambda_init, with_ctx=with_ctx),
        grid=(BATCH, ATTN_HEADS),
        in_specs=[_resident((4, QK_DIM)), _resident((1, V_DIM)), kv, kv, kv],
        out_specs=pl.BlockSpec((1, n_q * TM, V_DIM), lambda b, h: (b, 0, h)),
        out_shape=jax.ShapeDtypeStruct((BATCH, n_q * TM, ATTN_V_WIDTH), BF16),
        scratch_shapes=[pltpu.VMEM((VT_ROWS, S_ALL), BF16), pltpu.VMEM((S_ALL, 2 * TM), F32),
                        pltpu.VMEM((S_ALL, 2 * TM), F32), pltpu.VMEM((S_ALL, 2 * TM), BF16),
                        pltpu.VMEM((S_ALL, 2 * TM), BF16)],
        compiler_params=_params(2),
        name="attention",
    )(lamv, subln_g, q, k, v)


def _dft_cos_sin(n):
    j = np.arange(n, dtype=np.int64)
    ang = 2.0 * np.pi * ((j[:, None] * j[None, :]) % n).astype(np.float64) / n
    return np.cos(ang).astype(np.float32), np.sin(ang).astype(np.float32)


def _fourier_kernel(u_ref, bdc_ref, bds_ref, cl_ref, sl_ref, cc_ref, sc_ref, y_ref, *, with_ctx):
    ub = u_ref[0]
    tc = _dot(ub, bdc_ref[...]).astype(BF16)
    ts = _dot(ub, bds_ref[...]).astype(BF16)
    y = _dot(cl_ref[...], tc[0:SEQ]) - _dot(sl_ref[...], ts[0:SEQ])
    y_ref[0, 0:SEQ, :] = (y * (1.0 / math.sqrt(SEQ * FOURIER_GROUP_DIM))).astype(BF16)
    if with_ctx:
        yc = _dot(cc_ref[...], tc[SEQ:S_ALL]) - _dot(sc_ref[...], ts[SEQ:S_ALL])
        y_ref[0, SEQ:S_ALL, :] = (yc * (1.0 / math.sqrt(CTX_LEN * FOURIER_GROUP_DIM))).astype(BF16)


def _fourier(u, mats, *, with_ctx):
    rows = S_ALL if with_ctx else SEQ
    return pl.pallas_call(
        functools.partial(_fourier_kernel, with_ctx=with_ctx),
        grid=(BATCH,),
        in_specs=[pl.BlockSpec((1, S_ALL, FOURIER_WIDTH), lambda b: (b, 0, 0))]
                 + [_resident(m.shape) for m in mats],
        out_specs=pl.BlockSpec((1, rows, FOURIER_WIDTH), lambda b: (b, 0, 0)),
        out_shape=jax.ShapeDtypeStruct((BATCH, rows, FOURIER_WIDTH), BF16),
        compiler_params=_params(1),
        name="fourier",
    )(u, *mats)


def _merge_kernel(*refs, n_tok):
    tok_refs = refs[:n_tok]
    (mod_ref, g_ref, yf_ref, bg_ref, z_ref, zp_ref, zn_ref, o_ref, cw_ref, cb_ref, wfo_ref, wco_ref, wao_ref,
     win_ref, bin_ref, wout_ref, out_ref) = refs[n_tok:]
    i = pl.program_id(1)
    mod = mod_ref[0]
    x = _load_tokens(tok_refs)
    z = z_ref[0]
    row = lax.broadcasted_iota(jnp.int32, z.shape, 0)
    has_prev = jnp.logical_and(i >= 1, i < NT_LAT)
    has_next = i < NT_LAT - 1
    prev_row = jnp.where(has_prev, zp_ref[0, SUBLANES - 1:SUBLANES, :], 0.0)
    next_row = jnp.where(has_next, zn_ref[0, 0:1, :], 0.0)
    z_prev = jnp.where(row == 0, prev_row, pltpu.roll(z, 1, axis=0))
    z_next = jnp.where(row == TM - 1, next_row, pltpu.roll(z, TM - 1, axis=0))
    cw = cw_ref[...]
    conv = z_prev * cw[0:1] + z * cw[1:2] + z_next * cw[2:3] + cb_ref[...]
    y_c = _dot((bg_ref[0] * conv).astype(BF16), wco_ref[...])
    y_f = _dot(yf_ref[0], wfo_ref[...])
    y_a = _dot(o_ref[0], wao_ref[...])

    h = _norm_mod(x, g_ref[...], mod[:, 0:D_MODEL], mod[:, D_MODEL:2 * D_MODEL]).astype(BF16)

    def gate(n):
        lo = G_OFF + n * D_MODEL
        return _sigmoid(_dot(h, win_ref[:, lo:lo + D_MODEL]) + bin_ref[:, lo:lo + D_MODEL])

    y = gate(0) * y_f + gate(1) * y_c + gate(2) * y_a
    out_ref[0] = x + mod[:, 2 * D_MODEL:3 * D_MODEL] * _dot(y.astype(BF16), wout_ref[...])


def _merge(tokens, mod, g, yf, bg, z, o, cw, cb, wfo, wco, wao, w_in, b_in, wout, *, with_ctx):
    n_t = NT_ALL if with_ctx else NT_LAT
    tok = lambda width: pl.BlockSpec((1, TM, width), lambda b, i: (b, i, 0))
    per8 = TM // SUBLANES
    halo_prev = pl.BlockSpec((1, SUBLANES, CONV_WIDTH),
                             lambda b, i: (b, jnp.maximum(i * per8 - 1, 0), 0))
    halo_next = pl.BlockSpec((1, SUBLANES, CONV_WIDTH),
                             lambda b, i: (b, jnp.minimum((i + 1) * per8, S_ALL // SUBLANES - 1), 0))
    weights = [cw, cb, wfo, wco, wao, w_in, b_in, wout]
    return pl.pallas_call(
        functools.partial(_merge_kernel, n_tok=len(tokens)),
        grid=(BATCH, n_t),
        in_specs=_token_specs(len(tokens) == 2)
                 + [_mod_spec(), _resident((1, D_MODEL)), tok(FOURIER_WIDTH), tok(CONV_WIDTH), tok(CONV_WIDTH),
                    halo_prev, halo_next, tok(ATTN_V_WIDTH)]
                 + [_resident(w.shape) for w in weights],
        out_specs=tok(D_MODEL),
        out_shape=jax.ShapeDtypeStruct((BATCH, n_t * TM, D_MODEL), F32),
        compiler_params=_params(2),
        name="merge",
    )(*tokens, mod, g, yf, bg, z, z, z, o, *weights)


def _ffn_kernel(x_ref, mod_ref, g_ref, wg_ref, wu_ref, wd_ref, fg_ref, out_ref, *, last):
    mod = mod_ref[0]
    x = x_ref[0]
    h = _norm_mod(x, g_ref[...], mod[:, 3 * D_MODEL:4 * D_MODEL],
                  mod[:, 4 * D_MODEL:5 * D_MODEL]).astype(BF16)
    gate = _dot(h, wg_ref[...])
    a = (gate * _sigmoid(gate) * _dot(h, wu_ref[...])).astype(BF16)
    x2 = x + mod[:, 5 * D_MODEL:6 * D_MODEL] * _dot(a, wd_ref[...])
    out_ref[0] = _rms(x2, fg_ref[...]) if last else x2


def _ffn(x1, mod, g, wg, wu, wd, fg, *, last):
    n_t = x1.shape[1] // TM
    tok = pl.BlockSpec((1, TM, D_MODEL), lambda b, i: (b, i, 0))
    return pl.pallas_call(
        functools.partial(_ffn_kernel, last=last),
        grid=(BATCH, n_t),
        in_specs=[tok, _mod_spec(), _resident((1, D_MODEL)), _resident(wg.shape), _resident(wu.shape),
                  _resident(wd.shape), _resident((1, D_MODEL))],
        out_specs=tok,
        out_shape=jax.ShapeDtypeStruct(x1.shape, F32),
        compiler_params=_params(2),
        name="ffn",
    )(x1, mod, g, wg, wu, wd, fg)


def _rope_tables():
    pos = jnp.arange(SEQ)
    freqs = ROPE_BASE ** (-jnp.arange(0, ROPE_AXIS_DIM, 2, dtype=F32) / ROPE_AXIS_DIM)
    ang_r = (pos // GRID_W).astype(F32)[:, None] * freqs[None, :]
    ang_c = (pos % GRID_W).astype(F32)[:, None] * freqs[None, :]
    cos64 = jnp.concatenate([jnp.cos(ang_r)] * 2 + [jnp.cos(ang_c)] * 2, axis=-1)
    sin64 = jnp.concatenate([-jnp.sin(ang_r), jnp.sin(ang_r), -jnp.sin(ang_c), jnp.sin(ang_c)], axis=-1)
    cos_t = jnp.concatenate([jnp.tile(cos64, (1, 2)), jnp.ones((CTX_LEN, LANES), F32)], axis=0)
    sin_t = jnp.concatenate([jnp.tile(sin64, (1, 2)), jnp.zeros((CTX_LEN, LANES), F32)], axis=0)
    return cos_t, sin_t


def _fourier_mats():
    cg, sg = _dft_cos_sin(FOURIER_GROUP_DIM)
    eye = np.eye(FOURIER_GROUPS, dtype=np.float32)
    mats = [np.kron(eye, cg), np.kron(eye, sg), *_dft_cos_sin(SEQ), *_dft_cos_sin(CTX_LEN)]
    return [jnp.asarray(m).astype(BF16) for m in mats]


def kernel(x, c, ctx, c_ctx, w_ada, b_ada, norm1_g, norm2_g, w_in, b_in, conv_w, conv_b, w_fourier_out,
           w_conv_out, w_attn_out, lambda_q1, lambda_k1, lambda_q2, lambda_k2, subln_g, w_out, w_ffn_gate,
           w_ffn_up, w_ffn_down, final_g):
    cc = jnp.concatenate([c, c_ctx[None, :], jnp.zeros((MOD_ROWS - BATCH - 1, D_MODEL), F32)], axis=0)
    mod_all = _ada(cc, w_ada, b_ada)
    cos_t, sin_t = _rope_tables()
    mats = _fourier_mats()
    tokens = (x, ctx)
    row = lambda a: a.reshape(1, -1)

    for l in range(DEPTH):
        last = l == DEPTH - 1
        with_ctx = not last
        lambda_init = 0.8 - 0.6 * math.exp(-0.3 * l)
        mod = mod_all[l].reshape(MOD_ROWS, 1, 6 * D_MODEL)
        w_l = w_in[l].astype(BF16)
        g1 = row(norm1_g[l])
        u, bg, z, q, k, v = _in_proj(tokens, mod, g1, w_l, row(b_in[l]), cos_t, sin_t)
        lamv = jnp.stack([lambda_q1[l], lambda_k1[l], lambda_q2[l], lambda_k2[l]]).astype(F32)
        o = _attention(lamv, row(subln_g[l]), q, k, v, lambda_init=lambda_init, with_ctx=with_ctx)
        yf = _fourier(u, mats, with_ctx=with_ctx)
        x1 = _merge(tokens, mod, g1, yf, bg, z, o, conv_w[l], row(conv_b[l]),
                    w_fourier_out[l].astype(BF16), w_conv_out[l].astype(BF16), w_attn_out[l].astype(BF16),
                    w_l, row(b_in[l]), w_out[l].astype(BF16), with_ctx=with_ctx)
        tokens = (_ffn(x1, mod, row(norm2_g[l]), w_ffn_gate[l].astype(BF16), w_ffn_up[l].astype(BF16),
                       w_ffn_down[l].astype(BF16), row(final_g), last=last),)
    return tokens[0]
```

```python
import functools
import math

import numpy as np
import jax
import jax.numpy as jnp
from jax import lax
from jax.experimental import pallas as pl
from jax.experimental.pallas import tpu as pltpu

D_MODEL = 1024
BATCH = 8
SEQ = 2048
DEPTH = 2
GRID_W = 64
CTX_LEN = 256
FOURIER_GROUPS = 4
FOURIER_GROUP_DIM = 64
FOURIER_WIDTH = FOURIER_GROUPS * FOURIER_GROUP_DIM
CONV_WIDTH = 256
ATTN_HEADS = 4
QK_DIM = 64
V_DIM = 2 * QK_DIM
ATTN_QK_WIDTH = ATTN_HEADS * 2 * QK_DIM
ATTN_V_WIDTH = ATTN_HEADS * V_DIM
ROPE_BASE = 10000.0
ROPE_AXIS_DIM = QK_DIM // 2
N_BRANCH = 3
D_FF = 2816
EPS = 1e-6

F_OFF = 0
CB_OFF = F_OFF + FOURIER_WIDTH
CC_OFF = CB_OFF + CONV_WIDTH
CX_OFF = CC_OFF + CONV_WIDTH
Q_OFF = CX_OFF + CONV_WIDTH
K_OFF = Q_OFF + ATTN_QK_WIDTH
V_OFF = K_OFF + ATTN_QK_WIDTH
G_OFF = V_OFF + ATTN_V_WIDTH
D_IN = G_OFF + N_BRANCH * D_MODEL

S_ALL = SEQ + CTX_LEN
TM = 256
NT_ALL = S_ALL // TM
NT_LAT = SEQ // TM
MOD_ROWS = 16
CTX_MOD_ROW = BATCH
ADA_TN = 1536
SUBLANES = 8
LANES = 128
V7X_VMEM_LIMIT = 56 * 1024 * 1024
Q_SCALE = (QK_DIM ** -0.5) * math.log2(math.e)
BF16_SUBLANES = 16

BF16 = jnp.bfloat16
F32 = jnp.float32


def _dot(a, b):
    return jnp.dot(a, b, preferred_element_type=F32)


def _rms(x, g):
    return x * lax.rsqrt(jnp.mean(x * x, axis=-1, keepdims=True) + EPS) * g


def _norm_mod(x, g, shift, scale):
    return _rms(x, g) * (1.0 + scale) + shift


def _resident(shape):
    nd = len(shape)
    return pl.BlockSpec(shape, lambda *_: (0,) * nd, pipeline_mode=pl.Buffered(1))


def _layer_resident(shape, l):
    nd = len(shape)
    return pl.BlockSpec((None,) + tuple(shape), lambda *_: (l,) + (0,) * nd, pipeline_mode=pl.Buffered(1))


def _params(n_axes):
    return pltpu.CompilerParams(dimension_semantics=("arbitrary",) * n_axes,
                                vmem_limit_bytes=V7X_VMEM_LIMIT)


def _ada_kernel(cc_ref, w_ref, b_ref, o_ref):
    s = jax.nn.silu(cc_ref[...]).astype(BF16)
    o_ref[0] = _dot(s, w_ref[0].astype(BF16)) + b_ref[0]


def _ada(cc, w_ada, b_ada):
    n_col = 6 * D_MODEL
    return pl.pallas_call(
        _ada_kernel,
        grid=(DEPTH, n_col // ADA_TN),
        in_specs=[
            pl.BlockSpec((MOD_ROWS, D_MODEL), lambda l, j: (0, 0)),
            pl.BlockSpec((1, D_MODEL, ADA_TN), lambda l, j: (l, 0, j)),
            pl.BlockSpec((1, 1, ADA_TN), lambda l, j: (l, 0, j)),
        ],
        out_specs=pl.BlockSpec((1, MOD_ROWS, ADA_TN), lambda l, j: (l, 0, j)),
        out_shape=jax.ShapeDtypeStruct((DEPTH, MOD_ROWS, n_col), F32),
        compiler_params=_params(2),
        name="ada",
    )(cc, w_ada, b_ada.reshape(DEPTH, 1, n_col))


def _mod_spec(l):
    return pl.BlockSpec((1, 1, 6 * D_MODEL),
                        lambda b, i: (l * MOD_ROWS + jnp.where(i == NT_LAT, CTX_MOD_ROW, b), 0, 0))


def _token_specs(split):
    if not split:
        return [pl.BlockSpec((1, TM, D_MODEL), lambda b, i: (b, i, 0))]
    return [pl.BlockSpec((1, TM, D_MODEL), lambda b, i: (b, jnp.minimum(i, NT_LAT - 1), 0)),
            pl.BlockSpec((1, CTX_LEN, D_MODEL), lambda b, i: (b, 0, 0))]


def _load_tokens(tok_refs):
    if len(tok_refs) == 1:
        return tok_refs[0][0]
    x_ref, ctx_ref = tok_refs
    return jnp.where(pl.program_id(1) == NT_LAT, ctx_ref[0], x_ref[0])


def _sigmoid(t):
    return 0.5 * jnp.tanh(0.5 * t) + 0.5


def _rope(t, cos, sin_signed):
    lane = lax.broadcasted_iota(jnp.int32, t.shape, 1)
    first = (lane % (ROPE_AXIS_DIM)) < (ROPE_AXIS_DIM // 2)
    half = ROPE_AXIS_DIM // 2
    partner = jnp.where(first, pltpu.roll(t, LANES - half, axis=1), pltpu.roll(t, half, axis=1))
    return t * cos + partner * sin_signed


def _in_proj_kernel(*refs, n_tok):
    tok_refs = refs[:n_tok]
    mod_ref, g_ref, w_ref, b_ref, cos_ref, sin_ref, u_ref, bg_ref, z_ref, q_ref, k_ref, v_ref = refs[n_tok:]
    mod = mod_ref[0]
    h = _norm_mod(_load_tokens(tok_refs), g_ref[...], mod[:, 0:D_MODEL],
                  mod[:, D_MODEL:2 * D_MODEL]).astype(BF16)

    def proj(lo, hi):
        return _dot(h, w_ref[:, lo:hi]) + b_ref[:, lo:hi]

    u_ref[0] = proj(F_OFF, CB_OFF).astype(BF16)
    bg_ref[0] = proj(CB_OFF, CC_OFF)
    z_ref[0] = proj(CC_OFF, CX_OFF) * proj(CX_OFF, Q_OFF)
    cos = cos_ref[...]
    sin = sin_ref[...]
    q = proj(Q_OFF, K_OFF)
    k = proj(K_OFF, V_OFF)
    for hd in range(ATTN_HEADS):
        sl = slice(hd * V_DIM, (hd + 1) * V_DIM)
        q_ref[0, :, sl] = (_rope(q[:, sl], cos, sin) * Q_SCALE).astype(BF16)
        k_ref[0, :, sl] = _rope(k[:, sl], cos, sin).astype(BF16)
    v_ref[0] = proj(V_OFF, G_OFF).astype(BF16)


def _in_proj(l, tokens, mod, g, w, b, cos_t, sin_t):
    tok = lambda width: pl.BlockSpec((1, TM, width), lambda bb, i: (bb, i, 0))
    out = lambda width, dt: jax.ShapeDtypeStruct((BATCH, S_ALL, width), dt)
    return pl.pallas_call(
        functools.partial(_in_proj_kernel, n_tok=len(tokens)),
        grid=(BATCH, NT_ALL),
        in_specs=_token_specs(len(tokens) == 2) + [
            _mod_spec(l), _layer_resident((1, D_MODEL), l), _layer_resident((D_MODEL, G_OFF), l),
            _layer_resident((1, D_IN), l),
            pl.BlockSpec((TM, LANES), lambda bb, i: (i, 0)),
            pl.BlockSpec((TM, LANES), lambda bb, i: (i, 0)),
        ],
        out_specs=[tok(FOURIER_WIDTH), tok(CONV_WIDTH), tok(CONV_WIDTH),
                   tok(ATTN_QK_WIDTH), tok(ATTN_QK_WIDTH), tok(ATTN_V_WIDTH)],
        out_shape=[out(FOURIER_WIDTH, BF16), out(CONV_WIDTH, F32), out(CONV_WIDTH, F32),
                   out(ATTN_QK_WIDTH, BF16), out(ATTN_QK_WIDTH, BF16), out(ATTN_V_WIDTH, BF16)],
        compiler_params=_params(2),
        name="in_proj",
    )(*tokens, mod, g, w, b, cos_t, sin_t)


VT_ROWS = V_DIM + BF16_SUBLANES
TK = TM


def _attn_kernel(lamv_ref, sub_ref, q_ref, k_ref, v_ref, o_ref, vt_ref, st0_ref, st1_ref, e0_ref, e1_ref, *,
                 lambda_init, with_ctx):
    st_refs = (st0_ref, st1_ref)
    e_refs = (e0_ref, e1_ref)
    vt_ref[0:V_DIM, :] = v_ref[0].astype(F32).T.astype(BF16)
    r = lax.broadcasted_iota(jnp.int32, (BF16_SUBLANES, S_ALL), 0)
    vt_ref[V_DIM:VT_ROWS, :] = jnp.where(r == 0, 1.0, 0.0).astype(BF16)

    lv = lamv_ref[...]
    lam = (jnp.exp(jnp.sum(lv[0:1] * lv[1:2], axis=-1, keepdims=True))
           - jnp.exp(jnp.sum(lv[2:3] * lv[3:4], axis=-1, keepdims=True)) + lambda_init)
    groups = TK // SUBLANES
    all_chunks = tuple(range(S_ALL // TK))
    ctx_chunks = tuple(range(SEQ // TK, S_ALL // TK))

    def rows(t):
        return pl.ds(t * TM, TM)

    def scores(t, slot, chunks, col_max):
        qt = q_ref[0, rows(t), :].astype(F32).T
        row = lax.broadcasted_iota(jnp.int32, qt.shape, 0)
        zero = jnp.zeros_like(qt)
        qq = jnp.concatenate([jnp.where(row < QK_DIM, qt, zero), jnp.where(row >= QK_DIM, qt, zero)],
                             axis=1).astype(BF16)
        m8 = None
        for j in chunks:
            s = _dot(k_ref[0, j * TK:(j + 1) * TK, :], qq)
            st_refs[slot][j * TK:(j + 1) * TK, :] = s
            c = jnp.max(s.reshape(groups, SUBLANES, 2 * TM), axis=0)
            m8 = c if m8 is None else jnp.maximum(m8, c)
            yield
        col_max.append(m8)

    def exps(slot, chunks, m8):
        mb = jnp.broadcast_to(jnp.max(m8, axis=0, keepdims=True), (SUBLANES, 2 * TM))
        for j in chunks:
            s = st_refs[slot][j * TK:(j + 1) * TK, :]
            e = jnp.exp2(s.reshape(groups, SUBLANES, 2 * TM) - mb[None]).reshape(TK, 2 * TM)
            e_refs[slot][j * TK:(j + 1) * TK, :] = e.astype(BF16)
            yield

    def values(t, slot, chunks):
        lo, hi = chunks[0] * TK, (chunks[-1] + 1) * TK
        acc = _dot(vt_ref[:, lo:hi], e_refs[slot][lo:hi, :])
        on = acc[0:V_DIM] * (1.0 / acc[V_DIM:V_DIM + 1])
        o_t = on[:, 0:TM] - lam * on[:, TM:2 * TM]
        o_ref[0, rows(t), :] = (_rms(o_t.T, sub_ref[...]) * (1.0 - lambda_init)).astype(o_ref.dtype)

    def run(*gens):
        gens = list(gens)
        while gens:
            gens = [g for g in gens if next(g, "end") != "end"]

    tiles = [(t, all_chunks) for t in range(NT_LAT)] + ([(NT_LAT, ctx_chunks)] if with_ctx else [])
    m = []
    for step in range(len(tiles) + 2):
        if step >= 2:
            t, chunks = tiles[step - 2]
            values(t, t % 2, chunks)
        gens = []
        if step < len(tiles):
            t, chunks = tiles[step]
            gens.append(scores(t, t % 2, chunks, m))
        if 1 <= step <= len(tiles):
            t, chunks = tiles[step - 1]
            gens.append(exps(t % 2, chunks, m[t]))
        run(*gens)


def _attention(l, lamv, subln_g, q, k, v, *, lambda_init, with_ctx):
    n_q = NT_ALL if with_ctx else NT_LAT
    kv = pl.BlockSpec((1, S_ALL, V_DIM), lambda b, h: (b, 0, h))
    return pl.pallas_call(
        functools.partial(_attn_kernel, lambda_init=lambda_init, with_ctx=with_ctx),
        grid=(BATCH, ATTN_HEADS),
        in_specs=[_layer_resident((4, QK_DIM), l), _layer_resident((1, V_DIM), l), kv, kv, kv],
        out_specs=pl.BlockSpec((1, n_q * TM, V_DIM), lambda b, h: (b, 0, h)),
        out_shape=jax.ShapeDtypeStruct((BATCH, n_q * TM, ATTN_V_WIDTH), BF16),
        scratch_shapes=[pltpu.VMEM((VT_ROWS, S_ALL), BF16), pltpu.VMEM((S_ALL, 2 * TM), F32),
                        pltpu.VMEM((S_ALL, 2 * TM), F32), pltpu.VMEM((S_ALL, 2 * TM), BF16),
                        pltpu.VMEM((S_ALL, 2 * TM), BF16)],
        compiler_params=_params(2),
        name="attention",
    )(lamv, subln_g, q, k, v)


def _dft_cos_sin(n):
    j = np.arange(n, dtype=np.int64)
    ang = 2.0 * np.pi * ((j[:, None] * j[None, :]) % n).astype(np.float64) / n
    return np.cos(ang).astype(np.float32), np.sin(ang).astype(np.float32)


def _fourier_kernel(u_ref, bdc_ref, bds_ref, cl_ref, sl_ref, cc_ref, sc_ref, y_ref, *, with_ctx):
    ub = u_ref[0]
    tc = _dot(ub, bdc_ref[...]).astype(BF16)
    ts = _dot(ub, bds_ref[...]).astype(BF16)
    y = _dot(cl_ref[...], tc[0:SEQ]) - _dot(sl_ref[...], ts[0:SEQ])
    y_ref[0, 0:SEQ, :] = (y * (1.0 / math.sqrt(SEQ * FOURIER_GROUP_DIM))).astype(BF16)
    if with_ctx:
        yc = _dot(cc_ref[...], tc[SEQ:S_ALL]) - _dot(sc_ref[...], ts[SEQ:S_ALL])
        y_ref[0, SEQ:S_ALL, :] = (yc * (1.0 / math.sqrt(CTX_LEN * FOURIER_GROUP_DIM))).astype(BF16)


def _fourier(u, mats, *, with_ctx):
    rows = S_ALL if with_ctx else SEQ
    return pl.pallas_call(
        functools.partial(_fourier_kernel, with_ctx=with_ctx),
        grid=(BATCH,),
        in_specs=[pl.BlockSpec((1, S_ALL, FOURIER_WIDTH), lambda b: (b, 0, 0))]
                 + [_resident(m.shape) for m in mats],
        out_specs=pl.BlockSpec((1, rows, FOURIER_WIDTH), lambda b: (b, 0, 0)),
        out_shape=jax.ShapeDtypeStruct((BATCH, rows, FOURIER_WIDTH), BF16),
        compiler_params=_params(1),
        name="fourier",
    )(u, *mats)


def _merge_kernel(*refs, n_tok):
    tok_refs = refs[:n_tok]
    (mod_ref, g_ref, yf_ref, bg_ref, z_ref, zp_ref, zn_ref, o_ref, cw_ref, cb_ref, wfo_ref, wco_ref, wao_ref,
     win_ref, bin_ref, wout_ref, out_ref) = refs[n_tok:]
    i = pl.program_id(1)
    mod = mod_ref[0]
    x = _load_tokens(tok_refs)
    z = z_ref[0]
    row = lax.broadcasted_iota(jnp.int32, z.shape, 0)
    has_prev = jnp.logical_and(i >= 1, i < NT_LAT)
    has_next = i < NT_LAT - 1
    prev_row = jnp.where(has_prev, zp_ref[0, SUBLANES - 1:SUBLANES, :], 0.0)
    next_row = jnp.where(has_next, zn_ref[0, 0:1, :], 0.0)
    z_prev = jnp.where(row == 0, prev_row, pltpu.roll(z, 1, axis=0))
    z_next = jnp.where(row == TM - 1, next_row, pltpu.roll(z, TM - 1, axis=0))
    cw = cw_ref[...]
    conv = z_prev * cw[0:1] + z * cw[1:2] + z_next * cw[2:3] + cb_ref[...]
    y_c = _dot((bg_ref[0] * conv).astype(BF16), wco_ref[...])
    y_f = _dot(yf_ref[0], wfo_ref[...])
    y_a = _dot(o_ref[0], wao_ref[...])

    h = _norm_mod(x, g_ref[...], mod[:, 0:D_MODEL], mod[:, D_MODEL:2 * D_MODEL]).astype(BF16)

    def gate(n):
        lo = G_OFF + n * D_MODEL
        return _sigmoid(_dot(h, win_ref[:, lo:lo + D_MODEL]) + bin_ref[:, lo:lo + D_MODEL])

    y = gate(0) * y_f + gate(1) * y_c + gate(2) * y_a
    out_ref[0] = x + mod[:, 2 * D_MODEL:3 * D_MODEL] * _dot(y.astype(BF16), wout_ref[...])


def _merge(l, tokens, mod, g, yf, bg, z, o, cw, cb, wfo, wco, wao, w_in, b_in, wout, *, with_ctx):
    n_t = NT_ALL if with_ctx else NT_LAT
    tok = lambda width: pl.BlockSpec((1, TM, width), lambda b, i: (b, i, 0))
    per8 = TM // SUBLANES
    halo_prev = pl.BlockSpec((1, SUBLANES, CONV_WIDTH),
                             lambda b, i: (b, jnp.maximum(i * per8 - 1, 0), 0))
    halo_next = pl.BlockSpec((1, SUBLANES, CONV_WIDTH),
                             lambda b, i: (b, jnp.minimum((i + 1) * per8, S_ALL // SUBLANES - 1), 0))
    weights = [cw, cb, wfo, wco, wao, w_in, b_in, wout]
    return pl.pallas_call(
        functools.partial(_merge_kernel, n_tok=len(tokens)),
        grid=(BATCH, n_t),
        in_specs=_token_specs(len(tokens) == 2)
                 + [_mod_spec(l), _layer_resident((1, D_MODEL), l), tok(FOURIER_WIDTH), tok(CONV_WIDTH),
                    tok(CONV_WIDTH), halo_prev, halo_next, tok(ATTN_V_WIDTH)]
                 + [_layer_resident(w.shape[1:], l) for w in weights],
        out_specs=tok(D_MODEL),
        out_shape=jax.ShapeDtypeStruct((BATCH, n_t * TM, D_MODEL), F32),
        compiler_params=_params(2),
        name="merge",
    )(*tokens, mod, g, yf, bg, z, z, z, o, *weights)


def _ffn_kernel(x_ref, mod_ref, g_ref, wg_ref, wu_ref, wd_ref, fg_ref, out_ref, *, last):
    mod = mod_ref[0]
    x = x_ref[0]
    h = _norm_mod(x, g_ref[...], mod[:, 3 * D_MODEL:4 * D_MODEL],
                  mod[:, 4 * D_MODEL:5 * D_MODEL]).astype(BF16)
    gate = _dot(h, wg_ref[...])
    a = (gate * _sigmoid(gate) * _dot(h, wu_ref[...])).astype(BF16)
    x2 = x + mod[:, 5 * D_MODEL:6 * D_MODEL] * _dot(a, wd_ref[...])
    out_ref[0] = _rms(x2, fg_ref[...]) if last else x2


def _ffn(l, x1, mod, g, wg, wu, wd, fg, *, last):
    n_t = x1.shape[1] // TM
    tok = pl.BlockSpec((1, TM, D_MODEL), lambda b, i: (b, i, 0))
    return pl.pallas_call(
        functools.partial(_ffn_kernel, last=last),
        grid=(BATCH, n_t),
        in_specs=[tok, _mod_spec(l), _layer_resident((1, D_MODEL), l), _layer_resident(wg.shape[1:], l),
                  _layer_resident(wu.shape[1:], l), _layer_resident(wd.shape[1:], l), _resident((1, D_MODEL))],
        out_specs=tok,
        out_shape=jax.ShapeDtypeStruct(x1.shape, F32),
        compiler_params=_params(2),
        name="ffn",
    )(x1, mod, g, wg, wu, wd, fg)


def _rope_tables():
    pos = jnp.arange(SEQ)
    freqs = ROPE_BASE ** (-jnp.arange(0, ROPE_AXIS_DIM, 2, dtype=F32) / ROPE_AXIS_DIM)
    ang_r = (pos // GRID_W).astype(F32)[:, None] * freqs[None, :]
    ang_c = (pos % GRID_W).astype(F32)[:, None] * freqs[None, :]
    cos64 = jnp.concatenate([jnp.cos(ang_r)] * 2 + [jnp.cos(ang_c)] * 2, axis=-1)
    sin64 = jnp.concatenate([-jnp.sin(ang_r), jnp.sin(ang_r), -jnp.sin(ang_c), jnp.sin(ang_c)], axis=-1)
    cos_t = jnp.concatenate([jnp.tile(cos64, (1, 2)), jnp.ones((CTX_LEN, LANES), F32)], axis=0)
    sin_t = jnp.concatenate([jnp.tile(sin64, (1, 2)), jnp.zeros((CTX_LEN, LANES), F32)], axis=0)
    return cos_t, sin_t


def _fourier_mats():
    cg, sg = _dft_cos_sin(FOURIER_GROUP_DIM)
    eye = np.eye(FOURIER_GROUPS, dtype=np.float32)
    mats = [np.kron(eye, cg), np.kron(eye, sg), *_dft_cos_sin(SEQ), *_dft_cos_sin(CTX_LEN)]
    return [jnp.asarray(m).astype(BF16) for m in mats]


def kernel(x, c, ctx, c_ctx, w_ada, b_ada, norm1_g, norm2_g, w_in, b_in, conv_w, conv_b, w_fourier_out,
           w_conv_out, w_attn_out, lambda_q1, lambda_k1, lambda_q2, lambda_k2, subln_g, w_out, w_ffn_gate,
           w_ffn_up, w_ffn_down, final_g):
    cc = jnp.concatenate([c, c_ctx[None, :], jnp.zeros((MOD_ROWS - BATCH - 1, D_MODEL), F32)], axis=0)
    mod = _ada(cc, w_ada, b_ada).reshape(DEPTH * MOD_ROWS, 1, 6 * D_MODEL)
    cos_t, sin_t = _rope_tables()
    mats = _fourier_mats()
    vec = lambda a: a.reshape(DEPTH, 1, -1)
    w_in_b, w_fo_b, w_co_b, w_ao_b, w_out_b, w_g_b, w_u_b, w_d_b = (
        a.astype(BF16) for a in (w_in, w_fourier_out, w_conv_out, w_attn_out, w_out, w_ffn_gate, w_ffn_up,
                                 w_ffn_down))
    lamv = jnp.stack([lambda_q1, lambda_k1, lambda_q2, lambda_k2], axis=1).astype(F32)
    g1, g2, b_in3, conv_b3, subln3 = vec(norm1_g), vec(norm2_g), vec(b_in), vec(conv_b), vec(subln_g)
    tokens = (x, ctx)

    for l in range(DEPTH):
        last = l == DEPTH - 1
        with_ctx = not last
        lambda_init = 0.8 - 0.6 * math.exp(-0.3 * l)
        u, bg, z, q, k, v = _in_proj(l, tokens, mod, g1, w_in_b, b_in3, cos_t, sin_t)
        o = _attention(l, lamv, subln3, q, k, v, lambda_init=lambda_init, with_ctx=with_ctx)
        yf = _fourier(u, mats, with_ctx=with_ctx)
        x1 = _merge(l, tokens, mod, g1, yf, bg, z, o, conv_w, conv_b3, w_fo_b, w_co_b, w_ao_b, w_in_b, b_in3,
                    w_out_b, with_ctx=with_ctx)
        tokens = (_ffn(l, x1, mod, g2, w_g_b, w_u_b, w_d_b, final_g.reshape(1, -1), last=last),)
    return tokens[0]
```

```python
import functools
import math

import numpy as np
import jax
import jax.numpy as jnp
from jax import lax
from jax.experimental import pallas as pl
from jax.experimental.pallas import tpu as pltpu

D_MODEL = 1024
BATCH = 8
SEQ = 2048
DEPTH = 2
GRID_W = 64
CTX_LEN = 256
FOURIER_GROUPS = 4
FOURIER_GROUP_DIM = 64
FOURIER_WIDTH = FOURIER_GROUPS * FOURIER_GROUP_DIM
CONV_WIDTH = 256
ATTN_HEADS = 4
QK_DIM = 64
V_DIM = 2 * QK_DIM
ATTN_QK_WIDTH = ATTN_HEADS * 2 * QK_DIM
ATTN_V_WIDTH = ATTN_HEADS * V_DIM
ROPE_BASE = 10000.0
ROPE_AXIS_DIM = QK_DIM // 2
N_BRANCH = 3
D_FF = 2816
EPS = 1e-6

F_OFF = 0
CB_OFF = F_OFF + FOURIER_WIDTH
CC_OFF = CB_OFF + CONV_WIDTH
CX_OFF = CC_OFF + CONV_WIDTH
Q_OFF = CX_OFF + CONV_WIDTH
K_OFF = Q_OFF + ATTN_QK_WIDTH
V_OFF = K_OFF + ATTN_QK_WIDTH
G_OFF = V_OFF + ATTN_V_WIDTH
D_IN = G_OFF + N_BRANCH * D_MODEL

S_ALL = SEQ + CTX_LEN
TM = 256
NT_ALL = S_ALL // TM
NT_LAT = SEQ // TM
MOD_ROWS = 16
CTX_MOD_ROW = BATCH
ADA_TN = 1536
SUBLANES = 8
LANES = 128
V7X_VMEM_LIMIT = 56 * 1024 * 1024
Q_SCALE = (QK_DIM ** -0.5) * math.log2(math.e)
BF16_SUBLANES = 16

BF16 = jnp.bfloat16
F32 = jnp.float32


def _dot(a, b):
    return jnp.dot(a, b, preferred_element_type=F32)


def _rms(x, g):
    return x * lax.rsqrt(jnp.mean(x * x, axis=-1, keepdims=True) + EPS) * g


def _norm_mod(x, g, shift, scale):
    return _rms(x, g) * (1.0 + scale) + shift


def _resident(shape):
    nd = len(shape)
    return pl.BlockSpec(shape, lambda *_: (0,) * nd, pipeline_mode=pl.Buffered(1))


def _layer_resident(shape, l):
    nd = len(shape)
    return pl.BlockSpec((None,) + tuple(shape), lambda *_: (l,) + (0,) * nd, pipeline_mode=pl.Buffered(1))


def _params(n_axes):
    return pltpu.CompilerParams(dimension_semantics=("arbitrary",) * n_axes,
                                vmem_limit_bytes=V7X_VMEM_LIMIT)


def _ada_kernel(cc_ref, w_ref, b_ref, o_ref):
    s = jax.nn.silu(cc_ref[...]).astype(BF16)
    o_ref[0] = _dot(s, w_ref[0].astype(BF16)) + b_ref[0]


def _ada(cc, w_ada, b_ada):
    n_col = 6 * D_MODEL
    return pl.pallas_call(
        _ada_kernel,
        grid=(DEPTH, n_col // ADA_TN),
        in_specs=[
            pl.BlockSpec((MOD_ROWS, D_MODEL), lambda l, j: (0, 0)),
            pl.BlockSpec((1, D_MODEL, ADA_TN), lambda l, j: (l, 0, j)),
            pl.BlockSpec((1, 1, ADA_TN), lambda l, j: (l, 0, j)),
        ],
        out_specs=pl.BlockSpec((1, MOD_ROWS, ADA_TN), lambda l, j: (l, 0, j)),
        out_shape=jax.ShapeDtypeStruct((DEPTH, MOD_ROWS, n_col), F32),
        compiler_params=_params(2),
        name="ada",
    )(cc, w_ada, b_ada.reshape(DEPTH, 1, n_col))


def _mod_spec(l):
    return pl.BlockSpec((1, 1, 6 * D_MODEL),
                        lambda b, i: (l * MOD_ROWS + jnp.where(i == NT_LAT, CTX_MOD_ROW, b), 0, 0))


def _token_specs(split):
    if not split:
        return [pl.BlockSpec((1, TM, D_MODEL), lambda b, i: (b, i, 0))]
    return [pl.BlockSpec((1, TM, D_MODEL), lambda b, i: (b, jnp.minimum(i, NT_LAT - 1), 0)),
            pl.BlockSpec((1, CTX_LEN, D_MODEL), lambda b, i: (b, 0, 0))]


def _load_tokens(tok_refs):
    if len(tok_refs) == 1:
        return tok_refs[0][0]
    x_ref, ctx_ref = tok_refs
    return jnp.where(pl.program_id(1) == NT_LAT, ctx_ref[0], x_ref[0])


def _sigmoid(t):
    return 0.5 * jnp.tanh(0.5 * t) + 0.5


def _rope(t, cos, sin_signed):
    lane = lax.broadcasted_iota(jnp.int32, t.shape, 1)
    first = (lane % (ROPE_AXIS_DIM)) < (ROPE_AXIS_DIM // 2)
    half = ROPE_AXIS_DIM // 2
    partner = jnp.where(first, pltpu.roll(t, LANES - half, axis=1), pltpu.roll(t, half, axis=1))
    return t * cos + partner * sin_signed


def _in_proj_kernel(*refs, n_tok):
    tok_refs = refs[:n_tok]
    mod_ref, g_ref, w_ref, b_ref, cos_ref, sin_ref, u_ref, bg_ref, z_ref, q_ref, k_ref, v_ref = refs[n_tok:]
    mod = mod_ref[0]
    h = _norm_mod(_load_tokens(tok_refs), g_ref[...], mod[:, 0:D_MODEL],
                  mod[:, D_MODEL:2 * D_MODEL]).astype(BF16)

    def proj(lo, hi):
        return _dot(h, w_ref[:, lo:hi]) + b_ref[:, lo:hi]

    u_ref[0] = proj(F_OFF, CB_OFF).astype(BF16)
    bg_ref[0] = proj(CB_OFF, CC_OFF)
    z_ref[0] = proj(CC_OFF, CX_OFF) * proj(CX_OFF, Q_OFF)
    cos = cos_ref[...]
    sin = sin_ref[...]
    q = proj(Q_OFF, K_OFF)
    k = proj(K_OFF, V_OFF)
    for hd in range(ATTN_HEADS):
        sl = slice(hd * V_DIM, (hd + 1) * V_DIM)
        q_ref[0, :, sl] = (_rope(q[:, sl], cos, sin) * Q_SCALE).astype(BF16)
        k_ref[0, :, sl] = _rope(k[:, sl], cos, sin).astype(BF16)
    v_ref[0] = proj(V_OFF, G_OFF).astype(BF16)


def _in_proj(l, tokens, mod, g, w, b, cos_t, sin_t):
    tok = lambda width: pl.BlockSpec((1, TM, width), lambda bb, i: (bb, i, 0))
    out = lambda width, dt: jax.ShapeDtypeStruct((BATCH, S_ALL, width), dt)
    return pl.pallas_call(
        functools.partial(_in_proj_kernel, n_tok=len(tokens)),
        grid=(BATCH, NT_ALL),
        in_specs=_token_specs(len(tokens) == 2) + [
            _mod_spec(l), _layer_resident((1, D_MODEL), l), _layer_resident((D_MODEL, G_OFF), l),
            _layer_resident((1, D_IN), l),
            pl.BlockSpec((TM, LANES), lambda bb, i: (i, 0)),
            pl.BlockSpec((TM, LANES), lambda bb, i: (i, 0)),
        ],
        out_specs=[tok(FOURIER_WIDTH), tok(CONV_WIDTH), tok(CONV_WIDTH),
                   tok(ATTN_QK_WIDTH), tok(ATTN_QK_WIDTH), tok(ATTN_V_WIDTH)],
        out_shape=[out(FOURIER_WIDTH, BF16), out(CONV_WIDTH, F32), out(CONV_WIDTH, F32),
                   out(ATTN_QK_WIDTH, BF16), out(ATTN_QK_WIDTH, BF16), out(ATTN_V_WIDTH, BF16)],
        compiler_params=_params(2),
        name="in_proj",
    )(*tokens, mod, g, w, b, cos_t, sin_t)


VT_ROWS = V_DIM + BF16_SUBLANES
TK = TM
QK_LEAD = 2


def _attn_kernel(lamv_ref, sub_ref, q_ref, k_ref, v_ref, o_ref, vt_ref, *, lambda_init, with_ctx):
    vt_ref[0:V_DIM, :] = v_ref[0].astype(F32).T.astype(BF16)
    r = lax.broadcasted_iota(jnp.int32, (BF16_SUBLANES, S_ALL), 0)
    vt_ref[V_DIM:VT_ROWS, :] = jnp.where(r == 0, 1.0, 0.0).astype(BF16)

    lv = lamv_ref[...]
    lam = (jnp.exp(jnp.sum(lv[0:1] * lv[1:2], axis=-1, keepdims=True))
           - jnp.exp(jnp.sum(lv[2:3] * lv[3:4], axis=-1, keepdims=True)) + lambda_init)
    groups = TK // SUBLANES
    all_chunks = tuple(range(S_ALL // TK))
    ctx_chunks = tuple(range(SEQ // TK, S_ALL // TK))

    def rows(t):
        return pl.ds(t * TM, TM)

    def bcast8(c8):
        return jnp.broadcast_to(jnp.max(c8, axis=0, keepdims=True), (SUBLANES, 2 * TM))

    def start_tile(t):
        qt = q_ref[0, rows(t), :].astype(F32).T
        row = lax.broadcasted_iota(jnp.int32, qt.shape, 0)
        zero = jnp.zeros_like(qt)
        qq = jnp.concatenate([jnp.where(row < QK_DIM, qt, zero), jnp.where(row >= QK_DIM, qt, zero)],
                             axis=1).astype(BF16)
        return dict(qq=qq, s={}, cmax=[], ref=None, acc=None)

    state = {}

    def emit_scores(t, idx, chunks):
        if idx == 0:
            state[t] = start_tile(t)
        d = state[t]
        j = chunks[idx]
        sq = _dot(k_ref[0, j * TK:(j + 1) * TK, :], d["qq"]).reshape(groups, SUBLANES, 2 * TM)
        d["s"][idx] = sq
        d["cmax"].append(jnp.max(sq, axis=0))

    def emit_mix(t, idx, chunks):
        d = state[t]
        j = chunks[idx]
        s = d["s"].pop(idx)
        ref = d["ref"]
        new_ref = bcast8(d["cmax"][idx])
        if idx > 0:
            new_ref = jnp.maximum(ref, new_ref)
        e = jnp.exp2(s - new_ref[None]).reshape(TK, 2 * TM).astype(BF16)
        o = _dot(vt_ref[:, j * TK:(j + 1) * TK], e)
        if idx == 0:
            acc = o
        else:
            alpha = jnp.exp2(ref - new_ref)
            acc = (d["acc"].reshape(VT_ROWS // SUBLANES, SUBLANES, 2 * TM) * alpha[None]).reshape(VT_ROWS, 2 * TM) + o
        d["ref"], d["acc"] = new_ref, acc
        if idx == len(chunks) - 1:
            on = acc[0:V_DIM] * (1.0 / acc[V_DIM:V_DIM + 1])
            o_t = on[:, 0:TM] - lam * on[:, TM:2 * TM]
            o_ref[0, rows(t), :] = (_rms(o_t.T, sub_ref[...]) * (1.0 - lambda_init)).astype(o_ref.dtype)
            del state[t]

    tiles = [(t, all_chunks) for t in range(NT_LAT)] + ([(NT_LAT, ctx_chunks)] if with_ctx else [])
    items = [(t, idx, chunks) for t, chunks in tiles for idx in range(len(chunks))]
    for g in range(len(items) + QK_LEAD):
        if g < len(items):
            emit_scores(*items[g])
        if g >= QK_LEAD:
            emit_mix(*items[g - QK_LEAD])


def _attention(l, lamv, subln_g, q, k, v, *, lambda_init, with_ctx):
    n_q = NT_ALL if with_ctx else NT_LAT
    kv = pl.BlockSpec((1, S_ALL, V_DIM), lambda b, h: (b, 0, h))
    return pl.pallas_call(
        functools.partial(_attn_kernel, lambda_init=lambda_init, with_ctx=with_ctx),
        grid=(BATCH, ATTN_HEADS),
        in_specs=[_layer_resident((4, QK_DIM), l), _layer_resident((1, V_DIM), l), kv, kv, kv],
        out_specs=pl.BlockSpec((1, n_q * TM, V_DIM), lambda b, h: (b, 0, h)),
        out_shape=jax.ShapeDtypeStruct((BATCH, n_q * TM, ATTN_V_WIDTH), BF16),
        scratch_shapes=[pltpu.VMEM((VT_ROWS, S_ALL), BF16)],
        compiler_params=_params(2),
        name="attention",
    )(lamv, subln_g, q, k, v)


def _dft_cos_sin(n):
    j = np.arange(n, dtype=np.int64)
    ang = 2.0 * np.pi * ((j[:, None] * j[None, :]) % n).astype(np.float64) / n
    return np.cos(ang).astype(np.float32), np.sin(ang).astype(np.float32)


def _fourier_kernel(u_ref, bdc_ref, bds_ref, cl_ref, sl_ref, cc_ref, sc_ref, y_ref, *, with_ctx):
    ub = u_ref[0]
    tc = _dot(ub, bdc_ref[...]).astype(BF16)
    ts = _dot(ub, bds_ref[...]).astype(BF16)
    y = _dot(cl_ref[...], tc[0:SEQ]) - _dot(sl_ref[...], ts[0:SEQ])
    y_ref[0, 0:SEQ, :] = (y * (1.0 / math.sqrt(SEQ * FOURIER_GROUP_DIM))).astype(BF16)
    if with_ctx:
        yc = _dot(cc_ref[...], tc[SEQ:S_ALL]) - _dot(sc_ref[...], ts[SEQ:S_ALL])
        y_ref[0, SEQ:S_ALL, :] = (yc * (1.0 / math.sqrt(CTX_LEN * FOURIER_GROUP_DIM))).astype(BF16)


def _fourier(u, mats, *, with_ctx):
    rows = S_ALL if with_ctx else SEQ
    return pl.pallas_call(
        functools.partial(_fourier_kernel, with_ctx=with_ctx),
        grid=(BATCH,),
        in_specs=[pl.BlockSpec((1, S_ALL, FOURIER_WIDTH), lambda b: (b, 0, 0))]
                 + [_resident(m.shape) for m in mats],
        out_specs=pl.BlockSpec((1, rows, FOURIER_WIDTH), lambda b: (b, 0, 0)),
        out_shape=jax.ShapeDtypeStruct((BATCH, rows, FOURIER_WIDTH), BF16),
        compiler_params=_params(1),
        name="fourier",
    )(u, *mats)


def _merge_kernel(*refs, n_tok):
    tok_refs = refs[:n_tok]
    (mod_ref, g_ref, yf_ref, bg_ref, z_ref, zp_ref, zn_ref, o_ref, cw_ref, cb_ref, wfo_ref, wco_ref, wao_ref,
     win_ref, bin_ref, wout_ref, out_ref) = refs[n_tok:]
    i = pl.program_id(1)
    mod = mod_ref[0]
    x = _load_tokens(tok_refs)
    y_a = _dot(o_ref[0], wao_ref[...])
    y_f = _dot(yf_ref[0], wfo_ref[...])
    h = _norm_mod(x, g_ref[...], mod[:, 0:D_MODEL], mod[:, D_MODEL:2 * D_MODEL]).astype(BF16)

    def gate(n):
        lo = G_OFF + n * D_MODEL
        return _sigmoid(_dot(h, win_ref[:, lo:lo + D_MODEL]) + bin_ref[:, lo:lo + D_MODEL])

    y = gate(2) * y_a + gate(0) * y_f
    z = z_ref[0]
    row = lax.broadcasted_iota(jnp.int32, z.shape, 0)
    has_prev = jnp.logical_and(i >= 1, i < NT_LAT)
    has_next = i < NT_LAT - 1
    prev_row = jnp.where(has_prev, zp_ref[0, SUBLANES - 1:SUBLANES, :], 0.0)
    next_row = jnp.where(has_next, zn_ref[0, 0:1, :], 0.0)
    z_prev = jnp.where(row == 0, prev_row, pltpu.roll(z, 1, axis=0))
    z_next = jnp.where(row == TM - 1, next_row, pltpu.roll(z, TM - 1, axis=0))
    cw = cw_ref[...]
    conv = z_prev * cw[0:1] + z * cw[1:2] + z_next * cw[2:3] + cb_ref[...]
    y_c = _dot((bg_ref[0] * conv).astype(BF16), wco_ref[...])
    y = y + gate(1) * y_c
    out_ref[0] = x + mod[:, 2 * D_MODEL:3 * D_MODEL] * _dot(y.astype(BF16), wout_ref[...])


def _merge(l, tokens, mod, g, yf, bg, z, o, cw, cb, wfo, wco, wao, w_in, b_in, wout, *, with_ctx):
    n_t = NT_ALL if with_ctx else NT_LAT
    tok = lambda width: pl.BlockSpec((1, TM, width), lambda b, i: (b, i, 0))
    per8 = TM // SUBLANES
    halo_prev = pl.BlockSpec((1, SUBLANES, CONV_WIDTH),
                             lambda b, i: (b, jnp.maximum(i * per8 - 1, 0), 0))
    halo_next = pl.BlockSpec((1, SUBLANES, CONV_WIDTH),
                             lambda b, i: (b, jnp.minimum((i + 1) * per8, S_ALL // SUBLANES - 1), 0))
    weights = [cw, cb, wfo, wco, wao, w_in, b_in, wout]
    return pl.pallas_call(
        functools.partial(_merge_kernel, n_tok=len(tokens)),
        grid=(BATCH, n_t),
        in_specs=_token_specs(len(tokens) == 2)
                 + [_mod_spec(l), _layer_resident((1, D_MODEL), l), tok(FOURIER_WIDTH), tok(CONV_WIDTH),
                    tok(CONV_WIDTH), halo_prev, halo_next, tok(ATTN_V_WIDTH)]
                 + [_layer_resident(w.shape[1:], l) for w in weights],
        out_specs=tok(D_MODEL),
        out_shape=jax.ShapeDtypeStruct((BATCH, n_t * TM, D_MODEL), F32),
        compiler_params=_params(2),
        name="merge",
    )(*tokens, mod, g, yf, bg, z, z, z, o, *weights)


def _ffn_kernel(x_ref, mod_ref, g_ref, wg_ref, wu_ref, wd_ref, fg_ref, out_ref, *, last):
    mod = mod_ref[0]
    x = x_ref[0]
    h = _norm_mod(x, g_ref[...], mod[:, 3 * D_MODEL:4 * D_MODEL],
                  mod[:, 4 * D_MODEL:5 * D_MODEL]).astype(BF16)
    gate = _dot(h, wg_ref[...])
    a = (gate * _sigmoid(gate) * _dot(h, wu_ref[...])).astype(BF16)
    x2 = x + mod[:, 5 * D_MODEL:6 * D_MODEL] * _dot(a, wd_ref[...])
    out_ref[0] = _rms(x2, fg_ref[...]) if last else x2


def _ffn(l, x1, mod, g, wg, wu, wd, fg, *, last):
    n_t = x1.shape[1] // TM
    tok = pl.BlockSpec((1, TM, D_MODEL), lambda b, i: (b, i, 0))
    return pl.pallas_call(
        functools.partial(_ffn_kernel, last=last),
        grid=(BATCH, n_t),
        in_specs=[tok, _mod_spec(l), _layer_resident((1, D_MODEL), l), _layer_resident(wg.shape[1:], l),
                  _layer_resident(wu.shape[1:], l), _layer_resident(wd.shape[1:], l), _resident((1, D_MODEL))],
        out_specs=tok,
        out_shape=jax.ShapeDtypeStruct(x1.shape, F32),
        compiler_params=_params(2),
        name="ffn",
    )(x1, mod, g, wg, wu, wd, fg)


def _rope_tables():
    pos = jnp.arange(SEQ)
    freqs = ROPE_BASE ** (-jnp.arange(0, ROPE_AXIS_DIM, 2, dtype=F32) / ROPE_AXIS_DIM)
    ang_r = (pos // GRID_W).astype(F32)[:, None] * freqs[None, :]
    ang_c = (pos % GRID_W).astype(F32)[:, None] * freqs[None, :]
    cos64 = jnp.concatenate([jnp.cos(ang_r)] * 2 + [jnp.cos(ang_c)] * 2, axis=-1)
    sin64 = jnp.concatenate([-jnp.sin(ang_r), jnp.sin(ang_r), -jnp.sin(ang_c), jnp.sin(ang_c)], axis=-1)
    cos_t = jnp.concatenate([jnp.tile(cos64, (1, 2)), jnp.ones((CTX_LEN, LANES), F32)], axis=0)
    sin_t = jnp.concatenate([jnp.tile(sin64, (1, 2)), jnp.zeros((CTX_LEN, LANES), F32)], axis=0)
    return cos_t, sin_t


def _fourier_mats():
    cg, sg = _dft_cos_sin(FOURIER_GROUP_DIM)
    eye = np.eye(FOURIER_GROUPS, dtype=np.float32)
    mats = [np.kron(eye, cg), np.kron(eye, sg), *_dft_cos_sin(SEQ), *_dft_cos_sin(CTX_LEN)]
    return [jnp.asarray(m).astype(BF16) for m in mats]


def kernel(x, c, ctx, c_ctx, w_ada, b_ada, norm1_g, norm2_g, w_in, b_in, conv_w, conv_b, w_fourier_out,
           w_conv_out, w_attn_out, lambda_q1, lambda_k1, lambda_q2, lambda_k2, subln_g, w_out, w_ffn_gate,
           w_ffn_up, w_ffn_down, final_g):
    cc = jnp.concatenate([c, c_ctx[None, :], jnp.zeros((MOD_ROWS - BATCH - 1, D_MODEL), F32)], axis=0)
    mod = _ada(cc, w_ada, b_ada).reshape(DEPTH * MOD_ROWS, 1, 6 * D_MODEL)
    cos_t, sin_t = _rope_tables()
    mats = _fourier_mats()
    vec = lambda a: a.reshape(DEPTH, 1, -1)
    w_in_b, w_fo_b, w_co_b, w_ao_b, w_out_b, w_g_b, w_u_b, w_d_b = (
        a.astype(BF16) for a in (w_in, w_fourier_out, w_conv_out, w_attn_out, w_out, w_ffn_gate, w_ffn_up,
                                 w_ffn_down))
    lamv = jnp.stack([lambda_q1, lambda_k1, lambda_q2, lambda_k2], axis=1).astype(F32)
    g1, g2, b_in3, conv_b3, subln3 = vec(norm1_g), vec(norm2_g), vec(b_in), vec(conv_b), vec(subln_g)
    tokens = (x, ctx)

    for l in range(DEPTH):
        last = l == DEPTH - 1
        with_ctx = not last
        lambda_init = 0.8 - 0.6 * math.exp(-0.3 * l)
        u, bg, z, q, k, v = _in_proj(l, tokens, mod, g1, w_in_b, b_in3, cos_t, sin_t)
        o = _attention(l, lamv, subln3, q, k, v, lambda_init=lambda_init, with_ctx=with_ctx)
        yf = _fourier(u, mats, with_ctx=with_ctx)
        x1 = _merge(l, tokens, mod, g1, yf, bg, z, o, conv_w, conv_b3, w_fo_b, w_co_b, w_ao_b, w_in_b, b_in3,
                    w_out_b, with_ctx=with_ctx)
        tokens = (_ffn(l, x1, mod, g2, w_g_b, w_u_b, w_d_b, final_g.reshape(1, -1), last=last),)
    return tokens[0]
```

```python
import functools
import math

import numpy as np
import jax
import jax.numpy as jnp
from jax import lax
from jax.experimental import pallas as pl
from jax.experimental.pallas import tpu as pltpu

D_MODEL = 1024
BATCH = 8
SEQ = 2048
DEPTH = 2
GRID_W = 64
CTX_LEN = 256
FOURIER_GROUPS = 4
FOURIER_GROUP_DIM = 64
FOURIER_WIDTH = FOURIER_GROUPS * FOURIER_GROUP_DIM
CONV_WIDTH = 256
ATTN_HEADS = 4
QK_DIM = 64
V_DIM = 2 * QK_DIM
ATTN_QK_WIDTH = ATTN_HEADS * 2 * QK_DIM
ATTN_V_WIDTH = ATTN_HEADS * V_DIM
ROPE_BASE = 10000.0
ROPE_AXIS_DIM = QK_DIM // 2
N_BRANCH = 3
D_FF = 2816
EPS = 1e-6

F_OFF = 0
CB_OFF = F_OFF + FOURIER_WIDTH
CC_OFF = CB_OFF + CONV_WIDTH
CX_OFF = CC_OFF + CONV_WIDTH
Q_OFF = CX_OFF + CONV_WIDTH
K_OFF = Q_OFF + ATTN_QK_WIDTH
V_OFF = K_OFF + ATTN_QK_WIDTH
G_OFF = V_OFF + ATTN_V_WIDTH
D_IN = G_OFF + N_BRANCH * D_MODEL

S_ALL = SEQ + CTX_LEN
TM = 256
NT_ALL = S_ALL // TM
NT_LAT = SEQ // TM
TS = 2 * TM
NS_LAT = SEQ // TS
NS_ALL = NS_LAT + 1
MOD_ROWS = 16
CTX_MOD_ROW = BATCH
ADA_TN = 1536
SUBLANES = 8
LANES = 128
V7X_VMEM_LIMIT = 56 * 1024 * 1024
Q_SCALE = (QK_DIM ** -0.5) * math.log2(math.e)
BF16_SUBLANES = 16

BF16 = jnp.bfloat16
F32 = jnp.float32


def _dot(a, b):
    return jnp.dot(a, b, preferred_element_type=F32)


def _rms(x, g):
    return x * lax.rsqrt(jnp.mean(x * x, axis=-1, keepdims=True) + EPS) * g


def _norm_mod(x, g, shift, scale):
    return _rms(x, g) * (1.0 + scale) + shift


def _resident(shape):
    nd = len(shape)
    return pl.BlockSpec(shape, lambda *_: (0,) * nd, pipeline_mode=pl.Buffered(1))


def _layer_resident(shape, l):
    nd = len(shape)
    return pl.BlockSpec((None,) + tuple(shape), lambda *_: (l,) + (0,) * nd, pipeline_mode=pl.Buffered(1))


def _params(n_axes):
    return pltpu.CompilerParams(dimension_semantics=("arbitrary",) * n_axes,
                                vmem_limit_bytes=V7X_VMEM_LIMIT)


def _ada_kernel(cc_ref, w_ref, b_ref, o_ref):
    s = jax.nn.silu(cc_ref[...]).astype(BF16)
    o_ref[0] = _dot(s, w_ref[0].astype(BF16)) + b_ref[0]


def _ada(cc, w_ada, b_ada):
    n_col = 6 * D_MODEL
    return pl.pallas_call(
        _ada_kernel,
        grid=(DEPTH, n_col // ADA_TN),
        in_specs=[
            pl.BlockSpec((MOD_ROWS, D_MODEL), lambda l, j: (0, 0)),
            pl.BlockSpec((1, D_MODEL, ADA_TN), lambda l, j: (l, 0, j)),
            pl.BlockSpec((1, 1, ADA_TN), lambda l, j: (l, 0, j)),
        ],
        out_specs=pl.BlockSpec((1, MOD_ROWS, ADA_TN), lambda l, j: (l, 0, j)),
        out_shape=jax.ShapeDtypeStruct((DEPTH, MOD_ROWS, n_col), F32),
        compiler_params=_params(2),
        name="ada",
    )(cc, w_ada, b_ada.reshape(DEPTH, 1, n_col))


def _mod_spec(l):
    return pl.BlockSpec((1, 1, 6 * D_MODEL),
                        lambda b, i: (l * MOD_ROWS + jnp.where(i == NS_LAT, CTX_MOD_ROW, b), 0, 0))


def _step_spec(width):
    return pl.BlockSpec((1, TS, width), lambda b, i: (b, i, 0))


def _token_specs(split):
    if not split:
        return [_step_spec(D_MODEL)]
    return [pl.BlockSpec((1, TS, D_MODEL), lambda b, i: (b, jnp.minimum(i, NS_LAT - 1), 0)),
            pl.BlockSpec((1, CTX_LEN, D_MODEL), lambda b, i: (b, 0, 0))]


def _half_rows(half):
    return slice(half * TM, (half + 1) * TM)


def _load_tokens(tok_refs, half):
    if len(tok_refs) == 1:
        return tok_refs[0][0, _half_rows(half), :]
    x_ref, ctx_ref = tok_refs
    x = x_ref[0, _half_rows(half), :]
    if half == 0:
        x = jnp.where(pl.program_id(1) == NS_LAT, ctx_ref[0], x)
    return x


def _for_each_half(body, has_ctx_step):
    body(0)
    if has_ctx_step:
        pl.when(pl.program_id(1) < NS_LAT)(lambda: body(1))
    else:
        body(1)


def _sigmoid(t):
    return 0.5 * jnp.tanh(0.5 * t) + 0.5


def _rope(t, cos, sin_signed):
    lane = lax.broadcasted_iota(jnp.int32, t.shape, 1)
    first = (lane % (ROPE_AXIS_DIM)) < (ROPE_AXIS_DIM // 2)
    half = ROPE_AXIS_DIM // 2
    partner = jnp.where(first, pltpu.roll(t, LANES - half, axis=1), pltpu.roll(t, half, axis=1))
    return t * cos + partner * sin_signed


def _in_proj_kernel(*refs, n_tok):
    tok_refs = refs[:n_tok]
    mod_ref, g_ref, w_ref, b_ref, cos_ref, sin_ref, u_ref, bg_ref, z_ref, q_ref, k_ref, v_ref = refs[n_tok:]
    mod = mod_ref[0]

    def half(hf):
        r = _half_rows(hf)
        h = _norm_mod(_load_tokens(tok_refs, hf), g_ref[...], mod[:, 0:D_MODEL],
                      mod[:, D_MODEL:2 * D_MODEL]).astype(BF16)

        def proj(lo, hi):
            return _dot(h, w_ref[:, lo:hi]) + b_ref[:, lo:hi]

        u_ref[0, r, :] = proj(F_OFF, CB_OFF).astype(BF16)
        bg_ref[0, r, :] = proj(CB_OFF, CC_OFF)
        z_ref[0, r, :] = proj(CC_OFF, CX_OFF) * proj(CX_OFF, Q_OFF)
        cos = cos_ref[r, :]
        sin = sin_ref[r, :]
        q = proj(Q_OFF, K_OFF)
        k = proj(K_OFF, V_OFF)
        for hd in range(ATTN_HEADS):
            sl = slice(hd * V_DIM, (hd + 1) * V_DIM)
            q_ref[0, r, sl] = (_rope(q[:, sl], cos, sin) * Q_SCALE).astype(BF16)
            k_ref[0, r, sl] = _rope(k[:, sl], cos, sin).astype(BF16)
        v_ref[0, r, :] = proj(V_OFF, G_OFF).astype(BF16)

    _for_each_half(half, True)


def _in_proj(l, tokens, mod, g, w, b, cos_t, sin_t):
    tok = _step_spec
    out = lambda width, dt: jax.ShapeDtypeStruct((BATCH, S_ALL, width), dt)
    return pl.pallas_call(
        functools.partial(_in_proj_kernel, n_tok=len(tokens)),
        grid=(BATCH, NS_ALL),
        in_specs=_token_specs(len(tokens) == 2) + [
            _mod_spec(l), _layer_resident((1, D_MODEL), l), _layer_resident((D_MODEL, G_OFF), l),
            _layer_resident((1, D_IN), l),
            pl.BlockSpec((TS, LANES), lambda bb, i: (i, 0)),
            pl.BlockSpec((TS, LANES), lambda bb, i: (i, 0)),
        ],
        out_specs=[tok(FOURIER_WIDTH), tok(CONV_WIDTH), tok(CONV_WIDTH),
                   tok(ATTN_QK_WIDTH), tok(ATTN_QK_WIDTH), tok(ATTN_V_WIDTH)],
        out_shape=[out(FOURIER_WIDTH, BF16), out(CONV_WIDTH, F32), out(CONV_WIDTH, F32),
                   out(ATTN_QK_WIDTH, BF16), out(ATTN_QK_WIDTH, BF16), out(ATTN_V_WIDTH, BF16)],
        compiler_params=_params(2),
        name="in_proj",
    )(*tokens, mod, g, w, b, cos_t, sin_t)


VT_ROWS = V_DIM + BF16_SUBLANES
TK = TM
QK_LEAD = 2


def _attn_kernel(lamv_ref, sub_ref, q_ref, k_ref, v_ref, o_ref, vt_ref, *, lambda_init, with_ctx):
    vt_ref[0:V_DIM, :] = v_ref[0].astype(F32).T.astype(BF16)
    r = lax.broadcasted_iota(jnp.int32, (BF16_SUBLANES, S_ALL), 0)
    vt_ref[V_DIM:VT_ROWS, :] = jnp.where(r == 0, 1.0, 0.0).astype(BF16)

    lv = lamv_ref[...]
    lam = (jnp.exp(jnp.sum(lv[0:1] * lv[1:2], axis=-1, keepdims=True))
           - jnp.exp(jnp.sum(lv[2:3] * lv[3:4], axis=-1, keepdims=True)) + lambda_init)
    groups = TK // SUBLANES
    all_chunks = tuple(range(S_ALL // TK))
    ctx_chunks = tuple(range(SEQ // TK, S_ALL // TK))

    def rows(t):
        return pl.ds(t * TM, TM)

    def bcast8(c8):
        return jnp.broadcast_to(jnp.max(c8, axis=0, keepdims=True), (SUBLANES, 2 * TM))

    def start_tile(t):
        qt = q_ref[0, rows(t), :].astype(F32).T
        row = lax.broadcasted_iota(jnp.int32, qt.shape, 0)
        zero = jnp.zeros_like(qt)
        qq = jnp.concatenate([jnp.where(row < QK_DIM, qt, zero), jnp.where(row >= QK_DIM, qt, zero)],
                             axis=1).astype(BF16)
        return dict(qq=qq, s={}, cmax=[], ref=None, acc=None)

    state = {}

    def emit_scores(t, idx, chunks):
        if idx == 0:
            state[t] = start_tile(t)
        d = state[t]
        j = chunks[idx]
        sq = _dot(k_ref[0, j * TK:(j + 1) * TK, :], d["qq"]).reshape(groups, SUBLANES, 2 * TM)
        d["s"][idx] = sq
        d["cmax"].append(jnp.max(sq, axis=0))

    def emit_mix(t, idx, chunks):
        d = state[t]
        j = chunks[idx]
        s = d["s"].pop(idx)
        ref = d["ref"]
        new_ref = bcast8(d["cmax"][idx])
        if idx > 0:
            new_ref = jnp.maximum(ref, new_ref)
        e = jnp.exp2(s - new_ref[None]).reshape(TK, 2 * TM).astype(BF16)
        o = _dot(vt_ref[:, j * TK:(j + 1) * TK], e)
        if idx == 0:
            acc = o
        else:
            alpha = jnp.exp2(ref - new_ref)
            acc = (d["acc"].reshape(VT_ROWS // SUBLANES, SUBLANES, 2 * TM) * alpha[None]).reshape(VT_ROWS, 2 * TM) + o
        d["ref"], d["acc"] = new_ref, acc
        if idx == len(chunks) - 1:
            on = acc[0:V_DIM] * (1.0 / acc[V_DIM:V_DIM + 1])
            o_t = on[:, 0:TM] - lam * on[:, TM:2 * TM]
            o_ref[0, rows(t), :] = (_rms(o_t.T, sub_ref[...]) * (1.0 - lambda_init)).astype(o_ref.dtype)
            del state[t]

    tiles = [(t, all_chunks) for t in range(NT_LAT)] + ([(NT_LAT, ctx_chunks)] if with_ctx else [])
    items = [(t, idx, chunks) for t, chunks in tiles for idx in range(len(chunks))]
    for g in range(len(items) + QK_LEAD):
        if g < len(items):
            emit_scores(*items[g])
        if g >= QK_LEAD:
            emit_mix(*items[g - QK_LEAD])


def _attention(l, lamv, subln_g, q, k, v, *, lambda_init, with_ctx):
    n_q = NT_ALL if with_ctx else NT_LAT
    kv = pl.BlockSpec((1, S_ALL, V_DIM), lambda b, h: (b, 0, h))
    return pl.pallas_call(
        functools.partial(_attn_kernel, lambda_init=lambda_init, with_ctx=with_ctx),
        grid=(BATCH, ATTN_HEADS),
        in_specs=[_layer_resident((4, QK_DIM), l), _layer_resident((1, V_DIM), l), kv, kv, kv],
        out_specs=pl.BlockSpec((1, n_q * TM, V_DIM), lambda b, h: (b, 0, h)),
        out_shape=jax.ShapeDtypeStruct((BATCH, n_q * TM, ATTN_V_WIDTH), BF16),
        scratch_shapes=[pltpu.VMEM((VT_ROWS, S_ALL), BF16)],
        compiler_params=_params(2),
        name="attention",
    )(lamv, subln_g, q, k, v)


def _dft_cos_sin(n):
    j = np.arange(n, dtype=np.int64)
    ang = 2.0 * np.pi * ((j[:, None] * j[None, :]) % n).astype(np.float64) / n
    return np.cos(ang).astype(np.float32), np.sin(ang).astype(np.float32)


def _fourier_kernel(u_ref, bdc_ref, bds_ref, cl_ref, sl_ref, cc_ref, sc_ref, y_ref, *, with_ctx):
    ub = u_ref[0]
    tc = _dot(ub, bdc_ref[...]).astype(BF16)
    ts = _dot(ub, bds_ref[...]).astype(BF16)
    y = _dot(cl_ref[...], tc[0:SEQ]) - _dot(sl_ref[...], ts[0:SEQ])
    y_ref[0, 0:SEQ, :] = (y * (1.0 / math.sqrt(SEQ * FOURIER_GROUP_DIM))).astype(BF16)
    if with_ctx:
        yc = _dot(cc_ref[...], tc[SEQ:S_ALL]) - _dot(sc_ref[...], ts[SEQ:S_ALL])
        y_ref[0, SEQ:S_ALL, :] = (yc * (1.0 / math.sqrt(CTX_LEN * FOURIER_GROUP_DIM))).astype(BF16)


def _fourier(u, mats, *, with_ctx):
    rows = S_ALL if with_ctx else SEQ
    return pl.pallas_call(
        functools.partial(_fourier_kernel, with_ctx=with_ctx),
        grid=(BATCH,),
        in_specs=[pl.BlockSpec((1, S_ALL, FOURIER_WIDTH), lambda b: (b, 0, 0))]
                 + [_resident(m.shape) for m in mats],
        out_specs=pl.BlockSpec((1, rows, FOURIER_WIDTH), lambda b: (b, 0, 0)),
        out_shape=jax.ShapeDtypeStruct((BATCH, rows, FOURIER_WIDTH), BF16),
        compiler_params=_params(1),
        name="fourier",
    )(u, *mats)


def _merge_kernel(*refs, n_tok, with_ctx):
    tok_refs = refs[:n_tok]
    (mod_ref, g_ref, yf_ref, bg_ref, z_ref, zp_ref, zn_ref, o_ref, cw_ref, cb_ref, wfo_ref, wco_ref, wao_ref,
     win_ref, bin_ref, wout_ref, out_ref) = refs[n_tok:]
    i = pl.program_id(1)
    mod = mod_ref[0]
    z = z_ref[0]
    row = lax.broadcasted_iota(jnp.int32, z.shape, 0)
    has_prev = jnp.logical_and(i >= 1, i < NS_LAT)
    has_next = i < NS_LAT - 1
    prev_row = jnp.where(has_prev, zp_ref[0, SUBLANES - 1:SUBLANES, :], 0.0)
    next_row = jnp.where(has_next, zn_ref[0, 0:1, :], 0.0)
    z_prev = jnp.where(row == 0, prev_row, pltpu.roll(z, 1, axis=0))
    z_next = jnp.where(row == TS - 1, next_row, pltpu.roll(z, TS - 1, axis=0))
    if with_ctx:
        z_next = jnp.where(jnp.logical_and(i == NS_LAT, row == CTX_LEN - 1), 0.0, z_next)
    cw = cw_ref[...]
    conv = z_prev * cw[0:1] + z * cw[1:2] + z_next * cw[2:3] + cb_ref[...]
    yc_in = (bg_ref[0] * conv).astype(BF16)

    def half(hf):
        r = _half_rows(hf)
        x = _load_tokens(tok_refs, hf)
        y_a = _dot(o_ref[0, r, :], wao_ref[...])
        y_f = _dot(yf_ref[0, r, :], wfo_ref[...])
        h = _norm_mod(x, g_ref[...], mod[:, 0:D_MODEL], mod[:, D_MODEL:2 * D_MODEL]).astype(BF16)

        def gate(n):
            lo = G_OFF + n * D_MODEL
            return _sigmoid(_dot(h, win_ref[:, lo:lo + D_MODEL]) + bin_ref[:, lo:lo + D_MODEL])

        y = gate(2) * y_a + gate(0) * y_f
        y = y + gate(1) * _dot(yc_in[r, :], wco_ref[...])
        out_ref[0, r, :] = x + mod[:, 2 * D_MODEL:3 * D_MODEL] * _dot(y.astype(BF16), wout_ref[...])

    _for_each_half(half, with_ctx)


def _merge(l, tokens, mod, g, yf, bg, z, o, cw, cb, wfo, wco, wao, w_in, b_in, wout, *, with_ctx):
    tok = _step_spec
    per8 = TS // SUBLANES
    halo_prev = pl.BlockSpec((1, SUBLANES, CONV_WIDTH),
                             lambda b, i: (b, jnp.maximum(i * per8 - 1, 0), 0))
    halo_next = pl.BlockSpec((1, SUBLANES, CONV_WIDTH),
                             lambda b, i: (b, jnp.minimum((i + 1) * per8, S_ALL // SUBLANES - 1), 0))
    weights = [cw, cb, wfo, wco, wao, w_in, b_in, wout]
    return pl.pallas_call(
        functools.partial(_merge_kernel, n_tok=len(tokens), with_ctx=with_ctx),
        grid=(BATCH, NS_ALL if with_ctx else NS_LAT),
        in_specs=_token_specs(len(tokens) == 2)
                 + [_mod_spec(l), _layer_resident((1, D_MODEL), l), tok(FOURIER_WIDTH), tok(CONV_WIDTH),
                    tok(CONV_WIDTH), halo_prev, halo_next, tok(ATTN_V_WIDTH)]
                 + [_layer_resident(w.shape[1:], l) for w in weights],
        out_specs=tok(D_MODEL),
        out_shape=jax.ShapeDtypeStruct((BATCH, S_ALL if with_ctx else SEQ, D_MODEL), F32),
        compiler_params=_params(2),
        name="merge",
    )(*tokens, mod, g, yf, bg, z, z, z, o, *weights)


def _ffn_kernel(x_ref, mod_ref, g_ref, wg_ref, wu_ref, wd_ref, fg_ref, out_ref, *, last):
    mod = mod_ref[0]

    def half(hf):
        r = _half_rows(hf)
        x = x_ref[0, r, :]
        h = _norm_mod(x, g_ref[...], mod[:, 3 * D_MODEL:4 * D_MODEL],
                      mod[:, 4 * D_MODEL:5 * D_MODEL]).astype(BF16)
        gate = _dot(h, wg_ref[...])
        a = (gate * _sigmoid(gate) * _dot(h, wu_ref[...])).astype(BF16)
        x2 = x + mod[:, 5 * D_MODEL:6 * D_MODEL] * _dot(a, wd_ref[...])
        out_ref[0, r, :] = _rms(x2, fg_ref[...]) if last else x2

    _for_each_half(half, not last)


def _ffn(l, x1, mod, g, wg, wu, wd, fg, *, last):
    tok = _step_spec(D_MODEL)
    return pl.pallas_call(
        functools.partial(_ffn_kernel, last=last),
        grid=(BATCH, NS_LAT if last else NS_ALL),
        in_specs=[tok, _mod_spec(l), _layer_resident((1, D_MODEL), l), _layer_resident(wg.shape[1:], l),
                  _layer_resident(wu.shape[1:], l), _layer_resident(wd.shape[1:], l), _resident((1, D_MODEL))],
        out_specs=tok,
        out_shape=jax.ShapeDtypeStruct(x1.shape, F32),
        compiler_params=_params(2),
        name="ffn",
    )(x1, mod, g, wg, wu, wd, fg)


def _rope_tables():
    pos = jnp.arange(SEQ)
    freqs = ROPE_BASE ** (-jnp.arange(0, ROPE_AXIS_DIM, 2, dtype=F32) / ROPE_AXIS_DIM)
    ang_r = (pos // GRID_W).astype(F32)[:, None] * freqs[None, :]
    ang_c = (pos % GRID_W).astype(F32)[:, None] * freqs[None, :]
    cos64 = jnp.concatenate([jnp.cos(ang_r)] * 2 + [jnp.cos(ang_c)] * 2, axis=-1)
    sin64 = jnp.concatenate([-jnp.sin(ang_r), jnp.sin(ang_r), -jnp.sin(ang_c), jnp.sin(ang_c)], axis=-1)
    cos_t = jnp.concatenate([jnp.tile(cos64, (1, 2)), jnp.ones((CTX_LEN, LANES), F32)], axis=0)
    sin_t = jnp.concatenate([jnp.tile(sin64, (1, 2)), jnp.zeros((CTX_LEN, LANES), F32)], axis=0)
    return cos_t, sin_t


def _fourier_mats():
    cg, sg = _dft_cos_sin(FOURIER_GROUP_DIM)
    eye = np.eye(FOURIER_GROUPS, dtype=np.float32)
    mats = [np.kron(eye, cg), np.kron(eye, sg), *_dft_cos_sin(SEQ), *_dft_cos_sin(CTX_LEN)]
    return [jnp.asarray(m).astype(BF16) for m in mats]


def kernel(x, c, ctx, c_ctx, w_ada, b_ada, norm1_g, norm2_g, w_in, b_in, conv_w, conv_b, w_fourier_out,
           w_conv_out, w_attn_out, lambda_q1, lambda_k1, lambda_q2, lambda_k2, subln_g, w_out, w_ffn_gate,
           w_ffn_up, w_ffn_down, final_g):
    cc = jnp.concatenate([c, c_ctx[None, :], jnp.zeros((MOD_ROWS - BATCH - 1, D_MODEL), F32)], axis=0)
    mod = _ada(cc, w_ada, b_ada).reshape(DEPTH * MOD_ROWS, 1, 6 * D_MODEL)
    cos_t, sin_t = _rope_tables()
    mats = _fourier_mats()
    vec = lambda a: a.reshape(DEPTH, 1, -1)
    w_in_b, w_fo_b, w_co_b, w_ao_b, w_out_b, w_g_b, w_u_b, w_d_b = (
        a.astype(BF16) for a in (w_in, w_fourier_out, w_conv_out, w_attn_out, w_out, w_ffn_gate, w_ffn_up,
                                 w_ffn_down))
    lamv = jnp.stack([lambda_q1, lambda_k1, lambda_q2, lambda_k2], axis=1).astype(F32)
    g1, g2, b_in3, conv_b3, subln3 = vec(norm1_g), vec(norm2_g), vec(b_in), vec(conv_b), vec(subln_g)
    tokens = (x, ctx)

    for l in range(DEPTH):
        last = l == DEPTH - 1
        with_ctx = not last
        lambda_init = 0.8 - 0.6 * math.exp(-0.3 * l)
        u, bg, z, q, k, v = _in_proj(l, tokens, mod, g1, w_in_b, b_in3, cos_t, sin_t)
        o = _attention(l, lamv, subln3, q, k, v, lambda_init=lambda_init, with_ctx=with_ctx)
        yf = _fourier(u, mats, with_ctx=with_ctx)
        x1 = _merge(l, tokens, mod, g1, yf, bg, z, o, conv_w, conv_b3, w_fo_b, w_co_b, w_ao_b, w_in_b, b_in3,
                    w_out_b, with_ctx=with_ctx)
        tokens = (_ffn(l, x1, mod, g2, w_g_b, w_u_b, w_d_b, final_g.reshape(1, -1), last=last),)
    return tokens[0]
```

```python
import functools
import math

import numpy as np
import jax
import jax.numpy as jnp
from jax import lax
from jax.experimental import pallas as pl
from jax.experimental.pallas import tpu as pltpu

D_MODEL = 1024
BATCH = 8
SEQ = 2048
DEPTH = 2
GRID_W = 64
CTX_LEN = 256
FOURIER_GROUPS = 4
FOURIER_GROUP_DIM = 64
FOURIER_WIDTH = FOURIER_GROUPS * FOURIER_GROUP_DIM
CONV_WIDTH = 256
ATTN_HEADS = 4
QK_DIM = 64
V_DIM = 2 * QK_DIM
ATTN_QK_WIDTH = ATTN_HEADS * 2 * QK_DIM
ATTN_V_WIDTH = ATTN_HEADS * V_DIM
ROPE_BASE = 10000.0
ROPE_AXIS_DIM = QK_DIM // 2
N_BRANCH = 3
D_FF = 2816
EPS = 1e-6

F_OFF = 0
CB_OFF = F_OFF + FOURIER_WIDTH
CC_OFF = CB_OFF + CONV_WIDTH
CX_OFF = CC_OFF + CONV_WIDTH
Q_OFF = CX_OFF + CONV_WIDTH
K_OFF = Q_OFF + ATTN_QK_WIDTH
V_OFF = K_OFF + ATTN_QK_WIDTH
G_OFF = V_OFF + ATTN_V_WIDTH
D_IN = G_OFF + N_BRANCH * D_MODEL

S_ALL = SEQ + CTX_LEN
TM = 256
TS = 2 * TM
NS_LAT = SEQ // TS
NT_LAT = SEQ // TM
MOD_ROWS = 16
CTX_MOD_ROW = BATCH
ADA_TN = 1536
SUBLANES = 8
LANES = 128
BF16_SUBLANES = 16
V7X_VMEM_LIMIT = 56 * 1024 * 1024
Q_SCALE = (QK_DIM ** -0.5) * math.log2(math.e)

BF16 = jnp.bfloat16
F32 = jnp.float32

assert CTX_LEN == TM


def _dot(a, b):
    return jnp.dot(a, b, preferred_element_type=F32)


def _rms(x, g):
    return x * lax.rsqrt(jnp.mean(x * x, axis=-1, keepdims=True) + EPS) * g


def _norm_mod(x, g, mod, k):
    shift = mod[:, k * D_MODEL:(k + 1) * D_MODEL]
    scale = mod[:, (k + 1) * D_MODEL:(k + 2) * D_MODEL]
    return _rms(x, g) * (1.0 + scale) + shift


def _sigmoid(t):
    return 0.5 * jnp.tanh(0.5 * t) + 0.5


def _resident(shape):
    nd = len(shape)
    return pl.BlockSpec(shape, lambda *_: (0,) * nd, pipeline_mode=pl.Buffered(1))


def _layer_resident(shape, l):
    nd = len(shape)
    return pl.BlockSpec((None,) + tuple(shape), lambda *_: (l,) + (0,) * nd, pipeline_mode=pl.Buffered(1))


def _params(n_axes):
    return pltpu.CompilerParams(dimension_semantics=("arbitrary",) * n_axes,
                                vmem_limit_bytes=V7X_VMEM_LIMIT)


def _lat_spec(width):
    return pl.BlockSpec((1, TS, width), lambda b, i: (b, i, 0))


def _ctx_spec(width):
    return pl.BlockSpec((1, CTX_LEN, width), lambda b, i: (b, 0, 0))


def _mod_specs(l):
    return [pl.BlockSpec((1, 1, 6 * D_MODEL), lambda b, i: (l * MOD_ROWS + b, 0, 0)),
            pl.BlockSpec((1, 1, 6 * D_MODEL), lambda b, i: (l * MOD_ROWS + CTX_MOD_ROW, 0, 0))]


def _half_rows(half):
    return slice(half * TM, (half + 1) * TM)


def _on_first_step(body):
    pl.when(pl.program_id(1) == 0)(body)


def _lat_shape(width, dtype):
    return jax.ShapeDtypeStruct((BATCH, SEQ, width), dtype)


def _ctx_shape(width, dtype):
    return jax.ShapeDtypeStruct((BATCH, CTX_LEN, width), dtype)


def _ada_kernel(cc_ref, w_ref, b_ref, o_ref):
    s = jax.nn.silu(cc_ref[...]).astype(BF16)
    o_ref[0] = _dot(s, w_ref[0].astype(BF16)) + b_ref[0]


def _ada(cc, w_ada, b_ada):
    n_col = 6 * D_MODEL
    return pl.pallas_call(
        _ada_kernel,
        grid=(DEPTH, n_col // ADA_TN),
        in_specs=[
            pl.BlockSpec((MOD_ROWS, D_MODEL), lambda l, j: (0, 0)),
            pl.BlockSpec((1, D_MODEL, ADA_TN), lambda l, j: (l, 0, j)),
            pl.BlockSpec((1, 1, ADA_TN), lambda l, j: (l, 0, j)),
        ],
        out_specs=pl.BlockSpec((1, MOD_ROWS, ADA_TN), lambda l, j: (l, 0, j)),
        out_shape=jax.ShapeDtypeStruct((DEPTH, MOD_ROWS, n_col), F32),
        compiler_params=_params(2),
        name="ada",
    )(cc, w_ada, b_ada.reshape(DEPTH, 1, n_col))


def _rope(t, cos, sin_signed):
    lane = lax.broadcasted_iota(jnp.int32, t.shape, 1)
    first = (lane % (ROPE_AXIS_DIM)) < (ROPE_AXIS_DIM // 2)
    half = ROPE_AXIS_DIM // 2
    partner = jnp.where(first, pltpu.roll(t, LANES - half, axis=1), pltpu.roll(t, half, axis=1))
    return t * cos + partner * sin_signed


def _in_proj_kernel(x_ref, xc_ref, mod_ref, modc_ref, g_ref, w_ref, b_ref, cos_ref, sin_ref, *out_refs,
                    ctx_full):
    lat_out = out_refs[:6]
    ctx_out = out_refs[6:]

    def project(x, mod, r, rope_tables, outs, attn_only):
        h = _norm_mod(x, g_ref[...], mod, 0).astype(BF16)

        def proj(lo, hi):
            return _dot(h, w_ref[:, lo:hi]) + b_ref[:, lo:hi]

        if attn_only:
            k_ref, v_ref = outs
        else:
            u_ref, bg_ref, z_ref, q_ref, k_ref, v_ref = outs
            u_ref[0, r, :] = proj(F_OFF, CB_OFF).astype(BF16)
            bg_ref[0, r, :] = proj(CB_OFF, CC_OFF)
            z_ref[0, r, :] = proj(CC_OFF, CX_OFF) * proj(CX_OFF, Q_OFF)
            q = proj(Q_OFF, K_OFF)
        k = proj(K_OFF, V_OFF)
        if rope_tables is None:
            if not attn_only:
                q_ref[0, r, :] = (q * Q_SCALE).astype(BF16)
            k_ref[0, r, :] = k.astype(BF16)
        else:
            cos, sin = rope_tables
            for hd in range(ATTN_HEADS):
                sl = slice(hd * V_DIM, (hd + 1) * V_DIM)
                q_ref[0, r, sl] = (_rope(q[:, sl], cos, sin) * Q_SCALE).astype(BF16)
                k_ref[0, r, sl] = _rope(k[:, sl], cos, sin).astype(BF16)
        v_ref[0, r, :] = proj(V_OFF, G_OFF).astype(BF16)

    for hf in range(2):
        r = _half_rows(hf)
        project(x_ref[0, r, :], mod_ref[0], r, (cos_ref[r, :], sin_ref[r, :]), lat_out, False)

    @_on_first_step
    def _():
        project(xc_ref[0], modc_ref[0], slice(0, CTX_LEN), None, ctx_out, not ctx_full)


def _in_proj(l, x, xc, mod, g, w, b, cos_t, sin_t, *, ctx_full):
    widths = [(FOURIER_WIDTH, BF16), (CONV_WIDTH, F32), (CONV_WIDTH, F32), (ATTN_QK_WIDTH, BF16),
              (ATTN_QK_WIDTH, BF16), (ATTN_V_WIDTH, BF16)]
    ctx_widths = widths if ctx_full else widths[4:]
    outs = pl.pallas_call(
        functools.partial(_in_proj_kernel, ctx_full=ctx_full),
        grid=(BATCH, NS_LAT),
        in_specs=[_lat_spec(D_MODEL), _ctx_spec(D_MODEL), *_mod_specs(l), _layer_resident((1, D_MODEL), l),
                  _layer_resident((D_MODEL, G_OFF), l), _layer_resident((1, D_IN), l),
                  pl.BlockSpec((TS, LANES), lambda bb, i: (i, 0)),
                  pl.BlockSpec((TS, LANES), lambda bb, i: (i, 0))],
        out_specs=[_lat_spec(wd) for wd, _ in widths] + [_ctx_spec(wd) for wd, _ in ctx_widths],
        out_shape=[_lat_shape(wd, dt) for wd, dt in widths] + [_ctx_shape(wd, dt) for wd, dt in ctx_widths],
        compiler_params=_params(2),
        name="in_proj",
    )(x, xc, mod, mod, g, w, b, cos_t, sin_t)
    return outs[:6], outs[6:]


VT_ROWS = V_DIM + BF16_SUBLANES
TK = TM
QK_LEAD = 2
N_CHUNK_LAT = SEQ // TK
N_CHUNK_ALL = S_ALL // TK


def _attn_kernel(*refs, lambda_init, with_ctx):
    if with_ctx:
        lamv_ref, sub_ref, q_ref, k_ref, v_ref, kc_ref, vc_ref, qc_ref, o_ref, oc_ref, vt_ref = refs
    else:
        lamv_ref, sub_ref, q_ref, k_ref, v_ref, kc_ref, vc_ref, o_ref, vt_ref = refs
    vt_ref[0:V_DIM, 0:SEQ] = v_ref[0].astype(F32).T.astype(BF16)
    vt_ref[0:V_DIM, SEQ:S_ALL] = vc_ref[0].astype(F32).T.astype(BF16)
    r = lax.broadcasted_iota(jnp.int32, (BF16_SUBLANES, S_ALL), 0)
    vt_ref[V_DIM:VT_ROWS, :] = jnp.where(r == 0, 1.0, 0.0).astype(BF16)

    lv = lamv_ref[...]
    lam = (jnp.exp(jnp.sum(lv[0:1] * lv[1:2], axis=-1, keepdims=True))
           - jnp.exp(jnp.sum(lv[2:3] * lv[3:4], axis=-1, keepdims=True)) + lambda_init)
    groups = TK // SUBLANES
    all_chunks = tuple(range(N_CHUNK_ALL))
    ctx_chunks = tuple(range(N_CHUNK_LAT, N_CHUNK_ALL))

    def key_chunk(j):
        if j < N_CHUNK_LAT:
            return k_ref[0, j * TK:(j + 1) * TK, :]
        return kc_ref[0, (j - N_CHUNK_LAT) * TK:(j - N_CHUNK_LAT + 1) * TK, :]

    def query_tile(t):
        return q_ref[0, t * TM:(t + 1) * TM, :] if t < NT_LAT else qc_ref[0]

    def store_tile(t, val):
        if t < NT_LAT:
            o_ref[0, t * TM:(t + 1) * TM, :] = val
        else:
            oc_ref[0] = val

    def bcast8(c8):
        return jnp.broadcast_to(jnp.max(c8, axis=0, keepdims=True), (SUBLANES, 2 * TM))

    def start_tile(t):
        qt = query_tile(t).astype(F32).T
        row = lax.broadcasted_iota(jnp.int32, qt.shape, 0)
        zero = jnp.zeros_like(qt)
        qq = jnp.concatenate([jnp.where(row < QK_DIM, qt, zero), jnp.where(row >= QK_DIM, qt, zero)],
                             axis=1).astype(BF16)
        return dict(qq=qq, s={}, cmax=[], ref=None, acc=None)

    state = {}

    def emit_scores(t, idx, chunks):
        if idx == 0:
            state[t] = start_tile(t)
        d = state[t]
        sq = _dot(key_chunk(chunks[idx]), d["qq"]).reshape(groups, SUBLANES, 2 * TM)
        d["s"][idx] = sq
        d["cmax"].append(jnp.max(sq, axis=0))

    def emit_mix(t, idx, chunks):
        d = state[t]
        j = chunks[idx]
        s = d["s"].pop(idx)
        ref = d["ref"]
        new_ref = bcast8(d["cmax"][idx])
        if idx > 0:
            new_ref = jnp.maximum(ref, new_ref)
        e = jnp.exp2(s - new_ref[None]).reshape(TK, 2 * TM).astype(BF16)
        o = _dot(vt_ref[:, j * TK:(j + 1) * TK], e)
        if idx == 0:
            acc = o
        else:
            alpha = jnp.exp2(ref - new_ref)
            acc = (d["acc"].reshape(VT_ROWS // SUBLANES, SUBLANES, 2 * TM) * alpha[None]).reshape(VT_ROWS, 2 * TM) + o
        d["ref"], d["acc"] = new_ref, acc
        if idx == len(chunks) - 1:
            on = acc[0:V_DIM] * (1.0 / acc[V_DIM:V_DIM + 1])
            o_t = on[:, 0:TM] - lam * on[:, TM:2 * TM]
            store_tile(t, (_rms(o_t.T, sub_ref[...]) * (1.0 - lambda_init)).astype(BF16))
            del state[t]

    tiles = [(t, all_chunks) for t in range(NT_LAT)] + ([(NT_LAT, ctx_chunks)] if with_ctx else [])
    items = [(t, idx, chunks) for t, chunks in tiles for idx in range(len(chunks))]
    for g in range(len(items) + QK_LEAD):
        if g < len(items):
            emit_scores(*items[g])
        if g >= QK_LEAD:
            emit_mix(*items[g - QK_LEAD])


def _attention(l, lamv, subln_g, q, k, v, kc, vc, qc, *, lambda_init, with_ctx):
    lat = pl.BlockSpec((1, SEQ, V_DIM), lambda b, h: (b, 0, h))
    ctx = pl.BlockSpec((1, CTX_LEN, V_DIM), lambda b, h: (b, 0, h))
    return pl.pallas_call(
        functools.partial(_attn_kernel, lambda_init=lambda_init, with_ctx=with_ctx),
        grid=(BATCH, ATTN_HEADS),
        in_specs=[_layer_resident((4, QK_DIM), l), _layer_resident((1, V_DIM), l), lat, lat, lat, ctx, ctx]
                 + ([ctx] if with_ctx else []),
        out_specs=[lat, ctx] if with_ctx else [lat],
        out_shape=[_lat_shape(ATTN_V_WIDTH, BF16)] + ([_ctx_shape(ATTN_V_WIDTH, BF16)] if with_ctx else []),
        scratch_shapes=[pltpu.VMEM((VT_ROWS, S_ALL), BF16)],
        compiler_params=_params(2),
        name="attention",
    )(lamv, subln_g, q, k, v, kc, vc, *([qc] if with_ctx else []))


def _dft_cos_sin(n):
    j = np.arange(n, dtype=np.int64)
    ang = 2.0 * np.pi * ((j[:, None] * j[None, :]) % n).astype(np.float64) / n
    return np.cos(ang).astype(np.float32), np.sin(ang).astype(np.float32)


def _fourier_kernel(*refs, with_ctx):
    if with_ctx:
        u_ref, uc_ref, bdc_ref, bds_ref, cl_ref, sl_ref, cc_ref, sc_ref, y_ref, yc_ref = refs
    else:
        u_ref, bdc_ref, bds_ref, cl_ref, sl_ref, y_ref = refs

    def mix(u, c_mat, s_mat, n):
        tc = _dot(u, bdc_ref[...]).astype(BF16)
        ts = _dot(u, bds_ref[...]).astype(BF16)
        y = _dot(c_mat, tc) - _dot(s_mat, ts)
        return (y * (1.0 / math.sqrt(n * FOURIER_GROUP_DIM))).astype(BF16)

    y_ref[0] = mix(u_ref[0], cl_ref[...], sl_ref[...], SEQ)
    if with_ctx:
        yc_ref[0] = mix(uc_ref[0], cc_ref[...], sc_ref[...], CTX_LEN)


def _fourier(u, uc, mats, *, with_ctx):
    if not with_ctx:
        mats = mats[:4]
    lat = pl.BlockSpec((1, SEQ, FOURIER_WIDTH), lambda b: (b, 0, 0))
    ctx = pl.BlockSpec((1, CTX_LEN, FOURIER_WIDTH), lambda b: (b, 0, 0))
    return pl.pallas_call(
        functools.partial(_fourier_kernel, with_ctx=with_ctx),
        grid=(BATCH,),
        in_specs=([lat, ctx] if with_ctx else [lat]) + [_resident(m.shape) for m in mats],
        out_specs=[lat, ctx] if with_ctx else [lat],
        out_shape=[_lat_shape(FOURIER_WIDTH, BF16)] + ([_ctx_shape(FOURIER_WIDTH, BF16)] if with_ctx else []),
        compiler_params=_params(1),
        name="fourier",
    )(u, *([uc] if with_ctx else []), *mats)


def _merge_kernel(*refs, with_ctx):
    (x_ref, mod_ref, g_ref, yf_ref, bg_ref, z_ref, zp_ref, zn_ref, o_ref, cw_ref, cb_ref, wfo_ref, wco_ref,
     wao_ref, win_ref, bin_ref, wout_ref) = refs[:17]
    if with_ctx:
        xc_ref, modc_ref, yfc_ref, bgc_ref, zc_ref, oc_ref, out_ref, outc_ref = refs[17:]
    else:
        (out_ref,) = refs[17:]
    i = pl.program_id(1)
    cw = cw_ref[...]

    def conv_in(z, bg, prev_row, next_row):
        n = z.shape[0]
        row = lax.broadcasted_iota(jnp.int32, z.shape, 0)
        z_prev = jnp.where(row == 0, prev_row, pltpu.roll(z, 1, axis=0))
        z_next = jnp.where(row == n - 1, next_row, pltpu.roll(z, n - 1, axis=0))
        conv = z_prev * cw[0:1] + z * cw[1:2] + z_next * cw[2:3] + cb_ref[...]
        return (bg * conv).astype(BF16)

    def chain(x, mod, y_a_in, y_f_in, y_c_in):
        y_a = _dot(y_a_in, wao_ref[...])
        y_f = _dot(y_f_in, wfo_ref[...])
        h = _norm_mod(x, g_ref[...], mod, 0).astype(BF16)

        def gate(n):
            lo = G_OFF + n * D_MODEL
            return _sigmoid(_dot(h, win_ref[:, lo:lo + D_MODEL]) + bin_ref[:, lo:lo + D_MODEL])

        y = gate(2) * y_a + gate(0) * y_f
        y = y + gate(1) * _dot(y_c_in, wco_ref[...])
        return x + mod[:, 2 * D_MODEL:3 * D_MODEL] * _dot(y.astype(BF16), wout_ref[...])

    prev_row = jnp.where(i >= 1, zp_ref[0, SUBLANES - 1:SUBLANES, :], 0.0)
    next_row = jnp.where(i < NS_LAT - 1, zn_ref[0, 0:1, :], 0.0)
    yc_in = conv_in(z_ref[0], bg_ref[0], prev_row, next_row)
    for hf in range(2):
        r = _half_rows(hf)
        out_ref[0, r, :] = chain(x_ref[0, r, :], mod_ref[0], o_ref[0, r, :], yf_ref[0, r, :], yc_in[r, :])

    if with_ctx:
        @_on_first_step
        def _():
            ycc_in = conv_in(zc_ref[0], bgc_ref[0], 0.0, 0.0)
            outc_ref[0] = chain(xc_ref[0], modc_ref[0], oc_ref[0], yfc_ref[0], ycc_in)


def _merge(l, x, mod, g, yf, bg, z, o, ctx_parts, cw, cb, wfo, wco, wao, w_in, b_in, wout, *, with_ctx):
    per8 = TS // SUBLANES
    halo_prev = pl.BlockSpec((1, SUBLANES, CONV_WIDTH),
                             lambda b, i: (b, jnp.maximum(i * per8 - 1, 0), 0))
    halo_next = pl.BlockSpec((1, SUBLANES, CONV_WIDTH),
                             lambda b, i: (b, jnp.minimum((i + 1) * per8, SEQ // SUBLANES - 1), 0))
    weights = [cw, cb, wfo, wco, wao, w_in, b_in, wout]
    mod_lat, mod_ctx = _mod_specs(l)
    in_specs = ([_lat_spec(D_MODEL), mod_lat, _layer_resident((1, D_MODEL), l), _lat_spec(FOURIER_WIDTH),
                 _lat_spec(CONV_WIDTH), _lat_spec(CONV_WIDTH), halo_prev, halo_next, _lat_spec(ATTN_V_WIDTH)]
                + [_layer_resident(w.shape[1:], l) for w in weights])
    args = [x, mod, g, yf, bg, z, z, z, o, *weights]
    if with_ctx:
        xc, yfc, bgc, zc, oc = ctx_parts
        in_specs += [_ctx_spec(D_MODEL), mod_ctx, _ctx_spec(FOURIER_WIDTH), _ctx_spec(CONV_WIDTH),
                     _ctx_spec(CONV_WIDTH), _ctx_spec(ATTN_V_WIDTH)]
        args += [xc, mod, yfc, bgc, zc, oc]
    return pl.pallas_call(
        functools.partial(_merge_kernel, with_ctx=with_ctx),
        grid=(BATCH, NS_LAT),
        in_specs=in_specs,
        out_specs=[_lat_spec(D_MODEL)] + ([_ctx_spec(D_MODEL)] if with_ctx else []),
        out_shape=[_lat_shape(D_MODEL, F32)] + ([_ctx_shape(D_MODEL, F32)] if with_ctx else []),
        compiler_params=_params(2),
        name="merge",
    )(*args)


def _ffn_kernel(*refs, last):
    x_ref, mod_ref, g_ref, wg_ref, wu_ref, wd_ref, fg_ref = refs[:7]
    if last:
        (out_ref,) = refs[7:]
    else:
        xc_ref, modc_ref, out_ref, outc_ref = refs[7:]

    def chain(x, mod):
        h = _norm_mod(x, g_ref[...], mod, 3).astype(BF16)
        gate = _dot(h, wg_ref[...])
        a = (gate * _sigmoid(gate) * _dot(h, wu_ref[...])).astype(BF16)
        x2 = x + mod[:, 5 * D_MODEL:6 * D_MODEL] * _dot(a, wd_ref[...])
        return _rms(x2, fg_ref[...]) if last else x2

    for hf in range(2):
        r = _half_rows(hf)
        out_ref[0, r, :] = chain(x_ref[0, r, :], mod_ref[0])

    if not last:
        @_on_first_step
        def _():
            outc_ref[0] = chain(xc_ref[0], modc_ref[0])


def _ffn(l, x1, x1c, mod, g, wg, wu, wd, fg, *, last):
    mod_lat, mod_ctx = _mod_specs(l)
    in_specs = [_lat_spec(D_MODEL), mod_lat, _layer_resident((1, D_MODEL), l), _layer_resident(wg.shape[1:], l),
                _layer_resident(wu.shape[1:], l), _layer_resident(wd.shape[1:], l), _resident((1, D_MODEL))]
    args = [x1, mod, g, wg, wu, wd, fg]
    if not last:
        in_specs += [_ctx_spec(D_MODEL), mod_ctx]
        args += [x1c, mod]
    return pl.pallas_call(
        functools.partial(_ffn_kernel, last=last),
        grid=(BATCH, NS_LAT),
        in_specs=in_specs,
        out_specs=[_lat_spec(D_MODEL)] + ([] if last else [_ctx_spec(D_MODEL)]),
        out_shape=[_lat_shape(D_MODEL, F32)] + ([] if last else [_ctx_shape(D_MODEL, F32)]),
        compiler_params=_params(2),
        name="ffn",
    )(*args)


def _rope_tables():
    pos = jnp.arange(SEQ)
    freqs = ROPE_BASE ** (-jnp.arange(0, ROPE_AXIS_DIM, 2, dtype=F32) / ROPE_AXIS_DIM)
    ang_r = (pos // GRID_W).astype(F32)[:, None] * freqs[None, :]
    ang_c = (pos % GRID_W).astype(F32)[:, None] * freqs[None, :]
    cos64 = jnp.concatenate([jnp.cos(ang_r)] * 2 + [jnp.cos(ang_c)] * 2, axis=-1)
    sin64 = jnp.concatenate([-jnp.sin(ang_r), jnp.sin(ang_r), -jnp.sin(ang_c), jnp.sin(ang_c)], axis=-1)
    return jnp.tile(cos64, (1, 2)), jnp.tile(sin64, (1, 2))


def _fourier_mats():
    cg, sg = _dft_cos_sin(FOURIER_GROUP_DIM)
    eye = np.eye(FOURIER_GROUPS, dtype=np.float32)
    mats = [np.kron(eye, cg), np.kron(eye, sg), *_dft_cos_sin(SEQ), *_dft_cos_sin(CTX_LEN)]
    return [jnp.asarray(m).astype(BF16) for m in mats]


def kernel(x, c, ctx, c_ctx, w_ada, b_ada, norm1_g, norm2_g, w_in, b_in, conv_w, conv_b, w_fourier_out,
           w_conv_out, w_attn_out, lambda_q1, lambda_k1, lambda_q2, lambda_k2, subln_g, w_out, w_ffn_gate,
           w_ffn_up, w_ffn_down, final_g):
    cc = jnp.concatenate([c, c_ctx[None, :], jnp.zeros((MOD_ROWS - BATCH - 1, D_MODEL), F32)], axis=0)
    mod = _ada(cc, w_ada, b_ada).reshape(DEPTH * MOD_ROWS, 1, 6 * D_MODEL)
    cos_t, sin_t = _rope_tables()
    mats = _fourier_mats()
    vec = lambda a: a.reshape(DEPTH, 1, -1)
    w_in_b, w_fo_b, w_co_b, w_ao_b, w_out_b, w_g_b, w_u_b, w_d_b = (
        a.astype(BF16) for a in (w_in, w_fourier_out, w_conv_out, w_attn_out, w_out, w_ffn_gate, w_ffn_up,
                                 w_ffn_down))
    lamv = jnp.stack([lambda_q1, lambda_k1, lambda_q2, lambda_k2], axis=1).astype(F32)
    g1, g2, b_in3, conv_b3, subln3 = vec(norm1_g), vec(norm2_g), vec(b_in), vec(conv_b), vec(subln_g)
    xl, xc = x, ctx

    for l in range(DEPTH):
        last = l == DEPTH - 1
        with_ctx = not last
        lambda_init = 0.8 - 0.6 * math.exp(-0.3 * l)
        (u, bg, z, q, k, v), ctx_p = _in_proj(l, xl, xc, mod, g1, w_in_b, b_in3, cos_t, sin_t, ctx_full=with_ctx)
        uc, bgc, zc, qc, kc, vc = ctx_p if with_ctx else (None, None, None, None) + tuple(ctx_p)
        o = _attention(l, lamv, subln3, q, k, v, kc, vc, qc, lambda_init=lambda_init, with_ctx=with_ctx)
        yf = _fourier(u, uc, mats, with_ctx=with_ctx)
        ctx_parts = (xc, yf[1], bgc, zc, o[1]) if with_ctx else None
        x1 = _merge(l, xl, mod, g1, yf[0], bg, z, o[0], ctx_parts, conv_w, conv_b3, w_fo_b, w_co_b, w_ao_b,
                    w_in_b, b_in3, w_out_b, with_ctx=with_ctx)
        out = _ffn(l, x1[0], x1[1] if with_ctx else None, mod, g2, w_g_b, w_u_b, w_d_b, final_g.reshape(1, -1),
                   last=last)
        xl = out[0]
        if with_ctx:
            xc = out[1]
    return xl
```

```python
import functools
import math

import numpy as np
import jax
import jax.numpy as jnp
from jax import lax
from jax.experimental import pallas as pl
from jax.experimental.pallas import tpu as pltpu

D_MODEL = 1024
BATCH = 8
SEQ = 2048
DEPTH = 2
GRID_W = 64
CTX_LEN = 256
FOURIER_GROUPS = 4
FOURIER_GROUP_DIM = 64
FOURIER_WIDTH = FOURIER_GROUPS * FOURIER_GROUP_DIM
CONV_WIDTH = 256
ATTN_HEADS = 4
QK_DIM = 64
V_DIM = 2 * QK_DIM
ATTN_QK_WIDTH = ATTN_HEADS * 2 * QK_DIM
ATTN_V_WIDTH = ATTN_HEADS * V_DIM
ROPE_BASE = 10000.0
ROPE_AXIS_DIM = QK_DIM // 2
N_BRANCH = 3
D_FF = 2816
EPS = 1e-6

F_OFF = 0
CB_OFF = F_OFF + FOURIER_WIDTH
CC_OFF = CB_OFF + CONV_WIDTH
CX_OFF = CC_OFF + CONV_WIDTH
Q_OFF = CX_OFF + CONV_WIDTH
K_OFF = Q_OFF + ATTN_QK_WIDTH
V_OFF = K_OFF + ATTN_QK_WIDTH
G_OFF = V_OFF + ATTN_V_WIDTH
D_IN = G_OFF + N_BRANCH * D_MODEL

S_ALL = SEQ + CTX_LEN
TM = 256
TS = 2 * TM
NS_LAT = SEQ // TS
NT_LAT = SEQ // TM
MOD_ROWS = 16
CTX_MOD_ROW = BATCH
ADA_TN = 1536
SUBLANES = 8
LANES = 128
BF16_SUBLANES = 16
V7X_VMEM_LIMIT = 56 * 1024 * 1024
Q_SCALE = (QK_DIM ** -0.5) * math.log2(math.e)

BF16 = jnp.bfloat16
F32 = jnp.float32

assert CTX_LEN == TM


def _dot(a, b):
    return jnp.dot(a, b, preferred_element_type=F32)


def _rms(x, g):
    return x * lax.rsqrt(jnp.mean(x * x, axis=-1, keepdims=True) + EPS) * g


def _norm_mod(x, g, mod, k):
    shift = mod[:, k * D_MODEL:(k + 1) * D_MODEL]
    scale = mod[:, (k + 1) * D_MODEL:(k + 2) * D_MODEL]
    return _rms(x, g) * (1.0 + scale) + shift


def _sigmoid(t):
    return 0.5 * jnp.tanh(0.5 * t) + 0.5


def _resident(shape):
    nd = len(shape)
    return pl.BlockSpec(shape, lambda *_: (0,) * nd, pipeline_mode=pl.Buffered(1))


def _layer_resident(shape, l):
    nd = len(shape)
    return pl.BlockSpec((None,) + tuple(shape), lambda *_: (l,) + (0,) * nd, pipeline_mode=pl.Buffered(1))


def _params(n_axes):
    return pltpu.CompilerParams(dimension_semantics=("arbitrary",) * n_axes,
                                vmem_limit_bytes=V7X_VMEM_LIMIT)


def _lat_spec(width):
    return pl.BlockSpec((1, TS, width), lambda b, i: (b, i, 0))


def _ctx_spec(width):
    return pl.BlockSpec((1, CTX_LEN, width), lambda b, i: (b, 0, 0))


def _mod_specs(l):
    return [pl.BlockSpec((1, 1, 6 * D_MODEL), lambda b, i: (l * MOD_ROWS + b, 0, 0)),
            pl.BlockSpec((1, 1, 6 * D_MODEL), lambda b, i: (l * MOD_ROWS + CTX_MOD_ROW, 0, 0))]


def _half_rows(half):
    return slice(half * TM, (half + 1) * TM)


def _on_first_step(body):
    pl.when(pl.program_id(1) == 0)(body)


def _lat_shape(width, dtype):
    return jax.ShapeDtypeStruct((BATCH, SEQ, width), dtype)


def _ctx_shape(width, dtype):
    return jax.ShapeDtypeStruct((BATCH, CTX_LEN, width), dtype)


def _ada_kernel(cc_ref, w_ref, b_ref, o_ref):
    s = jax.nn.silu(cc_ref[...]).astype(BF16)
    o_ref[0] = _dot(s, w_ref[0].astype(BF16)) + b_ref[0]


def _ada(cc, w_ada, b_ada):
    n_col = 6 * D_MODEL
    return pl.pallas_call(
        _ada_kernel,
        grid=(DEPTH, n_col // ADA_TN),
        in_specs=[
            pl.BlockSpec((MOD_ROWS, D_MODEL), lambda l, j: (0, 0)),
            pl.BlockSpec((1, D_MODEL, ADA_TN), lambda l, j: (l, 0, j)),
            pl.BlockSpec((1, 1, ADA_TN), lambda l, j: (l, 0, j)),
        ],
        out_specs=pl.BlockSpec((1, MOD_ROWS, ADA_TN), lambda l, j: (l, 0, j)),
        out_shape=jax.ShapeDtypeStruct((DEPTH, MOD_ROWS, n_col), F32),
        compiler_params=_params(2),
        name="ada",
    )(cc, w_ada, b_ada.reshape(DEPTH, 1, n_col))


def _rope(t, cos, sin_signed):
    lane = lax.broadcasted_iota(jnp.int32, t.shape, 1)
    first = (lane % (ROPE_AXIS_DIM)) < (ROPE_AXIS_DIM // 2)
    half = ROPE_AXIS_DIM // 2
    partner = jnp.where(first, pltpu.roll(t, LANES - half, axis=1), pltpu.roll(t, half, axis=1))
    return t * cos + partner * sin_signed


def _in_proj_kernel(x_ref, xc_ref, mod_ref, modc_ref, g_ref, w_ref, b_ref, cos_ref, sin_ref, *out_refs,
                    ctx_full):
    lat_out = out_refs[:6]
    ctx_out = out_refs[6:]

    def project(x, mod, r, rope_tables, outs, attn_only):
        h = _norm_mod(x, g_ref[...], mod, 0).astype(BF16)

        def proj(lo, hi):
            return _dot(h, w_ref[:, lo:hi]) + b_ref[:, lo:hi]

        if attn_only:
            k_ref, v_ref = outs
        else:
            u_ref, bg_ref, z_ref, q_ref, k_ref, v_ref = outs
            u_ref[0, r, :] = proj(F_OFF, CB_OFF).astype(BF16)
            bg_ref[0, r, :] = proj(CB_OFF, CC_OFF)
            z_ref[0, r, :] = proj(CC_OFF, CX_OFF) * proj(CX_OFF, Q_OFF)
            q = proj(Q_OFF, K_OFF)
        k = proj(K_OFF, V_OFF)
        if rope_tables is None:
            if not attn_only:
                q_ref[0, r, :] = (q * Q_SCALE).astype(BF16)
            k_ref[0, r, :] = k.astype(BF16)
        else:
            cos, sin = rope_tables
            for hd in range(ATTN_HEADS):
                sl = slice(hd * V_DIM, (hd + 1) * V_DIM)
                q_ref[0, r, sl] = (_rope(q[:, sl], cos, sin) * Q_SCALE).astype(BF16)
                k_ref[0, r, sl] = _rope(k[:, sl], cos, sin).astype(BF16)
        v_ref[0, r, :] = proj(V_OFF, G_OFF).astype(BF16)

    for hf in range(2):
        r = _half_rows(hf)
        project(x_ref[0, r, :], mod_ref[0], r, (cos_ref[r, :], sin_ref[r, :]), lat_out, False)

    @_on_first_step
    def _():
        project(xc_ref[0], modc_ref[0], slice(0, CTX_LEN), None, ctx_out, not ctx_full)


def _in_proj(l, x, xc, mod, g, w, w_layer, b, cos_t, sin_t, *, ctx_full):
    widths = [(FOURIER_WIDTH, BF16), (CONV_WIDTH, F32), (CONV_WIDTH, F32), (ATTN_QK_WIDTH, BF16),
              (ATTN_QK_WIDTH, BF16), (ATTN_V_WIDTH, BF16)]
    ctx_widths = widths if ctx_full else widths[4:]
    outs = pl.pallas_call(
        functools.partial(_in_proj_kernel, ctx_full=ctx_full),
        grid=(BATCH, NS_LAT),
        in_specs=[_lat_spec(D_MODEL), _ctx_spec(D_MODEL), *_mod_specs(l), _layer_resident((1, D_MODEL), l),
                  _layer_resident((D_MODEL, G_OFF), w_layer), _layer_resident((1, D_IN), l),
                  pl.BlockSpec((TS, LANES), lambda bb, i: (i, 0)),
                  pl.BlockSpec((TS, LANES), lambda bb, i: (i, 0))],
        out_specs=[_lat_spec(wd) for wd, _ in widths] + [_ctx_spec(wd) for wd, _ in ctx_widths],
        out_shape=[_lat_shape(wd, dt) for wd, dt in widths] + [_ctx_shape(wd, dt) for wd, dt in ctx_widths],
        compiler_params=_params(2),
        name="in_proj",
    )(x, xc, mod, mod, g, w, b, cos_t, sin_t)
    return outs[:6], outs[6:]


VT_ROWS = V_DIM + BF16_SUBLANES
TK = TM
QK_LEAD = 2
N_CHUNK_LAT = SEQ // TK
N_CHUNK_ALL = S_ALL // TK


def _attn_kernel(*refs, lambda_init, with_ctx, n_cast):
    n_in = 8 if with_ctx else 7
    n_out = 2 if with_ctx else 1
    cast_in = refs[n_in:n_in + n_cast]
    cast_out = refs[n_in + n_cast + n_out:n_in + 2 * n_cast + n_out]
    if with_ctx:
        lamv_ref, sub_ref, q_ref, k_ref, v_ref, kc_ref, vc_ref, qc_ref = refs[:n_in]
        o_ref, oc_ref = refs[n_in + n_cast:n_in + n_cast + n_out]
    else:
        lamv_ref, sub_ref, q_ref, k_ref, v_ref, kc_ref, vc_ref = refs[:n_in]
        (o_ref,) = refs[n_in + n_cast:n_in + n_cast + n_out]
    vt_ref = refs[-1]
    for w_ref, wb_ref in zip(cast_in, cast_out):
        wb_ref[...] = w_ref[...].astype(BF16)
    vt_ref[0:V_DIM, 0:SEQ] = v_ref[0].astype(F32).T.astype(BF16)
    vt_ref[0:V_DIM, SEQ:S_ALL] = vc_ref[0].astype(F32).T.astype(BF16)
    r = lax.broadcasted_iota(jnp.int32, (BF16_SUBLANES, S_ALL), 0)
    vt_ref[V_DIM:VT_ROWS, :] = jnp.where(r == 0, 1.0, 0.0).astype(BF16)

    lv = lamv_ref[...]
    lam = (jnp.exp(jnp.sum(lv[0:1] * lv[1:2], axis=-1, keepdims=True))
           - jnp.exp(jnp.sum(lv[2:3] * lv[3:4], axis=-1, keepdims=True)) + lambda_init)
    groups = TK // SUBLANES
    all_chunks = tuple(range(N_CHUNK_ALL))
    ctx_chunks = tuple(range(N_CHUNK_LAT, N_CHUNK_ALL))

    def key_chunk(j):
        if j < N_CHUNK_LAT:
            return k_ref[0, j * TK:(j + 1) * TK, :]
        return kc_ref[0, (j - N_CHUNK_LAT) * TK:(j - N_CHUNK_LAT + 1) * TK, :]

    def query_tile(t):
        return q_ref[0, t * TM:(t + 1) * TM, :] if t < NT_LAT else qc_ref[0]

    def store_tile(t, val):
        if t < NT_LAT:
            o_ref[0, t * TM:(t + 1) * TM, :] = val
        else:
            oc_ref[0] = val

    def bcast8(c8):
        return jnp.broadcast_to(jnp.max(c8, axis=0, keepdims=True), (SUBLANES, 2 * TM))

    def start_tile(t):
        qt = query_tile(t).astype(F32).T
        row = lax.broadcasted_iota(jnp.int32, qt.shape, 0)
        zero = jnp.zeros_like(qt)
        qq = jnp.concatenate([jnp.where(row < QK_DIM, qt, zero), jnp.where(row >= QK_DIM, qt, zero)],
                             axis=1).astype(BF16)
        return dict(qq=qq, s={}, cmax=[], ref=None, acc=None)

    state = {}

    def emit_scores(t, idx, chunks):
        if idx == 0:
            state[t] = start_tile(t)
        d = state[t]
        sq = _dot(key_chunk(chunks[idx]), d["qq"]).reshape(groups, SUBLANES, 2 * TM)
        d["s"][idx] = sq
        d["cmax"].append(jnp.max(sq, axis=0))

    def emit_mix(t, idx, chunks):
        d = state[t]
        j = chunks[idx]
        s = d["s"].pop(idx)
        ref = d["ref"]
        new_ref = bcast8(d["cmax"][idx])
        if idx > 0:
            new_ref = jnp.maximum(ref, new_ref)
        e = jnp.exp2(s - new_ref[None]).reshape(TK, 2 * TM).astype(BF16)
        o = _dot(vt_ref[:, j * TK:(j + 1) * TK], e)
        if idx == 0:
            acc = o
        else:
            alpha = jnp.exp2(ref - new_ref)
            acc = (d["acc"].reshape(VT_ROWS // SUBLANES, SUBLANES, 2 * TM) * alpha[None]).reshape(VT_ROWS, 2 * TM) + o
        d["ref"], d["acc"] = new_ref, acc
        if idx == len(chunks) - 1:
            on = acc[0:V_DIM] * (1.0 / acc[V_DIM:V_DIM + 1])
            o_t = on[:, 0:TM] - lam * on[:, TM:2 * TM]
            store_tile(t, (_rms(o_t.T, sub_ref[...]) * (1.0 - lambda_init)).astype(BF16))
            del state[t]

    tiles = [(t, all_chunks) for t in range(NT_LAT)] + ([(NT_LAT, ctx_chunks)] if with_ctx else [])
    items = [(t, idx, chunks) for t, chunks in tiles for idx in range(len(chunks))]
    for g in range(len(items) + QK_LEAD):
        if g < len(items):
            emit_scores(*items[g])
        if g >= QK_LEAD:
            emit_mix(*items[g - QK_LEAD])


def _attention(l, lamv, subln_g, q, k, v, kc, vc, qc, *, lambda_init, with_ctx, cast=()):
    lat = pl.BlockSpec((1, SEQ, V_DIM), lambda b, h: (b, 0, h))
    ctx = pl.BlockSpec((1, CTX_LEN, V_DIM), lambda b, h: (b, 0, h))
    n_steps = BATCH * ATTN_HEADS
    cast_specs = [pl.BlockSpec((w.shape[0] // n_steps, w.shape[1]), lambda b, h: (b * ATTN_HEADS + h, 0))
                  for w in cast]
    assert all(w.shape[0] % (n_steps * BF16_SUBLANES) == 0 for w in cast)
    return pl.pallas_call(
        functools.partial(_attn_kernel, lambda_init=lambda_init, with_ctx=with_ctx, n_cast=len(cast)),
        grid=(BATCH, ATTN_HEADS),
        in_specs=[_layer_resident((4, QK_DIM), l), _layer_resident((1, V_DIM), l), lat, lat, lat, ctx, ctx]
                 + ([ctx] if with_ctx else []) + cast_specs,
        out_specs=([lat, ctx] if with_ctx else [lat]) + cast_specs,
        out_shape=[_lat_shape(ATTN_V_WIDTH, BF16)] + ([_ctx_shape(ATTN_V_WIDTH, BF16)] if with_ctx else [])
                  + [jax.ShapeDtypeStruct(w.shape, BF16) for w in cast],
        scratch_shapes=[pltpu.VMEM((VT_ROWS, S_ALL), BF16)],
        compiler_params=_params(2),
        name="attention",
    )(lamv, subln_g, q, k, v, kc, vc, *([qc] if with_ctx else []), *cast)


def _dft_cos_sin(n):
    j = np.arange(n, dtype=np.int64)
    ang = 2.0 * np.pi * ((j[:, None] * j[None, :]) % n).astype(np.float64) / n
    return np.cos(ang).astype(np.float32), np.sin(ang).astype(np.float32)


def _seq_dft_mats(n):
    c, s = _dft_cos_sin(n)
    h = n // 2
    rev = np.zeros((h, h), np.float32)
    rev[np.arange(1, h), h - np.arange(1, h)] = 1.0
    nyq = np.zeros((BF16_SUBLANES, n), np.float32)
    nyq[0] = c[h]
    return [c[:h], s[:h], rev, nyq]


def _fourier_kernel(*refs, with_ctx):
    if with_ctx:
        u_ref, uc_ref, bdc_ref, bds_ref = refs[:4]
        lat_mats, ctx_mats, (y_ref, yc_ref) = refs[4:8], refs[8:12], refs[12:]
    else:
        u_ref, bdc_ref, bds_ref = refs[:3]
        lat_mats, (y_ref,) = refs[3:7], refs[7:]

    def mix(u, mats, out_ref, n):
        c_ref, s_ref, rev_ref, nyq_ref = mats
        h = n // 2
        scale = 1.0 / math.sqrt(n * FOURIER_GROUP_DIM)
        tc = _dot(u, bdc_ref[...]).astype(BF16)
        ts = _dot(u, bds_ref[...]).astype(BF16)
        a = _dot(c_ref[...], tc)
        b = _dot(s_ref[...], ts)
        out_ref[0, 0:h, :] = ((a - b) * scale).astype(BF16)
        upper = _dot(rev_ref[...], ((a + b) * scale).astype(BF16))
        nyq = _dot(nyq_ref[...], tc)[0:1] * scale
        row = lax.broadcasted_iota(jnp.int32, upper.shape, 0)
        out_ref[0, h:n, :] = jnp.where(row == 0, nyq, upper).astype(BF16)

    mix(u_ref[0], lat_mats, y_ref, SEQ)
    if with_ctx:
        mix(uc_ref[0], ctx_mats, yc_ref, CTX_LEN)


def _fourier(u, uc, mats, *, with_ctx):
    if not with_ctx:
        mats = mats[:6]
    lat = pl.BlockSpec((1, SEQ, FOURIER_WIDTH), lambda b: (b, 0, 0))
    ctx = pl.BlockSpec((1, CTX_LEN, FOURIER_WIDTH), lambda b: (b, 0, 0))
    return pl.pallas_call(
        functools.partial(_fourier_kernel, with_ctx=with_ctx),
        grid=(BATCH,),
        in_specs=([lat, ctx] if with_ctx else [lat]) + [_resident(m.shape) for m in mats],
        out_specs=[lat, ctx] if with_ctx else [lat],
        out_shape=[_lat_shape(FOURIER_WIDTH, BF16)] + ([_ctx_shape(FOURIER_WIDTH, BF16)] if with_ctx else []),
        compiler_params=_params(1),
        name="fourier",
    )(u, *([uc] if with_ctx else []), *mats)


def _merge_kernel(*refs, with_ctx):
    (x_ref, mod_ref, g_ref, yf_ref, bg_ref, z_ref, zp_ref, zn_ref, o_ref, cw_ref, cb_ref, wfo_ref, wco_ref,
     wao_ref, win_ref, bin_ref, wout_ref) = refs[:17]
    if with_ctx:
        xc_ref, modc_ref, yfc_ref, bgc_ref, zc_ref, oc_ref, out_ref, outc_ref = refs[17:]
    else:
        (out_ref,) = refs[17:]
    i = pl.program_id(1)
    cw = cw_ref[...]

    def conv_in(z, bg, prev_row, next_row):
        n = z.shape[0]
        row = lax.broadcasted_iota(jnp.int32, z.shape, 0)
        z_prev = jnp.where(row == 0, prev_row, pltpu.roll(z, 1, axis=0))
        z_next = jnp.where(row == n - 1, next_row, pltpu.roll(z, n - 1, axis=0))
        conv = z_prev * cw[0:1] + z * cw[1:2] + z_next * cw[2:3] + cb_ref[...]
        return (bg * conv).astype(BF16)

    def chain(x, mod, y_a_in, y_f_in, y_c_in):
        y_a = _dot(y_a_in, wao_ref[...])
        y_f = _dot(y_f_in, wfo_ref[...])
        h = _norm_mod(x, g_ref[...], mod, 0).astype(BF16)

        def gate(n):
            lo = G_OFF + n * D_MODEL
            return _sigmoid(_dot(h, win_ref[:, lo:lo + D_MODEL]) + bin_ref[:, lo:lo + D_MODEL])

        y = gate(2) * y_a + gate(0) * y_f
        y = y + gate(1) * _dot(y_c_in, wco_ref[...])
        return x + mod[:, 2 * D_MODEL:3 * D_MODEL] * _dot(y.astype(BF16), wout_ref[...])

    prev_row = jnp.where(i >= 1, zp_ref[0, SUBLANES - 1:SUBLANES, :], 0.0)
    next_row = jnp.where(i < NS_LAT - 1, zn_ref[0, 0:1, :], 0.0)
    yc_in = conv_in(z_ref[0], bg_ref[0], prev_row, next_row)
    for hf in range(2):
        r = _half_rows(hf)
        out_ref[0, r, :] = chain(x_ref[0, r, :], mod_ref[0], o_ref[0, r, :], yf_ref[0, r, :], yc_in[r, :])

    if with_ctx:
        @_on_first_step
        def _():
            ycc_in = conv_in(zc_ref[0], bgc_ref[0], 0.0, 0.0)
            outc_ref[0] = chain(xc_ref[0], modc_ref[0], oc_ref[0], yfc_ref[0], ycc_in)


def _merge(l, x, mod, g, yf, bg, z, o, ctx_parts, cw, cb, wfo, wco, wao, w_in, b_in, wout, *, with_ctx):
    per8 = TS // SUBLANES
    halo_prev = pl.BlockSpec((1, SUBLANES, CONV_WIDTH),
                             lambda b, i: (b, jnp.maximum(i * per8 - 1, 0), 0))
    halo_next = pl.BlockSpec((1, SUBLANES, CONV_WIDTH),
                             lambda b, i: (b, jnp.minimum((i + 1) * per8, SEQ // SUBLANES - 1), 0))
    weights = [cw, cb, wfo, wco, wao, w_in, b_in, wout]
    mod_lat, mod_ctx = _mod_specs(l)
    in_specs = ([_lat_spec(D_MODEL), mod_lat, _layer_resident((1, D_MODEL), l), _lat_spec(FOURIER_WIDTH),
                 _lat_spec(CONV_WIDTH), _lat_spec(CONV_WIDTH), halo_prev, halo_next, _lat_spec(ATTN_V_WIDTH)]
                + [_layer_resident(w.shape[1:], l) for w in weights])
    args = [x, mod, g, yf, bg, z, z, z, o, *weights]
    if with_ctx:
        xc, yfc, bgc, zc, oc = ctx_parts
        in_specs += [_ctx_spec(D_MODEL), mod_ctx, _ctx_spec(FOURIER_WIDTH), _ctx_spec(CONV_WIDTH),
                     _ctx_spec(CONV_WIDTH), _ctx_spec(ATTN_V_WIDTH)]
        args += [xc, mod, yfc, bgc, zc, oc]
    return pl.pallas_call(
        functools.partial(_merge_kernel, with_ctx=with_ctx),
        grid=(BATCH, NS_LAT),
        in_specs=in_specs,
        out_specs=[_lat_spec(D_MODEL)] + ([_ctx_spec(D_MODEL)] if with_ctx else []),
        out_shape=[_lat_shape(D_MODEL, F32)] + ([_ctx_shape(D_MODEL, F32)] if with_ctx else []),
        compiler_params=_params(2),
        name="merge",
    )(*args)


def _ffn_kernel(*refs, last):
    x_ref, mod_ref, g_ref, wg_ref, wu_ref, wd_ref, fg_ref = refs[:7]
    if last:
        (out_ref,) = refs[7:]
    else:
        xc_ref, modc_ref, out_ref, outc_ref = refs[7:]

    def chain(x, mod):
        h = _norm_mod(x, g_ref[...], mod, 3).astype(BF16)
        gate = _dot(h, wg_ref[...])
        a = (gate * _sigmoid(gate) * _dot(h, wu_ref[...])).astype(BF16)
        x2 = x + mod[:, 5 * D_MODEL:6 * D_MODEL] * _dot(a, wd_ref[...])
        return _rms(x2, fg_ref[...]) if last else x2

    for hf in range(2):
        r = _half_rows(hf)
        out_ref[0, r, :] = chain(x_ref[0, r, :], mod_ref[0])

    if not last:
        @_on_first_step
        def _():
            outc_ref[0] = chain(xc_ref[0], modc_ref[0])


def _ffn(l, x1, x1c, mod, g, wg, wu, wd, fg, *, last):
    mod_lat, mod_ctx = _mod_specs(l)
    in_specs = [_lat_spec(D_MODEL), mod_lat, _layer_resident((1, D_MODEL), l), _layer_resident(wg.shape[1:], l),
                _layer_resident(wu.shape[1:], l), _layer_resident(wd.shape[1:], l), _resident((1, D_MODEL))]
    args = [x1, mod, g, wg, wu, wd, fg]
    if not last:
        in_specs += [_ctx_spec(D_MODEL), mod_ctx]
        args += [x1c, mod]
    return pl.pallas_call(
        functools.partial(_ffn_kernel, last=last),
        grid=(BATCH, NS_LAT),
        in_specs=in_specs,
        out_specs=[_lat_spec(D_MODEL)] + ([] if last else [_ctx_spec(D_MODEL)]),
        out_shape=[_lat_shape(D_MODEL, F32)] + ([] if last else [_ctx_shape(D_MODEL, F32)]),
        compiler_params=_params(2),
        name="ffn",
    )(*args)


def _rope_tables():
    pos = jnp.arange(SEQ)
    freqs = ROPE_BASE ** (-jnp.arange(0, ROPE_AXIS_DIM, 2, dtype=F32) / ROPE_AXIS_DIM)
    ang_r = (pos // GRID_W).astype(F32)[:, None] * freqs[None, :]
    ang_c = (pos % GRID_W).astype(F32)[:, None] * freqs[None, :]
    cos64 = jnp.concatenate([jnp.cos(ang_r)] * 2 + [jnp.cos(ang_c)] * 2, axis=-1)
    sin64 = jnp.concatenate([-jnp.sin(ang_r), jnp.sin(ang_r), -jnp.sin(ang_c), jnp.sin(ang_c)], axis=-1)
    return jnp.tile(cos64, (1, 2)), jnp.tile(sin64, (1, 2))


def _fourier_mats():
    cg, sg = _dft_cos_sin(FOURIER_GROUP_DIM)
    eye = np.eye(FOURIER_GROUPS, dtype=np.float32)
    mats = [np.kron(eye, cg), np.kron(eye, sg), *_seq_dft_mats(SEQ), *_seq_dft_mats(CTX_LEN)]
    return [jnp.asarray(m).astype(BF16) for m in mats]


def kernel(x, c, ctx, c_ctx, w_ada, b_ada, norm1_g, norm2_g, w_in, b_in, conv_w, conv_b, w_fourier_out,
           w_conv_out, w_attn_out, lambda_q1, lambda_k1, lambda_q2, lambda_k2, subln_g, w_out, w_ffn_gate,
           w_ffn_up, w_ffn_down, final_g):
    cc = jnp.concatenate([c, c_ctx[None, :], jnp.zeros((MOD_ROWS - BATCH - 1, D_MODEL), F32)], axis=0)
    mod = _ada(cc, w_ada, b_ada).reshape(DEPTH * MOD_ROWS, 1, 6 * D_MODEL)
    cos_t, sin_t = _rope_tables()
    mats = _fourier_mats()
    vec = lambda a: a.reshape(DEPTH, 1, -1)
    weights_f32 = (w_in, w_fourier_out, w_conv_out, w_attn_out, w_out, w_ffn_gate, w_ffn_up, w_ffn_down)
    w_in_first = w_in[0:1, :, :G_OFF].astype(BF16)
    lamv = jnp.stack([lambda_q1, lambda_k1, lambda_q2, lambda_k2], axis=1).astype(F32)
    g1, g2, b_in3, conv_b3, subln3 = vec(norm1_g), vec(norm2_g), vec(b_in), vec(conv_b), vec(subln_g)
    xl, xc = x, ctx

    for l in range(DEPTH):
        last = l == DEPTH - 1
        with_ctx = not last
        lambda_init = 0.8 - 0.6 * math.exp(-0.3 * l)
        w_proj, w_layer = (w_in_first, 0) if l == 0 else (w_in_b, l)
        (u, bg, z, q, k, v), ctx_p = _in_proj(l, xl, xc, mod, g1, w_proj, w_layer, b_in3, cos_t, sin_t,
                                              ctx_full=with_ctx)
        uc, bgc, zc, qc, kc, vc = ctx_p if with_ctx else (None, None, None, None) + tuple(ctx_p)
        cast = [a.reshape(-1, a.shape[-1]) for a in weights_f32] if l == 0 else []
        o = _attention(l, lamv, subln3, q, k, v, kc, vc, qc, lambda_init=lambda_init, with_ctx=with_ctx,
                       cast=cast)
        if l == 0:
            n_o = len(o) - len(cast)
            w_in_b, w_fo_b, w_co_b, w_ao_b, w_out_b, w_g_b, w_u_b, w_d_b = (
                wb.reshape(a.shape) for wb, a in zip(o[n_o:], weights_f32))
            o = o[:n_o]
        yf = _fourier(u, uc, mats, with_ctx=with_ctx)
        ctx_parts = (xc, yf[1], bgc, zc, o[1]) if with_ctx else None
        x1 = _merge(l, xl, mod, g1, yf[0], bg, z, o[0], ctx_parts, conv_w, conv_b3, w_fo_b, w_co_b, w_ao_b,
                    w_in_b, b_in3, w_out_b, with_ctx=with_ctx)
        out = _ffn(l, x1[0], x1[1] if with_ctx else None, mod, g2, w_g_b, w_u_b, w_d_b, final_g.reshape(1, -1),
                   last=last)
        xl = out[0]
        if with_ctx:
            xc = out[1]
    return xl
```

```python
import functools
import math

import numpy as np
import jax
import jax.numpy as jnp
from jax import lax
from jax.experimental import pallas as pl
from jax.experimental.pallas import tpu as pltpu

D_MODEL = 1024
BATCH = 8
SEQ = 2048
DEPTH = 2
GRID_W = 64
CTX_LEN = 256
FOURIER_GROUPS = 4
FOURIER_GROUP_DIM = 64
FOURIER_WIDTH = FOURIER_GROUPS * FOURIER_GROUP_DIM
CONV_WIDTH = 256
ATTN_HEADS = 4
QK_DIM = 64
V_DIM = 2 * QK_DIM
ATTN_QK_WIDTH = ATTN_HEADS * 2 * QK_DIM
ATTN_V_WIDTH = ATTN_HEADS * V_DIM
ROPE_BASE = 10000.0
ROPE_AXIS_DIM = QK_DIM // 2
N_BRANCH = 3
D_FF = 2816
EPS = 1e-6

F_OFF = 0
CB_OFF = F_OFF + FOURIER_WIDTH
CC_OFF = CB_OFF + CONV_WIDTH
CX_OFF = CC_OFF + CONV_WIDTH
Q_OFF = CX_OFF + CONV_WIDTH
K_OFF = Q_OFF + ATTN_QK_WIDTH
V_OFF = K_OFF + ATTN_QK_WIDTH
G_OFF = V_OFF + ATTN_V_WIDTH
D_IN = G_OFF + N_BRANCH * D_MODEL

S_ALL = SEQ + CTX_LEN
TM = 256
CHAINS = 4
TS = CHAINS * TM
NS_LAT = SEQ // TS
NT_LAT = SEQ // TM
MOD_ROWS = 16
CTX_MOD_ROW = BATCH
ADA_TN = 1536
SUBLANES = 8
LANES = 128
BF16_SUBLANES = 16
V7X_VMEM_LIMIT = 56 * 1024 * 1024
Q_SCALE = (QK_DIM ** -0.5) * math.log2(math.e)

BF16 = jnp.bfloat16
F32 = jnp.float32

assert CTX_LEN == TM


def _dot(a, b):
    return jnp.dot(a, b, preferred_element_type=F32)


def _rms(x, g):
    return x * lax.rsqrt(jnp.mean(x * x, axis=-1, keepdims=True) + EPS) * g


def _norm_mod(x, g, mod, k):
    shift = mod[:, k * D_MODEL:(k + 1) * D_MODEL]
    scale = mod[:, (k + 1) * D_MODEL:(k + 2) * D_MODEL]
    return _rms(x, g) * (1.0 + scale) + shift


def _sigmoid(t):
    return 0.5 * jnp.tanh(0.5 * t) + 0.5


def _resident(shape):
    nd = len(shape)
    return pl.BlockSpec(shape, lambda *_: (0,) * nd, pipeline_mode=pl.Buffered(1))


def _layer_resident(shape, l):
    nd = len(shape)
    return pl.BlockSpec((None,) + tuple(shape), lambda *_: (l,) + (0,) * nd, pipeline_mode=pl.Buffered(1))


def _params(n_axes):
    return pltpu.CompilerParams(dimension_semantics=("arbitrary",) * n_axes,
                                vmem_limit_bytes=V7X_VMEM_LIMIT)


def _lat_spec(width):
    return pl.BlockSpec((1, TS, width), lambda b, i: (b, i, 0))


def _ctx_spec(width):
    return pl.BlockSpec((1, CTX_LEN, width), lambda b, i: (b, 0, 0))


def _mod_specs(l):
    return [pl.BlockSpec((1, 1, 6 * D_MODEL), lambda b, i: (l * MOD_ROWS + b, 0, 0)),
            pl.BlockSpec((1, 1, 6 * D_MODEL), lambda b, i: (l * MOD_ROWS + CTX_MOD_ROW, 0, 0))]


def _half_rows(half):
    return slice(half * TM, (half + 1) * TM)


def _on_first_step(body):
    pl.when(pl.program_id(1) == 0)(body)


def _lat_shape(width, dtype):
    return jax.ShapeDtypeStruct((BATCH, SEQ, width), dtype)


def _ctx_shape(width, dtype):
    return jax.ShapeDtypeStruct((BATCH, CTX_LEN, width), dtype)


def _ada_kernel(cc_ref, w_ref, b_ref, o_ref):
    s = jax.nn.silu(cc_ref[...]).astype(BF16)
    o_ref[0] = _dot(s, w_ref[0].astype(BF16)) + b_ref[0]


def _ada(cc, w_ada, b_ada):
    n_col = 6 * D_MODEL
    return pl.pallas_call(
        _ada_kernel,
        grid=(DEPTH, n_col // ADA_TN),
        in_specs=[
            pl.BlockSpec((MOD_ROWS, D_MODEL), lambda l, j: (0, 0)),
            pl.BlockSpec((1, D_MODEL, ADA_TN), lambda l, j: (l, 0, j)),
            pl.BlockSpec((1, 1, ADA_TN), lambda l, j: (l, 0, j)),
        ],
        out_specs=pl.BlockSpec((1, MOD_ROWS, ADA_TN), lambda l, j: (l, 0, j)),
        out_shape=jax.ShapeDtypeStruct((DEPTH, MOD_ROWS, n_col), F32),
        compiler_params=_params(2),
        name="ada",
    )(cc, w_ada, b_ada.reshape(DEPTH, 1, n_col))


def _rope(t, cos, sin_signed):
    lane = lax.broadcasted_iota(jnp.int32, t.shape, 1)
    first = (lane % (ROPE_AXIS_DIM)) < (ROPE_AXIS_DIM // 2)
    half = ROPE_AXIS_DIM // 2
    partner = jnp.where(first, pltpu.roll(t, LANES - half, axis=1), pltpu.roll(t, half, axis=1))
    return t * cos + partner * sin_signed


def _in_proj_kernel(x_ref, xc_ref, mod_ref, modc_ref, g_ref, w_ref, b_ref, cos_ref, sin_ref, *out_refs,
                    ctx_full):
    lat_out = out_refs[:6]
    ctx_out = out_refs[6:]

    def project(x, mod, r, rope_tables, outs, attn_only):
        h = _norm_mod(x, g_ref[...], mod, 0).astype(BF16)

        def proj(lo, hi):
            return _dot(h, w_ref[:, lo:hi]) + b_ref[:, lo:hi]

        if attn_only:
            k_ref, v_ref = outs
        else:
            u_ref, bg_ref, z_ref, q_ref, k_ref, v_ref = outs
            u_ref[0, r, :] = proj(F_OFF, CB_OFF).astype(BF16)
            bg_ref[0, r, :] = proj(CB_OFF, CC_OFF)
            z_ref[0, r, :] = proj(CC_OFF, CX_OFF) * proj(CX_OFF, Q_OFF)
            q = proj(Q_OFF, K_OFF)
        k = proj(K_OFF, V_OFF)
        if rope_tables is None:
            if not attn_only:
                q_ref[0, r, :] = (q * Q_SCALE).astype(BF16)
            k_ref[0, r, :] = k.astype(BF16)
        else:
            cos, sin = rope_tables
            for hd in range(ATTN_HEADS):
                sl = slice(hd * V_DIM, (hd + 1) * V_DIM)
                q_ref[0, r, sl] = (_rope(q[:, sl], cos, sin) * Q_SCALE).astype(BF16)
                k_ref[0, r, sl] = _rope(k[:, sl], cos, sin).astype(BF16)
        v_ref[0, r, :] = proj(V_OFF, G_OFF).astype(BF16)

    for hf in range(CHAINS):
        r = _half_rows(hf)
        project(x_ref[0, r, :], mod_ref[0], r, (cos_ref[r, :], sin_ref[r, :]), lat_out, False)

    @_on_first_step
    def _():
        project(xc_ref[0], modc_ref[0], slice(0, CTX_LEN), None, ctx_out, not ctx_full)


def _in_proj(l, x, xc, mod, g, w, w_layer, b, cos_t, sin_t, *, ctx_full):
    widths = [(FOURIER_WIDTH, BF16), (CONV_WIDTH, F32), (CONV_WIDTH, F32), (ATTN_QK_WIDTH, BF16),
              (ATTN_QK_WIDTH, BF16), (ATTN_V_WIDTH, BF16)]
    ctx_widths = widths if ctx_full else widths[4:]
    outs = pl.pallas_call(
        functools.partial(_in_proj_kernel, ctx_full=ctx_full),
        grid=(BATCH, NS_LAT),
        in_specs=[_lat_spec(D_MODEL), _ctx_spec(D_MODEL), *_mod_specs(l), _layer_resident((1, D_MODEL), l),
                  _layer_resident((D_MODEL, G_OFF), w_layer), _layer_resident((1, D_IN), l),
                  pl.BlockSpec((TS, LANES), lambda bb, i: (i, 0)),
                  pl.BlockSpec((TS, LANES), lambda bb, i: (i, 0))],
        out_specs=[_lat_spec(wd) for wd, _ in widths] + [_ctx_spec(wd) for wd, _ in ctx_widths],
        out_shape=[_lat_shape(wd, dt) for wd, dt in widths] + [_ctx_shape(wd, dt) for wd, dt in ctx_widths],
        compiler_params=_params(2),
        name="in_proj",
    )(x, xc, mod, mod, g, w, b, cos_t, sin_t)
    return outs[:6], outs[6:]


VT_ROWS = V_DIM + BF16_SUBLANES
TK = TM
QK_LEAD = 2


def _attn_kernel(*refs, lambda_init, with_ctx, n_cast):
    n_in = 8 if with_ctx else 7
    n_out = 2 if with_ctx else 1
    cast_in = refs[n_in:n_in + n_cast]
    cast_out = refs[n_in + n_cast + n_out:n_in + 2 * n_cast + n_out]
    if with_ctx:
        lamv_ref, sub_ref, q_ref, k_ref, v_ref, kc_ref, vc_ref, qc_ref = refs[:n_in]
        o_ref, oc_ref = refs[n_in + n_cast:n_in + n_cast + n_out]
    else:
        lamv_ref, sub_ref, q_ref, k_ref, v_ref, kc_ref, vc_ref = refs[:n_in]
        (o_ref,) = refs[n_in + n_cast:n_in + n_cast + n_out]
    vt_ref = refs[-1]
    for w_ref, wb_ref in zip(cast_in, cast_out):
        wb_ref[...] = w_ref[...].astype(BF16)
    vt_ref[0:V_DIM, 0:SEQ] = v_ref[0].astype(F32).T.astype(BF16)
    vt_ref[0:V_DIM, SEQ:S_ALL] = vc_ref[0].astype(F32).T.astype(BF16)
    r = lax.broadcasted_iota(jnp.int32, (BF16_SUBLANES, S_ALL), 0)
    vt_ref[V_DIM:VT_ROWS, :] = jnp.where(r == 0, 1.0, 0.0).astype(BF16)

    lv = lamv_ref[...]
    lam = (jnp.exp(jnp.sum(lv[0:1] * lv[1:2], axis=-1, keepdims=True))
           - jnp.exp(jnp.sum(lv[2:3] * lv[3:4], axis=-1, keepdims=True)) + lambda_init)
    ctx_chunks = ((SEQ, S_ALL),)
    all_chunks = tuple((lo, lo + TK) for lo in range(0, SEQ, TK)) + ctx_chunks

    def key_chunk(lo, hi):
        return k_ref[0, lo:hi, :] if hi <= SEQ else kc_ref[0, lo - SEQ:hi - SEQ, :]

    def query_tile(t):
        return q_ref[0, t * TM:(t + 1) * TM, :] if t < NT_LAT else qc_ref[0]

    def store_tile(t, val):
        if t < NT_LAT:
            o_ref[0, t * TM:(t + 1) * TM, :] = val
        else:
            oc_ref[0] = val

    def bcast8(c8):
        return jnp.broadcast_to(jnp.max(c8, axis=0, keepdims=True), (SUBLANES, 2 * TM))

    def start_tile(t):
        qt = query_tile(t).astype(F32).T
        row = lax.broadcasted_iota(jnp.int32, qt.shape, 0)
        zero = jnp.zeros_like(qt)
        qq = jnp.concatenate([jnp.where(row < QK_DIM, qt, zero), jnp.where(row >= QK_DIM, qt, zero)],
                             axis=1).astype(BF16)
        return dict(qq=qq, s={}, cmax=[], ref=None, acc=None)

    state = {}

    def emit_scores(t, idx, chunks):
        if idx == 0:
            state[t] = start_tile(t)
        d = state[t]
        lo, hi = chunks[idx]
        sq = _dot(key_chunk(lo, hi), d["qq"]).reshape((hi - lo) // SUBLANES, SUBLANES, 2 * TM)
        d["s"][idx] = sq
        d["cmax"].append(jnp.max(sq, axis=0))

    def emit_mix(t, idx, chunks):
        d = state[t]
        lo, hi = chunks[idx]
        s = d["s"].pop(idx)
        ref = d["ref"]
        new_ref = bcast8(d["cmax"][idx])
        if idx > 0:
            new_ref = jnp.maximum(ref, new_ref)
        e = jnp.exp2(s - new_ref[None]).reshape(hi - lo, 2 * TM).astype(BF16)
        o = _dot(vt_ref[:, lo:hi], e)
        if idx == 0:
            acc = o
        else:
            alpha = jnp.exp2(ref - new_ref)
            acc = (d["acc"].reshape(VT_ROWS // SUBLANES, SUBLANES, 2 * TM) * alpha[None]).reshape(VT_ROWS, 2 * TM) + o
        d["ref"], d["acc"] = new_ref, acc
        if idx == len(chunks) - 1:
            on = acc[0:V_DIM] * (1.0 / acc[V_DIM:V_DIM + 1])
            o_t = on[:, 0:TM] - lam * on[:, TM:2 * TM]
            store_tile(t, (_rms(o_t.T, sub_ref[...]) * (1.0 - lambda_init)).astype(BF16))
            del state[t]

    tiles = [(t, all_chunks) for t in range(NT_LAT)] + ([(NT_LAT, ctx_chunks)] if with_ctx else [])
    items = [(t, idx, chunks) for t, chunks in tiles for idx in range(len(chunks))]
    for g in range(len(items) + QK_LEAD):
        if g < len(items):
            emit_scores(*items[g])
        if g >= QK_LEAD:
            emit_mix(*items[g - QK_LEAD])


def _attention(l, lamv, subln_g, q, k, v, kc, vc, qc, *, lambda_init, with_ctx, cast=()):
    lat = pl.BlockSpec((1, SEQ, V_DIM), lambda b, h: (b, 0, h))
    ctx = pl.BlockSpec((1, CTX_LEN, V_DIM), lambda b, h: (b, 0, h))
    n_steps = BATCH * ATTN_HEADS
    cast_specs = [pl.BlockSpec((w.shape[0] // n_steps, w.shape[1]), lambda b, h: (b * ATTN_HEADS + h, 0))
                  for w in cast]
    assert all(w.shape[0] % (n_steps * BF16_SUBLANES) == 0 for w in cast)
    return pl.pallas_call(
        functools.partial(_attn_kernel, lambda_init=lambda_init, with_ctx=with_ctx, n_cast=len(cast)),
        grid=(BATCH, ATTN_HEADS),
        in_specs=[_layer_resident((4, QK_DIM), l), _layer_resident((1, V_DIM), l), lat, lat, lat, ctx, ctx]
                 + ([ctx] if with_ctx else []) + cast_specs,
        out_specs=([lat, ctx] if with_ctx else [lat]) + cast_specs,
        out_shape=[_lat_shape(ATTN_V_WIDTH, BF16)] + ([_ctx_shape(ATTN_V_WIDTH, BF16)] if with_ctx else [])
                  + [jax.ShapeDtypeStruct(w.shape, BF16) for w in cast],
        scratch_shapes=[pltpu.VMEM((VT_ROWS, S_ALL), BF16)],
        compiler_params=_params(2),
        name="attention",
    )(lamv, subln_g, q, k, v, kc, vc, *([qc] if with_ctx else []), *cast)


def _dft_cos_sin(n):
    j = np.arange(n, dtype=np.int64)
    ang = 2.0 * np.pi * ((j[:, None] * j[None, :]) % n).astype(np.float64) / n
    return np.cos(ang).astype(np.float32), np.sin(ang).astype(np.float32)


def _seq_dft_mats(n):
    c, s = _dft_cos_sin(n)
    h = n // 2
    rev = np.zeros((h, h), np.float32)
    rev[np.arange(1, h), h - np.arange(1, h)] = 1.0
    nyq = np.zeros((BF16_SUBLANES, n), np.float32)
    nyq[0] = c[h]
    return [c[:h], s[:h], rev, nyq]


def _fourier_kernel(*refs, with_ctx):
    if with_ctx:
        u_ref, uc_ref, bdc_ref, bds_ref = refs[:4]
        lat_mats, ctx_mats, (y_ref, yc_ref) = refs[4:8], refs[8:12], refs[12:]
    else:
        u_ref, bdc_ref, bds_ref = refs[:3]
        lat_mats, (y_ref,) = refs[3:7], refs[7:]

    def mix(u, mats, out_ref, n):
        c_ref, s_ref, rev_ref, nyq_ref = mats
        h = n // 2
        scale = 1.0 / math.sqrt(n * FOURIER_GROUP_DIM)
        tc = _dot(u, bdc_ref[...]).astype(BF16)
        ts = _dot(u, bds_ref[...]).astype(BF16)
        a = _dot(c_ref[...], tc)
        b = _dot(s_ref[...], ts)
        out_ref[0, 0:h, :] = ((a - b) * scale).astype(BF16)
        upper = _dot(rev_ref[...], ((a + b) * scale).astype(BF16))
        nyq = _dot(nyq_ref[...], tc)[0:1] * scale
        row = lax.broadcasted_iota(jnp.int32, upper.shape, 0)
        out_ref[0, h:n, :] = jnp.where(row == 0, nyq, upper).astype(BF16)

    mix(u_ref[0], lat_mats, y_ref, SEQ)
    if with_ctx:
        mix(uc_ref[0], ctx_mats, yc_ref, CTX_LEN)


def _fourier(u, uc, mats, *, with_ctx):
    if not with_ctx:
        mats = mats[:6]
    lat = pl.BlockSpec((1, SEQ, FOURIER_WIDTH), lambda b: (b, 0, 0))
    ctx = pl.BlockSpec((1, CTX_LEN, FOURIER_WIDTH), lambda b: (b, 0, 0))
    return pl.pallas_call(
        functools.partial(_fourier_kernel, with_ctx=with_ctx),
        grid=(BATCH,),
        in_specs=([lat, ctx] if with_ctx else [lat]) + [_resident(m.shape) for m in mats],
        out_specs=[lat, ctx] if with_ctx else [lat],
        out_shape=[_lat_shape(FOURIER_WIDTH, BF16)] + ([_ctx_shape(FOURIER_WIDTH, BF16)] if with_ctx else []),
        compiler_params=_params(1),
        name="fourier",
    )(u, *([uc] if with_ctx else []), *mats)


def _merge_kernel(*refs, with_ctx):
    (x_ref, mod_ref, g_ref, yf_ref, bg_ref, z_ref, zp_ref, zn_ref, o_ref, cw_ref, cb_ref, wfo_ref, wco_ref,
     wao_ref, win_ref, bin_ref, wout_ref) = refs[:17]
    if with_ctx:
        xc_ref, modc_ref, yfc_ref, bgc_ref, zc_ref, oc_ref, out_ref, outc_ref = refs[17:]
    else:
        (out_ref,) = refs[17:]
    i = pl.program_id(1)
    cw = cw_ref[...]

    def conv_in(z, bg, prev_row, next_row):
        n = z.shape[0]
        row = lax.broadcasted_iota(jnp.int32, z.shape, 0)
        z_prev = jnp.where(row == 0, prev_row, pltpu.roll(z, 1, axis=0))
        z_next = jnp.where(row == n - 1, next_row, pltpu.roll(z, n - 1, axis=0))
        conv = z_prev * cw[0:1] + z * cw[1:2] + z_next * cw[2:3] + cb_ref[...]
        return (bg * conv).astype(BF16)

    def chain(x, mod, y_a_in, y_f_in, y_c_in):
        y_a = _dot(y_a_in, wao_ref[...])
        y_f = _dot(y_f_in, wfo_ref[...])
        h = _norm_mod(x, g_ref[...], mod, 0).astype(BF16)

        def gate(n):
            lo = G_OFF + n * D_MODEL
            return _sigmoid(_dot(h, win_ref[:, lo:lo + D_MODEL]) + bin_ref[:, lo:lo + D_MODEL])

        y = gate(2) * y_a + gate(0) * y_f
        y = y + gate(1) * _dot(y_c_in, wco_ref[...])
        return x + mod[:, 2 * D_MODEL:3 * D_MODEL] * _dot(y.astype(BF16), wout_ref[...])

    prev_row = jnp.where(i >= 1, zp_ref[0, SUBLANES - 1:SUBLANES, :], 0.0)
    next_row = jnp.where(i < NS_LAT - 1, zn_ref[0, 0:1, :], 0.0)
    yc_in = conv_in(z_ref[0], bg_ref[0], prev_row, next_row)
    for hf in range(CHAINS):
        r = _half_rows(hf)
        out_ref[0, r, :] = chain(x_ref[0, r, :], mod_ref[0], o_ref[0, r, :], yf_ref[0, r, :], yc_in[r, :])

    if with_ctx:
        @_on_first_step
        def _():
            ycc_in = conv_in(zc_ref[0], bgc_ref[0], 0.0, 0.0)
            outc_ref[0] = chain(xc_ref[0], modc_ref[0], oc_ref[0], yfc_ref[0], ycc_in)


def _merge(l, x, mod, g, yf, bg, z, o, ctx_parts, cw, cb, wfo, wco, wao, w_in, b_in, wout, *, with_ctx):
    per8 = TS // SUBLANES
    halo_prev = pl.BlockSpec((1, SUBLANES, CONV_WIDTH),
                             lambda b, i: (b, jnp.maximum(i * per8 - 1, 0), 0))
    halo_next = pl.BlockSpec((1, SUBLANES, CONV_WIDTH),
                             lambda b, i: (b, jnp.minimum((i + 1) * per8, SEQ // SUBLANES - 1), 0))
    weights = [cw, cb, wfo, wco, wao, w_in, b_in, wout]
    mod_lat, mod_ctx = _mod_specs(l)
    in_specs = ([_lat_spec(D_MODEL), mod_lat, _layer_resident((1, D_MODEL), l), _lat_spec(FOURIER_WIDTH),
                 _lat_spec(CONV_WIDTH), _lat_spec(CONV_WIDTH), halo_prev, halo_next, _lat_spec(ATTN_V_WIDTH)]
                + [_layer_resident(w.shape[1:], l) for w in weights])
    args = [x, mod, g, yf, bg, z, z, z, o, *weights]
    if with_ctx:
        xc, yfc, bgc, zc, oc = ctx_parts
        in_specs += [_ctx_spec(D_MODEL), mod_ctx, _ctx_spec(FOURIER_WIDTH), _ctx_spec(CONV_WIDTH),
                     _ctx_spec(CONV_WIDTH), _ctx_spec(ATTN_V_WIDTH)]
        args += [xc, mod, yfc, bgc, zc, oc]
    return pl.pallas_call(
        functools.partial(_merge_kernel, with_ctx=with_ctx),
        grid=(BATCH, NS_LAT),
        in_specs=in_specs,
        out_specs=[_lat_spec(D_MODEL)] + ([_ctx_spec(D_MODEL)] if with_ctx else []),
        out_shape=[_lat_shape(D_MODEL, F32)] + ([_ctx_shape(D_MODEL, F32)] if with_ctx else []),
        compiler_params=_params(2),
        name="merge",
    )(*args)


def _ffn_kernel(*refs, last):
    x_ref, mod_ref, g_ref, wg_ref, wu_ref, wd_ref, fg_ref = refs[:7]
    if last:
        (out_ref,) = refs[7:]
    else:
        xc_ref, modc_ref, out_ref, outc_ref = refs[7:]

    def chain(x, mod):
        h = _norm_mod(x, g_ref[...], mod, 3).astype(BF16)
        gate = _dot(h, wg_ref[...])
        a = (gate * _sigmoid(gate) * _dot(h, wu_ref[...])).astype(BF16)
        x2 = x + mod[:, 5 * D_MODEL:6 * D_MODEL] * _dot(a, wd_ref[...])
        return _rms(x2, fg_ref[...]) if last else x2

    for hf in range(CHAINS):
        r = _half_rows(hf)
        out_ref[0, r, :] = chain(x_ref[0, r, :], mod_ref[0])

    if not last:
        @_on_first_step
        def _():
            outc_ref[0] = chain(xc_ref[0], modc_ref[0])


def _ffn(l, x1, x1c, mod, g, wg, wu, wd, fg, *, last):
    mod_lat, mod_ctx = _mod_specs(l)
    in_specs = [_lat_spec(D_MODEL), mod_lat, _layer_resident((1, D_MODEL), l), _layer_resident(wg.shape[1:], l),
                _layer_resident(wu.shape[1:], l), _layer_resident(wd.shape[1:], l), _resident((1, D_MODEL))]
    args = [x1, mod, g, wg, wu, wd, fg]
    if not last:
        in_specs += [_ctx_spec(D_MODEL), mod_ctx]
        args += [x1c, mod]
    return pl.pallas_call(
        functools.partial(_ffn_kernel, last=last),
        grid=(BATCH, NS_LAT),
        in_specs=in_specs,
        out_specs=[_lat_spec(D_MODEL)] + ([] if last else [_ctx_spec(D_MODEL)]),
        out_shape=[_lat_shape(D_MODEL, F32)] + ([] if last else [_ctx_shape(D_MODEL, F32)]),
        compiler_params=_params(2),
        name="ffn",
    )(*args)


def _rope_tables():
    pos = jnp.arange(SEQ)
    freqs = ROPE_BASE ** (-jnp.arange(0, ROPE_AXIS_DIM, 2, dtype=F32) / ROPE_AXIS_DIM)
    ang_r = (pos // GRID_W).astype(F32)[:, None] * freqs[None, :]
    ang_c = (pos % GRID_W).astype(F32)[:, None] * freqs[None, :]
    cos64 = jnp.concatenate([jnp.cos(ang_r)] * 2 + [jnp.cos(ang_c)] * 2, axis=-1)
    sin64 = jnp.concatenate([-jnp.sin(ang_r), jnp.sin(ang_r), -jnp.sin(ang_c), jnp.sin(ang_c)], axis=-1)
    return jnp.tile(cos64, (1, 2)), jnp.tile(sin64, (1, 2))


def _fourier_mats():
    cg, sg = _dft_cos_sin(FOURIER_GROUP_DIM)
    eye = np.eye(FOURIER_GROUPS, dtype=np.float32)
    mats = [np.kron(eye, cg), np.kron(eye, sg), *_seq_dft_mats(SEQ), *_seq_dft_mats(CTX_LEN)]
    return [jnp.asarray(m).astype(BF16) for m in mats]


def kernel(x, c, ctx, c_ctx, w_ada, b_ada, norm1_g, norm2_g, w_in, b_in, conv_w, conv_b, w_fourier_out,
           w_conv_out, w_attn_out, lambda_q1, lambda_k1, lambda_q2, lambda_k2, subln_g, w_out, w_ffn_gate,
           w_ffn_up, w_ffn_down, final_g):
    cc = jnp.concatenate([c, c_ctx[None, :], jnp.zeros((MOD_ROWS - BATCH - 1, D_MODEL), F32)], axis=0)
    mod = _ada(cc, w_ada, b_ada).reshape(DEPTH * MOD_ROWS, 1, 6 * D_MODEL)
    cos_t, sin_t = _rope_tables()
    mats = _fourier_mats()
    vec = lambda a: a.reshape(DEPTH, 1, -1)
    weights_f32 = (w_in, w_fourier_out, w_conv_out, w_attn_out, w_out, w_ffn_gate, w_ffn_up, w_ffn_down)
    w_in_first = w_in[0:1, :, :G_OFF].astype(BF16)
    lamv = jnp.stack([lambda_q1, lambda_k1, lambda_q2, lambda_k2], axis=1).astype(F32)
    g1, g2, b_in3, conv_b3, subln3 = vec(norm1_g), vec(norm2_g), vec(b_in), vec(conv_b), vec(subln_g)
    xl, xc = x, ctx

    for l in range(DEPTH):
        last = l == DEPTH - 1
        with_ctx = not last
        lambda_init = 0.8 - 0.6 * math.exp(-0.3 * l)
        w_proj, w_layer = (w_in_first, 0) if l == 0 else (w_in_b, l)
        (u, bg, z, q, k, v), ctx_p = _in_proj(l, xl, xc, mod, g1, w_proj, w_layer, b_in3, cos_t, sin_t,
                                              ctx_full=with_ctx)
        uc, bgc, zc, qc, kc, vc = ctx_p if with_ctx else (None, None, None, None) + tuple(ctx_p)
        cast = [a.reshape(-1, a.shape[-1]) for a in weights_f32] if l == 0 else []
        o = _attention(l, lamv, subln3, q, k, v, kc, vc, qc, lambda_init=lambda_init, with_ctx=with_ctx,
                       cast=cast)
        if l == 0:
            n_o = len(o) - len(cast)
            w_in_b, w_fo_b, w_co_b, w_ao_b, w_out_b, w_g_b, w_u_b, w_d_b = (
                wb.reshape(a.shape) for wb, a in zip(o[n_o:], weights_f32))
            o = o[:n_o]
        yf = _fourier(u, uc, mats, with_ctx=with_ctx)
        ctx_parts = (xc, yf[1], bgc, zc, o[1]) if with_ctx else None
        x1 = _merge(l, xl, mod, g1, yf[0], bg, z, o[0], ctx_parts, conv_w, conv_b3, w_fo_b, w_co_b, w_ao_b,
                    w_in_b, b_in3, w_out_b, with_ctx=with_ctx)
        out = _ffn(l, x1[0], x1[1] if with_ctx else None, mod, g2, w_g_b, w_u_b, w_d_b, final_g.reshape(1, -1),
                   last=last)
        xl = out[0]
        if with_ctx:
            xc = out[1]
    return xl
```

```python
import functools
import math

import numpy as np
import jax
import jax.numpy as jnp
from jax import lax
from jax.experimental import pallas as pl
from jax.experimental.pallas import tpu as pltpu

D_MODEL = 1024
BATCH = 8
SEQ = 2048
DEPTH = 2
GRID_W = 64
CTX_LEN = 256
FOURIER_GROUPS = 4
FOURIER_GROUP_DIM = 64
FOURIER_WIDTH = FOURIER_GROUPS * FOURIER_GROUP_DIM
CONV_WIDTH = 256
ATTN_HEADS = 4
QK_DIM = 64
V_DIM = 2 * QK_DIM
ATTN_QK_WIDTH = ATTN_HEADS * 2 * QK_DIM
ATTN_V_WIDTH = ATTN_HEADS * V_DIM
ROPE_BASE = 10000.0
ROPE_AXIS_DIM = QK_DIM // 2
N_BRANCH = 3
D_FF = 2816
EPS = 1e-6

F_OFF = 0
CB_OFF = F_OFF + FOURIER_WIDTH
CC_OFF = CB_OFF + CONV_WIDTH
CX_OFF = CC_OFF + CONV_WIDTH
Q_OFF = CX_OFF + CONV_WIDTH
K_OFF = Q_OFF + ATTN_QK_WIDTH
V_OFF = K_OFF + ATTN_QK_WIDTH
G_OFF = V_OFF + ATTN_V_WIDTH
D_IN = G_OFF + N_BRANCH * D_MODEL

S_ALL = SEQ + CTX_LEN
TM = 256
CHAINS = 4
TS = CHAINS * TM
NS_LAT = SEQ // TS
NT_LAT = SEQ // TM
MOD_ROWS = 16
CTX_MOD_ROW = BATCH
ADA_TN = 1536
SUBLANES = 8
LANES = 128
BF16_SUBLANES = 16
V7X_VMEM_LIMIT = 56 * 1024 * 1024
Q_SCALE = (QK_DIM ** -0.5) * math.log2(math.e)

BF16 = jnp.bfloat16
F32 = jnp.float32

assert CTX_LEN == TM


def _dot(a, b):
    return jnp.dot(a, b, preferred_element_type=F32)


def _rms(x, g):
    return x * lax.rsqrt(jnp.mean(x * x, axis=-1, keepdims=True) + EPS) * g


def _norm_mod(x, g, mod, k):
    shift = mod[:, k * D_MODEL:(k + 1) * D_MODEL]
    scale = mod[:, (k + 1) * D_MODEL:(k + 2) * D_MODEL]
    return _rms(x, g) * (1.0 + scale) + shift


def _sigmoid(t):
    return 0.5 * jnp.tanh(0.5 * t) + 0.5


def _resident(shape):
    nd = len(shape)
    return pl.BlockSpec(shape, lambda *_: (0,) * nd, pipeline_mode=pl.Buffered(1))


def _layer_resident(shape, l):
    nd = len(shape)
    return pl.BlockSpec((None,) + tuple(shape), lambda *_: (l,) + (0,) * nd, pipeline_mode=pl.Buffered(1))


def _params(n_axes):
    return pltpu.CompilerParams(dimension_semantics=("arbitrary",) * n_axes,
                                vmem_limit_bytes=V7X_VMEM_LIMIT)


def _lat_spec(width):
    return pl.BlockSpec((1, TS, width), lambda b, i: (b, i, 0))


def _ctx_spec(width):
    return pl.BlockSpec((1, CTX_LEN, width), lambda b, i: (b, 0, 0))


def _mod_specs(l):
    return [pl.BlockSpec((1, 1, 6 * D_MODEL), lambda b, i: (l * MOD_ROWS + b, 0, 0)),
            pl.BlockSpec((1, 1, 6 * D_MODEL), lambda b, i: (l * MOD_ROWS + CTX_MOD_ROW, 0, 0))]


def _half_rows(half):
    return slice(half * TM, (half + 1) * TM)


def _on_first_step(body):
    pl.when(pl.program_id(1) == 0)(body)


def _lat_shape(width, dtype):
    return jax.ShapeDtypeStruct((BATCH, SEQ, width), dtype)


def _ctx_shape(width, dtype):
    return jax.ShapeDtypeStruct((BATCH, CTX_LEN, width), dtype)


def _ada_kernel(cc_ref, w_ref, b_ref, o_ref):
    s = jax.nn.silu(cc_ref[...]).astype(BF16)
    o_ref[0] = _dot(s, w_ref[0].astype(BF16)) + b_ref[0]


def _ada(cc, w_ada, b_ada):
    n_col = 6 * D_MODEL
    return pl.pallas_call(
        _ada_kernel,
        grid=(DEPTH, n_col // ADA_TN),
        in_specs=[
            pl.BlockSpec((MOD_ROWS, D_MODEL), lambda l, j: (0, 0)),
            pl.BlockSpec((1, D_MODEL, ADA_TN), lambda l, j: (l, 0, j)),
            pl.BlockSpec((1, 1, ADA_TN), lambda l, j: (l, 0, j)),
        ],
        out_specs=pl.BlockSpec((1, MOD_ROWS, ADA_TN), lambda l, j: (l, 0, j)),
        out_shape=jax.ShapeDtypeStruct((DEPTH, MOD_ROWS, n_col), F32),
        compiler_params=_params(2),
        name="ada",
    )(cc, w_ada, b_ada.reshape(DEPTH, 1, n_col))


def _rope(t, cos, sin_signed):
    lane = lax.broadcasted_iota(jnp.int32, t.shape, 1)
    first = (lane % (ROPE_AXIS_DIM)) < (ROPE_AXIS_DIM // 2)
    half = ROPE_AXIS_DIM // 2
    partner = jnp.where(first, pltpu.roll(t, LANES - half, axis=1), pltpu.roll(t, half, axis=1))
    return t * cos + partner * sin_signed


def _in_proj_kernel(x_ref, xc_ref, mod_ref, modc_ref, g_ref, w_ref, b_ref, cos_ref, sin_ref, *out_refs,
                    ctx_full, cast_w):
    if cast_w:
        w_f32_ref, w_ref, out_refs = w_ref, out_refs[-1], out_refs[:-1]

        @pl.when(jnp.logical_and(pl.program_id(0) == 0, pl.program_id(1) == 0))
        def _():
            w_ref[...] = w_f32_ref[...].astype(BF16)
    lat_out = out_refs[:6]
    ctx_out = out_refs[6:]

    def project(x, mod, r, rope_tables, outs, attn_only):
        h = _norm_mod(x, g_ref[...], mod, 0).astype(BF16)

        def proj(lo, hi):
            return _dot(h, w_ref[:, lo:hi]) + b_ref[:, lo:hi]

        if attn_only:
            k_ref, v_ref = outs
        else:
            u_ref, bg_ref, z_ref, q_ref, k_ref, v_ref = outs
            u_ref[0, r, :] = proj(F_OFF, CB_OFF).astype(BF16)
            bg_ref[0, r, :] = proj(CB_OFF, CC_OFF)
            z_ref[0, r, :] = proj(CC_OFF, CX_OFF) * proj(CX_OFF, Q_OFF)
            q = proj(Q_OFF, K_OFF)
        k = proj(K_OFF, V_OFF)
        if rope_tables is None:
            if not attn_only:
                q_ref[0, r, :] = (q * Q_SCALE).astype(BF16)
            k_ref[0, r, :] = k.astype(BF16)
        else:
            cos, sin = rope_tables
            for hd in range(ATTN_HEADS):
                sl = slice(hd * V_DIM, (hd + 1) * V_DIM)
                q_ref[0, r, sl] = (_rope(q[:, sl], cos, sin) * Q_SCALE).astype(BF16)
                k_ref[0, r, sl] = _rope(k[:, sl], cos, sin).astype(BF16)
        v_ref[0, r, :] = proj(V_OFF, G_OFF).astype(BF16)

    for hf in range(CHAINS):
        r = _half_rows(hf)
        project(x_ref[0, r, :], mod_ref[0], r, (cos_ref[r, :], sin_ref[r, :]), lat_out, False)

    @_on_first_step
    def _():
        project(xc_ref[0], modc_ref[0], slice(0, CTX_LEN), None, ctx_out, not ctx_full)


def _in_proj(l, x, xc, mod, g, w, w_layer, b, cos_t, sin_t, *, ctx_full):
    widths = [(FOURIER_WIDTH, BF16), (CONV_WIDTH, F32), (CONV_WIDTH, F32), (ATTN_QK_WIDTH, BF16),
              (ATTN_QK_WIDTH, BF16), (ATTN_V_WIDTH, BF16)]
    ctx_widths = widths if ctx_full else widths[4:]
    cast_w = w.dtype != BF16
    outs = pl.pallas_call(
        functools.partial(_in_proj_kernel, ctx_full=ctx_full, cast_w=cast_w),
        scratch_shapes=[pltpu.VMEM((D_MODEL, G_OFF), BF16)] if cast_w else [],
        grid=(BATCH, NS_LAT),
        in_specs=[_lat_spec(D_MODEL), _ctx_spec(D_MODEL), *_mod_specs(l), _layer_resident((1, D_MODEL), l),
                  _layer_resident((D_MODEL, G_OFF), w_layer), _layer_resident((1, D_IN), l),
                  pl.BlockSpec((TS, LANES), lambda bb, i: (i, 0)),
                  pl.BlockSpec((TS, LANES), lambda bb, i: (i, 0))],
        out_specs=[_lat_spec(wd) for wd, _ in widths] + [_ctx_spec(wd) for wd, _ in ctx_widths],
        out_shape=[_lat_shape(wd, dt) for wd, dt in widths] + [_ctx_shape(wd, dt) for wd, dt in ctx_widths],
        compiler_params=_params(2),
        name="in_proj",
    )(x, xc, mod, mod, g, w, b, cos_t, sin_t)
    return outs[:6], outs[6:]


VT_ROWS = V_DIM + BF16_SUBLANES
TK = TM
QK_LEAD = 2


def _attn_kernel(*refs, lambda_init, with_ctx, n_cast):
    n_in = 8 if with_ctx else 7
    n_out = 2 if with_ctx else 1
    cast_in = refs[n_in:n_in + n_cast]
    cast_out = refs[n_in + n_cast + n_out:n_in + 2 * n_cast + n_out]
    if with_ctx:
        lamv_ref, sub_ref, q_ref, k_ref, v_ref, kc_ref, vc_ref, qc_ref = refs[:n_in]
        o_ref, oc_ref = refs[n_in + n_cast:n_in + n_cast + n_out]
    else:
        lamv_ref, sub_ref, q_ref, k_ref, v_ref, kc_ref, vc_ref = refs[:n_in]
        (o_ref,) = refs[n_in + n_cast:n_in + n_cast + n_out]
    vt_ref = refs[-1]
    for w_ref, wb_ref in zip(cast_in, cast_out):
        wb_ref[...] = w_ref[...].astype(BF16)
    vt_ref[0:V_DIM, 0:SEQ] = v_ref[0].astype(F32).T.astype(BF16)
    vt_ref[0:V_DIM, SEQ:S_ALL] = vc_ref[0].astype(F32).T.astype(BF16)
    r = lax.broadcasted_iota(jnp.int32, (BF16_SUBLANES, S_ALL), 0)
    vt_ref[V_DIM:VT_ROWS, :] = jnp.where(r == 0, 1.0, 0.0).astype(BF16)

    lv = lamv_ref[...]
    lam = (jnp.exp(jnp.sum(lv[0:1] * lv[1:2], axis=-1, keepdims=True))
           - jnp.exp(jnp.sum(lv[2:3] * lv[3:4], axis=-1, keepdims=True)) + lambda_init)
    ctx_chunks = ((SEQ, S_ALL),)
    all_chunks = tuple((lo, lo + TK) for lo in range(0, SEQ, TK)) + ctx_chunks

    def key_chunk(lo, hi):
        return k_ref[0, lo:hi, :] if hi <= SEQ else kc_ref[0, lo - SEQ:hi - SEQ, :]

    def query_tile(t):
        return q_ref[0, t * TM:(t + 1) * TM, :] if t < NT_LAT else qc_ref[0]

    def store_tile(t, val):
        if t < NT_LAT:
            o_ref[0, t * TM:(t + 1) * TM, :] = val
        else:
            oc_ref[0] = val

    def bcast8(c8):
        return jnp.broadcast_to(jnp.max(c8, axis=0, keepdims=True), (SUBLANES, 2 * TM))

    def start_tile(t):
        qt = query_tile(t).astype(F32).T
        row = lax.broadcasted_iota(jnp.int32, qt.shape, 0)
        zero = jnp.zeros_like(qt)
        qq = jnp.concatenate([jnp.where(row < QK_DIM, qt, zero), jnp.where(row >= QK_DIM, qt, zero)],
                             axis=1).astype(BF16)
        return dict(qq=qq, s={}, cmax=[], ref=None, acc=None)

    state = {}

    def emit_scores(t, idx, chunks):
        if idx == 0:
            state[t] = start_tile(t)
        d = state[t]
        lo, hi = chunks[idx]
        sq = _dot(key_chunk(lo, hi), d["qq"]).reshape((hi - lo) // SUBLANES, SUBLANES, 2 * TM)
        d["s"][idx] = sq
        d["cmax"].append(jnp.max(sq, axis=0))

    def emit_mix(t, idx, chunks):
        d = state[t]
        lo, hi = chunks[idx]
        s = d["s"].pop(idx)
        ref = d["ref"]
        new_ref = bcast8(d["cmax"][idx])
        if idx > 0:
            new_ref = jnp.maximum(ref, new_ref)
        e = jnp.exp2(s - new_ref[None]).reshape(hi - lo, 2 * TM).astype(BF16)
        o = _dot(vt_ref[:, lo:hi], e)
        if idx == 0:
            acc = o
        else:
            alpha = jnp.exp2(ref - new_ref)
            acc = (d["acc"].reshape(VT_ROWS // SUBLANES, SUBLANES, 2 * TM) * alpha[None]).reshape(VT_ROWS, 2 * TM) + o
        d["ref"], d["acc"] = new_ref, acc
        if idx == len(chunks) - 1:
            on = acc[0:V_DIM] * (1.0 / acc[V_DIM:V_DIM + 1])
            o_t = on[:, 0:TM] - lam * on[:, TM:2 * TM]
            store_tile(t, (_rms(o_t.T, sub_ref[...]) * (1.0 - lambda_init)).astype(BF16))
            del state[t]

    tiles = [(t, all_chunks) for t in range(NT_LAT)] + ([(NT_LAT, ctx_chunks)] if with_ctx else [])
    items = [(t, idx, chunks) for t, chunks in tiles for idx in range(len(chunks))]
    for g in range(len(items) + QK_LEAD):
        if g < len(items):
            emit_scores(*items[g])
        if g >= QK_LEAD:
            emit_mix(*items[g - QK_LEAD])


def _attention(l, lamv, subln_g, q, k, v, kc, vc, qc, *, lambda_init, with_ctx, cast=()):
    lat = pl.BlockSpec((1, SEQ, V_DIM), lambda b, h: (b, 0, h))
    ctx = pl.BlockSpec((1, CTX_LEN, V_DIM), lambda b, h: (b, 0, h))
    n_steps = BATCH * ATTN_HEADS
    cast_specs = [pl.BlockSpec((w.shape[0] // n_steps, w.shape[1]), lambda b, h: (b * ATTN_HEADS + h, 0))
                  for w in cast]
    assert all(w.shape[0] % (n_steps * BF16_SUBLANES) == 0 for w in cast)
    return pl.pallas_call(
        functools.partial(_attn_kernel, lambda_init=lambda_init, with_ctx=with_ctx, n_cast=len(cast)),
        grid=(BATCH, ATTN_HEADS),
        in_specs=[_layer_resident((4, QK_DIM), l), _layer_resident((1, V_DIM), l), lat, lat, lat, ctx, ctx]
                 + ([ctx] if with_ctx else []) + cast_specs,
        out_specs=([lat, ctx] if with_ctx else [lat]) + cast_specs,
        out_shape=[_lat_shape(ATTN_V_WIDTH, BF16)] + ([_ctx_shape(ATTN_V_WIDTH, BF16)] if with_ctx else [])
                  + [jax.ShapeDtypeStruct(w.shape, BF16) for w in cast],
        scratch_shapes=[pltpu.VMEM((VT_ROWS, S_ALL), BF16)],
        compiler_params=_params(2),
        name="attention",
    )(lamv, subln_g, q, k, v, kc, vc, *([qc] if with_ctx else []), *cast)


FOURIER_BATCHES = 2
def _dft_cos_sin(n):
    j = np.arange(n, dtype=np.int64)
    ang = 2.0 * np.pi * ((j[:, None] * j[None, :]) % n).astype(np.float64) / n
    return np.cos(ang).astype(np.float32), np.sin(ang).astype(np.float32)


def _seq_dft_mats(n):
    c, s = _dft_cos_sin(n)
    h = n // 2
    rev = np.zeros((h, h), np.float32)
    rev[np.arange(1, h), h - np.arange(1, h)] = 1.0
    nyq = np.zeros((BF16_SUBLANES, n), np.float32)
    nyq[0] = c[h]
    return [c[:h], s[:h], rev, nyq]


def _fourier_kernel(*refs, with_ctx):
    if with_ctx:
        u_ref, uc_ref, bdc_ref, bds_ref = refs[:4]
        lat_mats, ctx_mats, (y_ref, yc_ref) = refs[4:8], refs[8:12], refs[12:]
    else:
        u_ref, bdc_ref, bds_ref = refs[:3]
        lat_mats, (y_ref,) = refs[3:7], refs[7:]

    def mix(u, mats, out_ref, bb, n):
        c_ref, s_ref, rev_ref, nyq_ref = mats
        h = n // 2
        scale = 1.0 / math.sqrt(n * FOURIER_GROUP_DIM)
        tc = _dot(u, bdc_ref[...]).astype(BF16)
        ts = _dot(u, bds_ref[...]).astype(BF16)
        a = _dot(c_ref[...], tc)
        b = _dot(s_ref[...], ts)
        out_ref[bb, 0:h, :] = ((a - b) * scale).astype(BF16)
        upper = _dot(rev_ref[...], ((a + b) * scale).astype(BF16))
        nyq = _dot(nyq_ref[...], tc)[0:1] * scale
        row = lax.broadcasted_iota(jnp.int32, upper.shape, 0)
        out_ref[bb, h:n, :] = jnp.where(row == 0, nyq, upper).astype(BF16)

    for bb in range(FOURIER_BATCHES):
        mix(u_ref[bb], lat_mats, y_ref, bb, SEQ)
        if with_ctx:
            mix(uc_ref[bb], ctx_mats, yc_ref, bb, CTX_LEN)


def _fourier(u, uc, mats, *, with_ctx):
    if not with_ctx:
        mats = mats[:6]
    lat = pl.BlockSpec((FOURIER_BATCHES, SEQ, FOURIER_WIDTH), lambda b: (b, 0, 0))
    ctx = pl.BlockSpec((FOURIER_BATCHES, CTX_LEN, FOURIER_WIDTH), lambda b: (b, 0, 0))
    return pl.pallas_call(
        functools.partial(_fourier_kernel, with_ctx=with_ctx),
        grid=(BATCH // FOURIER_BATCHES,),
        in_specs=([lat, ctx] if with_ctx else [lat]) + [_resident(m.shape) for m in mats],
        out_specs=[lat, ctx] if with_ctx else [lat],
        out_shape=[_lat_shape(FOURIER_WIDTH, BF16)] + ([_ctx_shape(FOURIER_WIDTH, BF16)] if with_ctx else []),
        compiler_params=_params(1),
        name="fourier",
    )(u, *([uc] if with_ctx else []), *mats)


MERGE_NB = 256
def _merge_kernel(*refs, with_ctx):
    (x_ref, mod_ref, g_ref, yf_ref, bg_ref, z_ref, zp_ref, zn_ref, o_ref, cw_ref, cb_ref, wfo_ref, wco_ref,
     wao_ref, win_ref, bin_ref, wout_ref) = refs[:17]
    if with_ctx:
        xc_ref, modc_ref, yfc_ref, bgc_ref, zc_ref, oc_ref, out_ref, outc_ref = refs[17:]
    else:
        (out_ref,) = refs[17:]
    i = pl.program_id(1)
    cw = cw_ref[...]

    def conv_in(z, bg, prev_row, next_row):
        n = z.shape[0]
        row = lax.broadcasted_iota(jnp.int32, z.shape, 0)
        z_prev = jnp.where(row == 0, prev_row, pltpu.roll(z, 1, axis=0))
        z_next = jnp.where(row == n - 1, next_row, pltpu.roll(z, n - 1, axis=0))
        conv = z_prev * cw[0:1] + z * cw[1:2] + z_next * cw[2:3] + cb_ref[...]
        return (bg * conv).astype(BF16)

    def chain(x, mod, y_a_in, y_f_in, y_c_in):
        h = _norm_mod(x, g_ref[...], mod, 0).astype(BF16)
        blocks = []
        for c in range(D_MODEL // MERGE_NB):
            cs = slice(c * MERGE_NB, (c + 1) * MERGE_NB)

            def gate(n):
                lo = G_OFF + n * D_MODEL + c * MERGE_NB
                return _sigmoid(_dot(h, win_ref[:, lo:lo + MERGE_NB]) + bin_ref[:, lo:lo + MERGE_NB])

            y_a = _dot(y_a_in, wao_ref[:, cs])
            y_f = _dot(y_f_in, wfo_ref[:, cs])
            y_c = _dot(y_c_in, wco_ref[:, cs])
            blocks.append((gate(2) * y_a + gate(0) * y_f + gate(1) * y_c).astype(BF16))
        y = jnp.concatenate(blocks, axis=1)
        return x + mod[:, 2 * D_MODEL:3 * D_MODEL] * _dot(y, wout_ref[...])

    prev_row = jnp.where(i >= 1, zp_ref[0, SUBLANES - 1:SUBLANES, :], 0.0)
    next_row = jnp.where(i < NS_LAT - 1, zn_ref[0, 0:1, :], 0.0)
    yc_in = conv_in(z_ref[0], bg_ref[0], prev_row, next_row)
    for hf in range(CHAINS):
        r = _half_rows(hf)
        out_ref[0, r, :] = chain(x_ref[0, r, :], mod_ref[0], o_ref[0, r, :], yf_ref[0, r, :], yc_in[r, :])

    if with_ctx:
        @_on_first_step
        def _():
            ycc_in = conv_in(zc_ref[0], bgc_ref[0], 0.0, 0.0)
            outc_ref[0] = chain(xc_ref[0], modc_ref[0], oc_ref[0], yfc_ref[0], ycc_in)


def _merge(l, x, mod, g, yf, bg, z, o, ctx_parts, cw, cb, wfo, wco, wao, w_in, b_in, wout, *, with_ctx):
    per8 = TS // SUBLANES
    halo_prev = pl.BlockSpec((1, SUBLANES, CONV_WIDTH),
                             lambda b, i: (b, jnp.maximum(i * per8 - 1, 0), 0))
    halo_next = pl.BlockSpec((1, SUBLANES, CONV_WIDTH),
                             lambda b, i: (b, jnp.minimum((i + 1) * per8, SEQ // SUBLANES - 1), 0))
    weights = [cw, cb, wfo, wco, wao, w_in, b_in, wout]
    mod_lat, mod_ctx = _mod_specs(l)
    in_specs = ([_lat_spec(D_MODEL), mod_lat, _layer_resident((1, D_MODEL), l), _lat_spec(FOURIER_WIDTH),
                 _lat_spec(CONV_WIDTH), _lat_spec(CONV_WIDTH), halo_prev, halo_next, _lat_spec(ATTN_V_WIDTH)]
                + [_layer_resident(w.shape[1:], l) for w in weights])
    args = [x, mod, g, yf, bg, z, z, z, o, *weights]
    if with_ctx:
        xc, yfc, bgc, zc, oc = ctx_parts
        in_specs += [_ctx_spec(D_MODEL), mod_ctx, _ctx_spec(FOURIER_WIDTH), _ctx_spec(CONV_WIDTH),
                     _ctx_spec(CONV_WIDTH), _ctx_spec(ATTN_V_WIDTH)]
        args += [xc, mod, yfc, bgc, zc, oc]
    return pl.pallas_call(
        functools.partial(_merge_kernel, with_ctx=with_ctx),
        grid=(BATCH, NS_LAT),
        in_specs=in_specs,
        out_specs=[_lat_spec(D_MODEL)] + ([_ctx_spec(D_MODEL)] if with_ctx else []),
        out_shape=[_lat_shape(D_MODEL, F32)] + ([_ctx_shape(D_MODEL, F32)] if with_ctx else []),
        compiler_params=_params(2),
        name="merge",
    )(*args)


def _ffn_kernel(*refs, last):
    x_ref, mod_ref, g_ref, wg_ref, wu_ref, wd_ref, fg_ref = refs[:7]
    if last:
        (out_ref,) = refs[7:]
    else:
        xc_ref, modc_ref, out_ref, outc_ref = refs[7:]

    def chain(x, mod):
        h = _norm_mod(x, g_ref[...], mod, 3).astype(BF16)
        gate = _dot(h, wg_ref[...])
        a = (gate * _sigmoid(gate) * _dot(h, wu_ref[...])).astype(BF16)
        x2 = x + mod[:, 5 * D_MODEL:6 * D_MODEL] * _dot(a, wd_ref[...])
        return _rms(x2, fg_ref[...]) if last else x2

    for hf in range(CHAINS):
        r = _half_rows(hf)
        out_ref[0, r, :] = chain(x_ref[0, r, :], mod_ref[0])

    if not last:
        @_on_first_step
        def _():
            outc_ref[0] = chain(xc_ref[0], modc_ref[0])


def _ffn(l, x1, x1c, mod, g, wg, wu, wd, fg, *, last):
    mod_lat, mod_ctx = _mod_specs(l)
    in_specs = [_lat_spec(D_MODEL), mod_lat, _layer_resident((1, D_MODEL), l), _layer_resident(wg.shape[1:], l),
                _layer_resident(wu.shape[1:], l), _layer_resident(wd.shape[1:], l), _resident((1, D_MODEL))]
    args = [x1, mod, g, wg, wu, wd, fg]
    if not last:
        in_specs += [_ctx_spec(D_MODEL), mod_ctx]
        args += [x1c, mod]
    return pl.pallas_call(
        functools.partial(_ffn_kernel, last=last),
        grid=(BATCH, NS_LAT),
        in_specs=in_specs,
        out_specs=[_lat_spec(D_MODEL)] + ([] if last else [_ctx_spec(D_MODEL)]),
        out_shape=[_lat_shape(D_MODEL, F32)] + ([] if last else [_ctx_shape(D_MODEL, F32)]),
        compiler_params=_params(2),
        name="ffn",
    )(*args)


def _rope_tables():
    pos = jnp.arange(SEQ)
    freqs = ROPE_BASE ** (-jnp.arange(0, ROPE_AXIS_DIM, 2, dtype=F32) / ROPE_AXIS_DIM)
    ang_r = (pos // GRID_W).astype(F32)[:, None] * freqs[None, :]
    ang_c = (pos % GRID_W).astype(F32)[:, None] * freqs[None, :]
    cos64 = jnp.concatenate([jnp.cos(ang_r)] * 2 + [jnp.cos(ang_c)] * 2, axis=-1)
    sin64 = jnp.concatenate([-jnp.sin(ang_r), jnp.sin(ang_r), -jnp.sin(ang_c), jnp.sin(ang_c)], axis=-1)
    return jnp.tile(cos64, (1, 2)), jnp.tile(sin64, (1, 2))


def _fourier_mats():
    cg, sg = _dft_cos_sin(FOURIER_GROUP_DIM)
    eye = np.eye(FOURIER_GROUPS, dtype=np.float32)
    mats = [np.kron(eye, cg), np.kron(eye, sg), *_seq_dft_mats(SEQ), *_seq_dft_mats(CTX_LEN)]
    return [jnp.asarray(m).astype(BF16) for m in mats]


def kernel(x, c, ctx, c_ctx, w_ada, b_ada, norm1_g, norm2_g, w_in, b_in, conv_w, conv_b, w_fourier_out,
           w_conv_out, w_attn_out, lambda_q1, lambda_k1, lambda_q2, lambda_k2, subln_g, w_out, w_ffn_gate,
           w_ffn_up, w_ffn_down, final_g):
    cc = jnp.concatenate([c, c_ctx[None, :], jnp.zeros((MOD_ROWS - BATCH - 1, D_MODEL), F32)], axis=0)
    mod = _ada(cc, w_ada, b_ada).reshape(DEPTH * MOD_ROWS, 1, 6 * D_MODEL)
    cos_t, sin_t = _rope_tables()
    mats = _fourier_mats()
    vec = lambda a: a.reshape(DEPTH, 1, -1)
    weights_f32 = (w_in, w_fourier_out, w_conv_out, w_attn_out, w_out, w_ffn_gate, w_ffn_up, w_ffn_down)
    lamv = jnp.stack([lambda_q1, lambda_k1, lambda_q2, lambda_k2], axis=1).astype(F32)
    g1, g2, b_in3, conv_b3, subln3 = vec(norm1_g), vec(norm2_g), vec(b_in), vec(conv_b), vec(subln_g)
    xl, xc = x, ctx

    for l in range(DEPTH):
        last = l == DEPTH - 1
        with_ctx = not last
        lambda_init = 0.8 - 0.6 * math.exp(-0.3 * l)
        (u, bg, z, q, k, v), ctx_p = _in_proj(l, xl, xc, mod, g1, w_in if l == 0 else w_in_b, l, b_in3, cos_t,
                                              sin_t, ctx_full=with_ctx)
        uc, bgc, zc, qc, kc, vc = ctx_p if with_ctx else (None, None, None, None) + tuple(ctx_p)
        cast = [a.reshape(-1, a.shape[-1]) for a in weights_f32] if l == 0 else []
        o = _attention(l, lamv, subln3, q, k, v, kc, vc, qc, lambda_init=lambda_init, with_ctx=with_ctx,
                       cast=cast)
        if l == 0:
            n_o = len(o) - len(cast)
            w_in_b, w_fo_b, w_co_b, w_ao_b, w_out_b, w_g_b, w_u_b, w_d_b = (
                wb.reshape(a.shape) for wb, a in zip(o[n_o:], weights_f32))
            o = o[:n_o]
        yf = _fourier(u, uc, mats, with_ctx=with_ctx)
        ctx_parts = (xc, yf[1], bgc, zc, o[1]) if with_ctx else None
        x1 = _merge(l, xl, mod, g1, yf[0], bg, z, o[0], ctx_parts, conv_w, conv_b3, w_fo_b, w_co_b, w_ao_b,
                    w_in_b, b_in3, w_out_b, with_ctx=with_ctx)
        out = _ffn(l, x1[0], x1[1] if with_ctx else None, mod, g2, w_g_b, w_u_b, w_d_b, final_g.reshape(1, -1),
                   last=last)
        xl = out[0]
        if with_ctx:
            xc = out[1]
    return xl
```

```python
import functools
import math

import numpy as np
import jax
import jax.numpy as jnp
from jax import lax
from jax.experimental import pallas as pl
from jax.experimental.pallas import tpu as pltpu

D_MODEL = 1024
BATCH = 8
SEQ = 2048
DEPTH = 2
GRID_W = 64
CTX_LEN = 256
FOURIER_GROUPS = 4
FOURIER_GROUP_DIM = 64
FOURIER_WIDTH = FOURIER_GROUPS * FOURIER_GROUP_DIM
CONV_WIDTH = 256
ATTN_HEADS = 4
QK_DIM = 64
V_DIM = 2 * QK_DIM
ATTN_QK_WIDTH = ATTN_HEADS * 2 * QK_DIM
ATTN_V_WIDTH = ATTN_HEADS * V_DIM
ROPE_BASE = 10000.0
ROPE_AXIS_DIM = QK_DIM // 2
N_BRANCH = 3
D_FF = 2816
EPS = 1e-6

F_OFF = 0
CB_OFF = F_OFF + FOURIER_WIDTH
CC_OFF = CB_OFF + CONV_WIDTH
CX_OFF = CC_OFF + CONV_WIDTH
Q_OFF = CX_OFF + CONV_WIDTH
K_OFF = Q_OFF + ATTN_QK_WIDTH
V_OFF = K_OFF + ATTN_QK_WIDTH
G_OFF = V_OFF + ATTN_V_WIDTH
D_IN = G_OFF + N_BRANCH * D_MODEL

S_ALL = SEQ + CTX_LEN
TM = 256
CHAINS = 4
TS = CHAINS * TM
NS_LAT = SEQ // TS
NT_LAT = SEQ // TM
MOD_ROWS = 16
CTX_MOD_ROW = BATCH
ADA_TN = 1536
SUBLANES = 8
LANES = 128
BF16_SUBLANES = 16
V7X_VMEM_LIMIT = 56 * 1024 * 1024
Q_SCALE = (QK_DIM ** -0.5) * math.log2(math.e)

BF16 = jnp.bfloat16
F32 = jnp.float32

assert CTX_LEN == TM and BATCH == SUBLANES and CTX_MOD_ROW % SUBLANES == 0


def _dot(a, b):
    return jnp.dot(a, b, preferred_element_type=F32)


def _rms(x, g):
    return x * lax.rsqrt(jnp.mean(x * x, axis=-1, keepdims=True) + EPS) * g


def _norm_mod(x, g, mod, k):
    shift = mod[:, k * D_MODEL:(k + 1) * D_MODEL]
    scale = mod[:, (k + 1) * D_MODEL:(k + 2) * D_MODEL]
    return _rms(x, g) * (1.0 + scale) + shift


def _sigmoid(t):
    return 0.5 * jnp.tanh(0.5 * t) + 0.5


def _resident(shape):
    nd = len(shape)
    return pl.BlockSpec(shape, lambda *_: (0,) * nd, pipeline_mode=pl.Buffered(1))


def _layer_resident(shape, l):
    nd = len(shape)
    return pl.BlockSpec((None,) + tuple(shape), lambda *_: (l,) + (0,) * nd, pipeline_mode=pl.Buffered(1))


def _params(n_axes):
    return pltpu.CompilerParams(dimension_semantics=("arbitrary",) * n_axes,
                                vmem_limit_bytes=V7X_VMEM_LIMIT)


def _lat_spec(width):
    return pl.BlockSpec((1, TS, width), lambda b, i: (b, i, 0))


def _ctx_spec(width):
    return pl.BlockSpec((1, CTX_LEN, width), lambda b, i: (b, 0, 0))


def _mod_specs(l):
    per_layer = MOD_ROWS // SUBLANES
    return [pl.BlockSpec((SUBLANES, 6 * D_MODEL), lambda b, i: (l * per_layer, 0)),
            pl.BlockSpec((SUBLANES, 6 * D_MODEL), lambda b, i: (l * per_layer + CTX_MOD_ROW // SUBLANES, 0))]


def _lat_mod(mod_ref):
    return mod_ref[pl.ds(pl.program_id(0), 1), :]


def _ctx_mod(modc_ref):
    return modc_ref[0:1, :]


def _stacked_vec(n):
    return _resident((DEPTH, n))


def _half_rows(half):
    return slice(half * TM, (half + 1) * TM)


def _on_first_step(body):
    pl.when(pl.program_id(1) == 0)(body)


def _lat_shape(width, dtype):
    return jax.ShapeDtypeStruct((BATCH, SEQ, width), dtype)


def _ctx_shape(width, dtype):
    return jax.ShapeDtypeStruct((BATCH, CTX_LEN, width), dtype)


def _ada_kernel(cc_ref, w_ref, b_ref, o_ref):
    s = jax.nn.silu(cc_ref[...]).astype(BF16)
    o_ref[0] = _dot(s, w_ref[0].astype(BF16)) + b_ref[0]


def _ada(cc, w_ada, b_ada):
    n_col = 6 * D_MODEL
    return pl.pallas_call(
        _ada_kernel,
        grid=(DEPTH, n_col // ADA_TN),
        in_specs=[
            pl.BlockSpec((MOD_ROWS, D_MODEL), lambda l, j: (0, 0)),
            pl.BlockSpec((1, D_MODEL, ADA_TN), lambda l, j: (l, 0, j)),
            pl.BlockSpec((1, 1, ADA_TN), lambda l, j: (l, 0, j)),
        ],
        out_specs=pl.BlockSpec((1, MOD_ROWS, ADA_TN), lambda l, j: (l, 0, j)),
        out_shape=jax.ShapeDtypeStruct((DEPTH, MOD_ROWS, n_col), F32),
        compiler_params=_params(2),
        name="ada",
    )(cc, w_ada, b_ada.reshape(DEPTH, 1, n_col))


def _rope(t, cos, sin_signed):
    lane = lax.broadcasted_iota(jnp.int32, t.shape, 1)
    first = (lane % (ROPE_AXIS_DIM)) < (ROPE_AXIS_DIM // 2)
    half = ROPE_AXIS_DIM // 2
    partner = jnp.where(first, pltpu.roll(t, LANES - half, axis=1), pltpu.roll(t, half, axis=1))
    return t * cos + partner * sin_signed


def _in_proj_kernel(x_ref, xc_ref, mod_ref, modc_ref, g_ref, w_ref, b_ref, cos_ref, sin_ref, *out_refs,
                    ctx_full, cast_w, layer):
    if cast_w:
        w_f32_ref, w_ref, out_refs = w_ref, out_refs[-1], out_refs[:-1]

        @pl.when(jnp.logical_and(pl.program_id(0) == 0, pl.program_id(1) == 0))
        def _():
            w_ref[...] = w_f32_ref[...].astype(BF16)
    lat_out = out_refs[:6]
    ctx_out = out_refs[6:]

    def project(x, mod, r, rope_tables, outs, attn_only):
        h = _norm_mod(x, g_ref[layer:layer + 1, :], mod, 0).astype(BF16)

        def proj(lo, hi):
            return _dot(h, w_ref[:, lo:hi]) + b_ref[layer:layer + 1, lo:hi]

        if attn_only:
            k_ref, v_ref = outs
        else:
            u_ref, bg_ref, z_ref, q_ref, k_ref, v_ref = outs
            u_ref[0, r, :] = proj(F_OFF, CB_OFF).astype(BF16)
            bg_ref[0, r, :] = proj(CB_OFF, CC_OFF)
            z_ref[0, r, :] = proj(CC_OFF, CX_OFF) * proj(CX_OFF, Q_OFF)
            q = proj(Q_OFF, K_OFF)
        k = proj(K_OFF, V_OFF)
        if rope_tables is None:
            if not attn_only:
                q_ref[0, r, :] = (q * Q_SCALE).astype(BF16)
            k_ref[0, r, :] = k.astype(BF16)
        else:
            cos, sin = rope_tables
            for hd in range(ATTN_HEADS):
                sl = slice(hd * V_DIM, (hd + 1) * V_DIM)
                q_ref[0, r, sl] = (_rope(q[:, sl], cos, sin) * Q_SCALE).astype(BF16)
                k_ref[0, r, sl] = _rope(k[:, sl], cos, sin).astype(BF16)
        v_ref[0, r, :] = proj(V_OFF, G_OFF).astype(BF16)

    for hf in range(CHAINS):
        r = _half_rows(hf)
        project(x_ref[0, r, :], _lat_mod(mod_ref), r, (cos_ref[r, :], sin_ref[r, :]), lat_out, False)

    @_on_first_step
    def _():
        project(xc_ref[0], _ctx_mod(modc_ref), slice(0, CTX_LEN), None, ctx_out, not ctx_full)


def _in_proj(l, x, xc, mod, g, w, w_layer, b, cos_t, sin_t, *, ctx_full):
    widths = [(FOURIER_WIDTH, BF16), (CONV_WIDTH, F32), (CONV_WIDTH, F32), (ATTN_QK_WIDTH, BF16),
              (ATTN_QK_WIDTH, BF16), (ATTN_V_WIDTH, BF16)]
    ctx_widths = widths if ctx_full else widths[4:]
    cast_w = w.dtype != BF16
    outs = pl.pallas_call(
        functools.partial(_in_proj_kernel, ctx_full=ctx_full, cast_w=cast_w, layer=l),
        scratch_shapes=[pltpu.VMEM((D_MODEL, G_OFF), BF16)] if cast_w else [],
        grid=(BATCH, NS_LAT),
        in_specs=[_lat_spec(D_MODEL), _ctx_spec(D_MODEL), *_mod_specs(l), _stacked_vec(D_MODEL),
                  _layer_resident((D_MODEL, G_OFF), w_layer), _stacked_vec(D_IN),
                  pl.BlockSpec((TS, LANES), lambda bb, i: (i, 0)),
                  pl.BlockSpec((TS, LANES), lambda bb, i: (i, 0))],
        out_specs=[_lat_spec(wd) for wd, _ in widths] + [_ctx_spec(wd) for wd, _ in ctx_widths],
        out_shape=[_lat_shape(wd, dt) for wd, dt in widths] + [_ctx_shape(wd, dt) for wd, dt in ctx_widths],
        compiler_params=_params(2),
        name="in_proj",
    )(x, xc, mod, mod, g, w, b, cos_t, sin_t)
    return outs[:6], outs[6:]


VT_ROWS = V_DIM + BF16_SUBLANES
TK = TM
QK_LEAD = 2


def _attn_kernel(*refs, lambda_init, with_ctx, n_cast, layer):
    n_in = 8 if with_ctx else 7
    n_out = 2 if with_ctx else 1
    cast_in = refs[n_in:n_in + n_cast]
    cast_out = refs[n_in + n_cast + n_out:n_in + 2 * n_cast + n_out]
    if with_ctx:
        lamv_ref, sub_ref, q_ref, k_ref, v_ref, kc_ref, vc_ref, qc_ref = refs[:n_in]
        o_ref, oc_ref = refs[n_in + n_cast:n_in + n_cast + n_out]
    else:
        lamv_ref, sub_ref, q_ref, k_ref, v_ref, kc_ref, vc_ref = refs[:n_in]
        (o_ref,) = refs[n_in + n_cast:n_in + n_cast + n_out]
    vt_ref = refs[-1]
    for w_ref, wb_ref in zip(cast_in, cast_out):
        wb_ref[...] = w_ref[...].astype(BF16)
    vt_ref[0:V_DIM, 0:SEQ] = v_ref[0].astype(F32).T.astype(BF16)
    vt_ref[0:V_DIM, SEQ:S_ALL] = vc_ref[0].astype(F32).T.astype(BF16)
    r = lax.broadcasted_iota(jnp.int32, (BF16_SUBLANES, S_ALL), 0)
    vt_ref[V_DIM:VT_ROWS, :] = jnp.where(r == 0, 1.0, 0.0).astype(BF16)

    lv = lamv_ref[...]
    lam = (jnp.exp(jnp.sum(lv[0:1] * lv[1:2], axis=-1, keepdims=True))
           - jnp.exp(jnp.sum(lv[2:3] * lv[3:4], axis=-1, keepdims=True)) + lambda_init)
    ctx_chunks = ((SEQ, S_ALL),)
    all_chunks = tuple((lo, lo + TK) for lo in range(0, SEQ, TK)) + ctx_chunks

    def key_chunk(lo, hi):
        return k_ref[0, lo:hi, :] if hi <= SEQ else kc_ref[0, lo - SEQ:hi - SEQ, :]

    def query_tile(t):
        return q_ref[0, t * TM:(t + 1) * TM, :] if t < NT_LAT else qc_ref[0]

    def store_tile(t, val):
        if t < NT_LAT:
            o_ref[0, t * TM:(t + 1) * TM, :] = val
        else:
            oc_ref[0] = val

    def bcast8(c8):
        return jnp.broadcast_to(jnp.max(c8, axis=0, keepdims=True), (SUBLANES, 2 * TM))

    def start_tile(t):
        qt = query_tile(t).astype(F32).T
        row = lax.broadcasted_iota(jnp.int32, qt.shape, 0)
        zero = jnp.zeros_like(qt)
        qq = jnp.concatenate([jnp.where(row < QK_DIM, qt, zero), jnp.where(row >= QK_DIM, qt, zero)],
                             axis=1).astype(BF16)
        return dict(qq=qq, s={}, cmax=[], ref=None, acc=None)

    state = {}

    def emit_scores(t, idx, chunks):
        if idx == 0:
            state[t] = start_tile(t)
        d = state[t]
        lo, hi = chunks[idx]
        sq = _dot(key_chunk(lo, hi), d["qq"]).reshape((hi - lo) // SUBLANES, SUBLANES, 2 * TM)
        d["s"][idx] = sq
        d["cmax"].append(jnp.max(sq, axis=0))

    def emit_mix(t, idx, chunks):
        d = state[t]
        lo, hi = chunks[idx]
        s = d["s"].pop(idx)
        ref = d["ref"]
        new_ref = bcast8(d["cmax"][idx])
        if idx > 0:
            new_ref = jnp.maximum(ref, new_ref)
        e = jnp.exp2(s - new_ref[None]).reshape(hi - lo, 2 * TM).astype(BF16)
        o = _dot(vt_ref[:, lo:hi], e)
        if idx == 0:
            acc = o
        else:
            alpha = jnp.exp2(ref - new_ref)
            acc = (d["acc"].reshape(VT_ROWS // SUBLANES, SUBLANES, 2 * TM) * alpha[None]).reshape(VT_ROWS, 2 * TM) + o
        d["ref"], d["acc"] = new_ref, acc
        if idx == len(chunks) - 1:
            on = acc[0:V_DIM] * (1.0 / acc[V_DIM:V_DIM + 1])
            o_t = on[:, 0:TM] - lam * on[:, TM:2 * TM]
            store_tile(t, (_rms(o_t.T, sub_ref[layer:layer + 1, :]) * (1.0 - lambda_init)).astype(BF16))
            del state[t]

    tiles = [(t, all_chunks) for t in range(NT_LAT)] + ([(NT_LAT, ctx_chunks)] if with_ctx else [])
    items = [(t, idx, chunks) for t, chunks in tiles for idx in range(len(chunks))]
    for g in range(len(items) + QK_LEAD):
        if g < len(items):
            emit_scores(*items[g])
        if g >= QK_LEAD:
            emit_mix(*items[g - QK_LEAD])


def _attention(l, lamv, subln_g, q, k, v, kc, vc, qc, *, lambda_init, with_ctx, cast=()):
    lat = pl.BlockSpec((1, SEQ, V_DIM), lambda b, h: (b, 0, h))
    ctx = pl.BlockSpec((1, CTX_LEN, V_DIM), lambda b, h: (b, 0, h))
    n_steps = BATCH * ATTN_HEADS
    cast_specs = [pl.BlockSpec((w.shape[0] // n_steps, w.shape[1]), lambda b, h: (b * ATTN_HEADS + h, 0))
                  for w in cast]
    assert all(w.shape[0] % (n_steps * BF16_SUBLANES) == 0 for w in cast)
    return pl.pallas_call(
        functools.partial(_attn_kernel, lambda_init=lambda_init, with_ctx=with_ctx, n_cast=len(cast), layer=l),
        grid=(BATCH, ATTN_HEADS),
        in_specs=[_layer_resident((4, QK_DIM), l), _stacked_vec(V_DIM), lat, lat, lat, ctx, ctx]
                 + ([ctx] if with_ctx else []) + cast_specs,
        out_specs=([lat, ctx] if with_ctx else [lat]) + cast_specs,
        out_shape=[_lat_shape(ATTN_V_WIDTH, BF16)] + ([_ctx_shape(ATTN_V_WIDTH, BF16)] if with_ctx else [])
                  + [jax.ShapeDtypeStruct(w.shape, BF16) for w in cast],
        scratch_shapes=[pltpu.VMEM((VT_ROWS, S_ALL), BF16)],
        compiler_params=_params(2),
        name="attention",
    )(lamv, subln_g, q, k, v, kc, vc, *([qc] if with_ctx else []), *cast)


FOURIER_BATCHES = 2


def _dft_cos_sin(n):
    j = np.arange(n, dtype=np.int64)
    ang = 2.0 * np.pi * ((j[:, None] * j[None, :]) % n).astype(np.float64) / n
    return np.cos(ang).astype(np.float32), np.sin(ang).astype(np.float32)


def _seq_dft_mats(n):
    c, s = _dft_cos_sin(n)
    h = n // 2
    rev = np.zeros((h, h), np.float32)
    rev[np.arange(1, h), h - np.arange(1, h)] = 1.0
    nyq = np.zeros((BF16_SUBLANES, n), np.float32)
    nyq[0] = c[h]
    return [c[:h], s[:h], rev, nyq]


def _fourier_kernel(*refs, with_ctx):
    if with_ctx:
        u_ref, uc_ref, bdc_ref, bds_ref = refs[:4]
        lat_mats, ctx_mats, (y_ref, yc_ref) = refs[4:8], refs[8:12], refs[12:]
    else:
        u_ref, bdc_ref, bds_ref = refs[:3]
        lat_mats, (y_ref,) = refs[3:7], refs[7:]

    def mix(u, mats, out_ref, bb, n):
        c_ref, s_ref, rev_ref, nyq_ref = mats
        h = n // 2
        scale = 1.0 / math.sqrt(n * FOURIER_GROUP_DIM)
        tc = _dot(u, bdc_ref[...]).astype(BF16)
        ts = _dot(u, bds_ref[...]).astype(BF16)
        a = _dot(c_ref[...], tc)
        b = _dot(s_ref[...], ts)
        out_ref[bb, 0:h, :] = ((a - b) * scale).astype(BF16)
        upper = _dot(rev_ref[...], ((a + b) * scale).astype(BF16))
        nyq = _dot(nyq_ref[...], tc)[0:1] * scale
        row = lax.broadcasted_iota(jnp.int32, upper.shape, 0)
        out_ref[bb, h:n, :] = jnp.where(row == 0, nyq, upper).astype(BF16)

    for bb in range(FOURIER_BATCHES):
        mix(u_ref[bb], lat_mats, y_ref, bb, SEQ)
        if with_ctx:
            mix(uc_ref[bb], ctx_mats, yc_ref, bb, CTX_LEN)


def _fourier(u, uc, mats, *, with_ctx):
    if not with_ctx:
        mats = mats[:6]
    lat = pl.BlockSpec((FOURIER_BATCHES, SEQ, FOURIER_WIDTH), lambda b: (b, 0, 0))
    ctx = pl.BlockSpec((FOURIER_BATCHES, CTX_LEN, FOURIER_WIDTH), lambda b: (b, 0, 0))
    return pl.pallas_call(
        functools.partial(_fourier_kernel, with_ctx=with_ctx),
        grid=(BATCH // FOURIER_BATCHES,),
        in_specs=([lat, ctx] if with_ctx else [lat]) + [_resident(m.shape) for m in mats],
        out_specs=[lat, ctx] if with_ctx else [lat],
        out_shape=[_lat_shape(FOURIER_WIDTH, BF16)] + ([_ctx_shape(FOURIER_WIDTH, BF16)] if with_ctx else []),
        compiler_params=_params(1),
        name="fourier",
    )(u, *([uc] if with_ctx else []), *mats)


MERGE_NB = 256


def _merge_kernel(*refs, with_ctx, layer):
    (x_ref, mod_ref, g_ref, yf_ref, bg_ref, z_ref, zp_ref, zn_ref, o_ref, cw_ref, cb_ref, wfo_ref, wco_ref,
     wao_ref, win_ref, bin_ref, wout_ref) = refs[:17]
    if with_ctx:
        xc_ref, modc_ref, yfc_ref, bgc_ref, zc_ref, oc_ref, out_ref, outc_ref = refs[17:]
    else:
        (out_ref,) = refs[17:]
    i = pl.program_id(1)
    cw = cw_ref[...]

    def conv_in(z, bg, prev_row, next_row):
        n = z.shape[0]
        row = lax.broadcasted_iota(jnp.int32, z.shape, 0)
        z_prev = jnp.where(row == 0, prev_row, pltpu.roll(z, 1, axis=0))
        z_next = jnp.where(row == n - 1, next_row, pltpu.roll(z, n - 1, axis=0))
        conv = z_prev * cw[0:1] + z * cw[1:2] + z_next * cw[2:3] + cb_ref[layer:layer + 1, :]
        return (bg * conv).astype(BF16)

    def chain(x, mod, y_a_in, y_f_in, y_c_in):
        h = _norm_mod(x, g_ref[layer:layer + 1, :], mod, 0).astype(BF16)
        blocks = []
        for c in range(D_MODEL // MERGE_NB):
            cs = slice(c * MERGE_NB, (c + 1) * MERGE_NB)

            def gate(n):
                lo = G_OFF + n * D_MODEL + c * MERGE_NB
                return _sigmoid(_dot(h, win_ref[:, lo:lo + MERGE_NB]) + bin_ref[layer:layer + 1, lo:lo + MERGE_NB])

            y_a = _dot(y_a_in, wao_ref[:, cs])
            y_f = _dot(y_f_in, wfo_ref[:, cs])
            y_c = _dot(y_c_in, wco_ref[:, cs])
            blocks.append((gate(2) * y_a + gate(0) * y_f + gate(1) * y_c).astype(BF16))
        y = jnp.concatenate(blocks, axis=1)
        return x + mod[:, 2 * D_MODEL:3 * D_MODEL] * _dot(y, wout_ref[...])

    prev_row = jnp.where(i >= 1, zp_ref[0, SUBLANES - 1:SUBLANES, :], 0.0)
    next_row = jnp.where(i < NS_LAT - 1, zn_ref[0, 0:1, :], 0.0)
    yc_in = conv_in(z_ref[0], bg_ref[0], prev_row, next_row)
    for hf in range(CHAINS):
        r = _half_rows(hf)
        out_ref[0, r, :] = chain(x_ref[0, r, :], _lat_mod(mod_ref), o_ref[0, r, :], yf_ref[0, r, :], yc_in[r, :])

    if with_ctx:
        @_on_first_step
        def _():
            ycc_in = conv_in(zc_ref[0], bgc_ref[0], 0.0, 0.0)
            outc_ref[0] = chain(xc_ref[0], _ctx_mod(modc_ref), oc_ref[0], yfc_ref[0], ycc_in)


def _merge(l, x, mod, g, yf, bg, z, o, ctx_parts, cw, cb, wfo, wco, wao, w_in, b_in, wout, *, with_ctx):
    per8 = TS // SUBLANES
    halo_prev = pl.BlockSpec((1, SUBLANES, CONV_WIDTH),
                             lambda b, i: (b, jnp.maximum(i * per8 - 1, 0), 0))
    halo_next = pl.BlockSpec((1, SUBLANES, CONV_WIDTH),
                             lambda b, i: (b, jnp.minimum((i + 1) * per8, SEQ // SUBLANES - 1), 0))
    weights = [cw, cb, wfo, wco, wao, w_in, b_in, wout]
    weight_specs = [_stacked_vec(w.shape[1]) if w.ndim == 2 else _layer_resident(w.shape[1:], l) for w in weights]
    mod_lat, mod_ctx = _mod_specs(l)
    in_specs = ([_lat_spec(D_MODEL), mod_lat, _stacked_vec(D_MODEL), _lat_spec(FOURIER_WIDTH),
                 _lat_spec(CONV_WIDTH), _lat_spec(CONV_WIDTH), halo_prev, halo_next, _lat_spec(ATTN_V_WIDTH)]
                + weight_specs)
    args = [x, mod, g, yf, bg, z, z, z, o, *weights]
    if with_ctx:
        xc, yfc, bgc, zc, oc = ctx_parts
        in_specs += [_ctx_spec(D_MODEL), mod_ctx, _ctx_spec(FOURIER_WIDTH), _ctx_spec(CONV_WIDTH),
                     _ctx_spec(CONV_WIDTH), _ctx_spec(ATTN_V_WIDTH)]
        args += [xc, mod, yfc, bgc, zc, oc]
    return pl.pallas_call(
        functools.partial(_merge_kernel, with_ctx=with_ctx, layer=l),
        grid=(BATCH, NS_LAT),
        in_specs=in_specs,
        out_specs=[_lat_spec(D_MODEL)] + ([_ctx_spec(D_MODEL)] if with_ctx else []),
        out_shape=[_lat_shape(D_MODEL, F32)] + ([_ctx_shape(D_MODEL, F32)] if with_ctx else []),
        compiler_params=_params(2),
        name="merge",
    )(*args)


FFN_NB = 256


def _ffn_kernel(*refs, last, layer):
    x_ref, mod_ref, g_ref, wg_ref, wu_ref, wd_ref, fg_ref = refs[:7]
    if last:
        (out_ref,) = refs[7:]
    else:
        xc_ref, modc_ref, out_ref, outc_ref = refs[7:]

    def chain(x, mod):
        h = _norm_mod(x, g_ref[layer:layer + 1, :], mod, 3).astype(BF16)
        blocks = []
        for c in range(D_FF // FFN_NB):
            cs = slice(c * FFN_NB, (c + 1) * FFN_NB)
            gate = _dot(h, wg_ref[:, cs])
            blocks.append((gate * _sigmoid(gate) * _dot(h, wu_ref[:, cs])).astype(BF16))
        a = jnp.concatenate(blocks, axis=1)
        x2 = x + mod[:, 5 * D_MODEL:6 * D_MODEL] * _dot(a, wd_ref[...])
        return _rms(x2, fg_ref[...]) if last else x2

    for hf in range(CHAINS):
        r = _half_rows(hf)
        out_ref[0, r, :] = chain(x_ref[0, r, :], _lat_mod(mod_ref))

    if not last:
        @_on_first_step
        def _():
            outc_ref[0] = chain(xc_ref[0], _ctx_mod(modc_ref))


def _ffn(l, x1, x1c, mod, g, wg, wu, wd, fg, *, last):
    mod_lat, mod_ctx = _mod_specs(l)
    in_specs = [_lat_spec(D_MODEL), mod_lat, _stacked_vec(D_MODEL), _layer_resident(wg.shape[1:], l),
                _layer_resident(wu.shape[1:], l), _layer_resident(wd.shape[1:], l), _resident((1, D_MODEL))]
    args = [x1, mod, g, wg, wu, wd, fg]
    if not last:
        in_specs += [_ctx_spec(D_MODEL), mod_ctx]
        args += [x1c, mod]
    return pl.pallas_call(
        functools.partial(_ffn_kernel, last=last, layer=l),
        grid=(BATCH, NS_LAT),
        in_specs=in_specs,
        out_specs=[_lat_spec(D_MODEL)] + ([] if last else [_ctx_spec(D_MODEL)]),
        out_shape=[_lat_shape(D_MODEL, F32)] + ([] if last else [_ctx_shape(D_MODEL, F32)]),
        compiler_params=_params(2),
        name="ffn",
    )(*args)


def _rope_tables():
    pos = jnp.arange(SEQ)
    freqs = ROPE_BASE ** (-jnp.arange(0, ROPE_AXIS_DIM, 2, dtype=F32) / ROPE_AXIS_DIM)
    ang_r = (pos // GRID_W).astype(F32)[:, None] * freqs[None, :]
    ang_c = (pos % GRID_W).astype(F32)[:, None] * freqs[None, :]
    cos64 = jnp.concatenate([jnp.cos(ang_r)] * 2 + [jnp.cos(ang_c)] * 2, axis=-1)
    sin64 = jnp.concatenate([-jnp.sin(ang_r), jnp.sin(ang_r), -jnp.sin(ang_c), jnp.sin(ang_c)], axis=-1)
    return jnp.tile(cos64, (1, 2)), jnp.tile(sin64, (1, 2))


def _fourier_mats():
    cg, sg = _dft_cos_sin(FOURIER_GROUP_DIM)
    eye = np.eye(FOURIER_GROUPS, dtype=np.float32)
    mats = [np.kron(eye, cg), np.kron(eye, sg), *_seq_dft_mats(SEQ), *_seq_dft_mats(CTX_LEN)]
    return [jnp.asarray(m).astype(BF16) for m in mats]


def kernel(x, c, ctx, c_ctx, w_ada, b_ada, norm1_g, norm2_g, w_in, b_in, conv_w, conv_b, w_fourier_out,
           w_conv_out, w_attn_out, lambda_q1, lambda_k1, lambda_q2, lambda_k2, subln_g, w_out, w_ffn_gate,
           w_ffn_up, w_ffn_down, final_g):
    cc = jnp.concatenate([c, c_ctx[None, :], jnp.zeros((MOD_ROWS - BATCH - 1, D_MODEL), F32)], axis=0)
    mod = _ada(cc, w_ada, b_ada).reshape(DEPTH * MOD_ROWS, 6 * D_MODEL)
    cos_t, sin_t = _rope_tables()
    mats = _fourier_mats()
    weights_f32 = (w_in, w_fourier_out, w_conv_out, w_attn_out, w_out, w_ffn_gate, w_ffn_up, w_ffn_down)
    lamv = jnp.stack([lambda_q1, lambda_k1, lambda_q2, lambda_k2], axis=1).astype(F32)
    xl, xc = x, ctx

    for l in range(DEPTH):
        last = l == DEPTH - 1
        with_ctx = not last
        lambda_init = 0.8 - 0.6 * math.exp(-0.3 * l)
        (u, bg, z, q, k, v), ctx_p = _in_proj(l, xl, xc, mod, norm1_g, w_in if l == 0 else w_in_b, l, b_in, cos_t,
                                              sin_t, ctx_full=with_ctx)
        uc, bgc, zc, qc, kc, vc = ctx_p if with_ctx else (None, None, None, None) + tuple(ctx_p)
        cast = [a.reshape(-1, a.shape[-1]) for a in weights_f32] if l == 0 else []
        o = _attention(l, lamv, subln_g, q, k, v, kc, vc, qc, lambda_init=lambda_init, with_ctx=with_ctx,
                       cast=cast)
        if l == 0:
            n_o = len(o) - len(cast)
            w_in_b, w_fo_b, w_co_b, w_ao_b, w_out_b, w_g_b, w_u_b, w_d_b = (
                wb.reshape(a.shape) for wb, a in zip(o[n_o:], weights_f32))
            o = o[:n_o]
        yf = _fourier(u, uc, mats, with_ctx=with_ctx)
        ctx_parts = (xc, yf[1], bgc, zc, o[1]) if with_ctx else None
        x1 = _merge(l, xl, mod, norm1_g, yf[0], bg, z, o[0], ctx_parts, conv_w, conv_b, w_fo_b, w_co_b, w_ao_b,
                    w_in_b, b_in, w_out_b, with_ctx=with_ctx)
        out = _ffn(l, x1[0], x1[1] if with_ctx else None, mod, norm2_g, w_g_b, w_u_b, w_d_b, final_g.reshape(1, -1),
                   last=last)
        xl = out[0]
        if with_ctx:
            xc = out[1]
    return xl
```

```python
import functools
import math

import numpy as np
import jax
import jax.numpy as jnp
from jax import lax
from jax.experimental import pallas as pl
from jax.experimental.pallas import tpu as pltpu

D_MODEL = 1024
BATCH = 8
SEQ = 2048
DEPTH = 2
GRID_W = 64
CTX_LEN = 256
FOURIER_GROUPS = 4
FOURIER_GROUP_DIM = 64
FOURIER_WIDTH = FOURIER_GROUPS * FOURIER_GROUP_DIM
CONV_WIDTH = 256
ATTN_HEADS = 4
QK_DIM = 64
V_DIM = 2 * QK_DIM
ATTN_QK_WIDTH = ATTN_HEADS * 2 * QK_DIM
ATTN_V_WIDTH = ATTN_HEADS * V_DIM
ROPE_BASE = 10000.0
ROPE_AXIS_DIM = QK_DIM // 2
N_BRANCH = 3
D_FF = 2816
EPS = 1e-6

F_OFF = 0
CB_OFF = F_OFF + FOURIER_WIDTH
CC_OFF = CB_OFF + CONV_WIDTH
CX_OFF = CC_OFF + CONV_WIDTH
Q_OFF = CX_OFF + CONV_WIDTH
K_OFF = Q_OFF + ATTN_QK_WIDTH
V_OFF = K_OFF + ATTN_QK_WIDTH
G_OFF = V_OFF + ATTN_V_WIDTH
D_IN = G_OFF + N_BRANCH * D_MODEL

S_ALL = SEQ + CTX_LEN
TM = 256
CHAINS = 4
TS = CHAINS * TM
NS_LAT = SEQ // TS
NT_LAT = SEQ // TM
MOD_ROWS = 16
CTX_MOD_ROW = BATCH
ADA_TN = 1536
SUBLANES = 8
LANES = 128
BF16_SUBLANES = 16
V7X_VMEM_LIMIT = 56 * 1024 * 1024
Q_SCALE = (QK_DIM ** -0.5) * math.log2(math.e)

BF16 = jnp.bfloat16
F32 = jnp.float32

assert CTX_LEN == TM and BATCH == SUBLANES and CTX_MOD_ROW % SUBLANES == 0


def _dot(a, b):
    return jnp.dot(a, b, preferred_element_type=F32)


def _rms(x, g):
    return x * lax.rsqrt(jnp.mean(x * x, axis=-1, keepdims=True) + EPS) * g


def _norm_mod(x, g, mod, k):
    shift = mod[:, k * D_MODEL:(k + 1) * D_MODEL]
    scale = mod[:, (k + 1) * D_MODEL:(k + 2) * D_MODEL]
    return _rms(x, g) * (1.0 + scale) + shift


def _sigmoid(t):
    return 0.5 * jnp.tanh(0.5 * t) + 0.5


def _resident(shape):
    nd = len(shape)
    return pl.BlockSpec(shape, lambda *_: (0,) * nd, pipeline_mode=pl.Buffered(1))


def _layer_resident(shape, l):
    nd = len(shape)
    return pl.BlockSpec((None,) + tuple(shape), lambda *_: (l,) + (0,) * nd, pipeline_mode=pl.Buffered(1))


def _params(n_axes):
    return pltpu.CompilerParams(dimension_semantics=("arbitrary",) * n_axes,
                                vmem_limit_bytes=V7X_VMEM_LIMIT)


def _lat_spec(width):
    return pl.BlockSpec((1, TS, width), lambda b, i: (b, i, 0))


def _ctx_spec(width):
    return pl.BlockSpec((1, CTX_LEN, width), lambda b, i: (b, 0, 0))


def _mod_specs(l):
    per_layer = MOD_ROWS // SUBLANES
    return [pl.BlockSpec((SUBLANES, 6 * D_MODEL), lambda b, i: (l * per_layer, 0)),
            pl.BlockSpec((SUBLANES, 6 * D_MODEL), lambda b, i: (l * per_layer + CTX_MOD_ROW // SUBLANES, 0))]


def _lat_mod(mod_ref):
    return mod_ref[pl.ds(pl.program_id(0), 1), :]


def _ctx_mod(modc_ref):
    return modc_ref[0:1, :]


def _stacked_vec(n):
    return _resident((DEPTH, n))


def _half_rows(half):
    return slice(half * TM, (half + 1) * TM)


def _on_first_step(body):
    pl.when(pl.program_id(1) == 0)(body)


def _lat_shape(width, dtype):
    return jax.ShapeDtypeStruct((BATCH, SEQ, width), dtype)


def _ctx_shape(width, dtype):
    return jax.ShapeDtypeStruct((BATCH, CTX_LEN, width), dtype)


def _ada_kernel(cc_ref, w_ref, b_ref, o_ref):
    s = jax.nn.silu(cc_ref[...]).astype(BF16)
    o_ref[0] = _dot(s, w_ref[0].astype(BF16)) + b_ref[0]


def _ada(cc, w_ada, b_ada):
    n_col = 6 * D_MODEL
    return pl.pallas_call(
        _ada_kernel,
        grid=(DEPTH, n_col // ADA_TN),
        in_specs=[
            pl.BlockSpec((MOD_ROWS, D_MODEL), lambda l, j: (0, 0)),
            pl.BlockSpec((1, D_MODEL, ADA_TN), lambda l, j: (l, 0, j)),
            pl.BlockSpec((1, 1, ADA_TN), lambda l, j: (l, 0, j)),
        ],
        out_specs=pl.BlockSpec((1, MOD_ROWS, ADA_TN), lambda l, j: (l, 0, j)),
        out_shape=jax.ShapeDtypeStruct((DEPTH, MOD_ROWS, n_col), F32),
        compiler_params=_params(2),
        name="ada",
    )(cc, w_ada, b_ada.reshape(DEPTH, 1, n_col))


def _rope(t, cos, sin_signed):
    lane = lax.broadcasted_iota(jnp.int32, t.shape, 1)
    first = (lane % (ROPE_AXIS_DIM)) < (ROPE_AXIS_DIM // 2)
    half = ROPE_AXIS_DIM // 2
    partner = jnp.where(first, pltpu.roll(t, LANES - half, axis=1), pltpu.roll(t, half, axis=1))
    return t * cos + partner * sin_signed


def _in_proj_kernel(x_ref, xc_ref, mod_ref, modc_ref, g_ref, w_ref, b_ref, cos_ref, sin_ref, *out_refs,
                    ctx_full, cast_w, layer):
    if cast_w:
        w_f32_ref, w_ref, out_refs = w_ref, out_refs[-1], out_refs[:-1]

        @pl.when(jnp.logical_and(pl.program_id(0) == 0, pl.program_id(1) == 0))
        def _():
            w_ref[...] = w_f32_ref[...].astype(BF16)
    lat_out = out_refs[:6]
    ctx_out = out_refs[6:]

    def project(x, mod, r, rope_tables, outs, attn_only):
        h = _norm_mod(x, g_ref[layer:layer + 1, :], mod, 0).astype(BF16)

        def proj(lo, hi):
            return _dot(h, w_ref[:, lo:hi]) + b_ref[layer:layer + 1, lo:hi]

        if attn_only:
            k_ref, v_ref = outs
        else:
            u_ref, bg_ref, z_ref, q_ref, k_ref, v_ref = outs
            u_ref[0, r, :] = proj(F_OFF, CB_OFF).astype(BF16)
            bg_ref[0, r, :] = proj(CB_OFF, CC_OFF)
            z_ref[0, r, :] = proj(CC_OFF, CX_OFF) * proj(CX_OFF, Q_OFF)
            q = proj(Q_OFF, K_OFF)
        k = proj(K_OFF, V_OFF)
        if rope_tables is None:
            if not attn_only:
                q_ref[0, r, :] = (q * Q_SCALE).astype(BF16)
            k_ref[0, r, :] = k.astype(BF16)
        else:
            cos, sin = rope_tables
            for hd in range(ATTN_HEADS):
                sl = slice(hd * V_DIM, (hd + 1) * V_DIM)
                q_ref[0, r, sl] = (_rope(q[:, sl], cos, sin) * Q_SCALE).astype(BF16)
                k_ref[0, r, sl] = _rope(k[:, sl], cos, sin).astype(BF16)
        v_ref[0, r, :] = proj(V_OFF, G_OFF).astype(BF16)

    for hf in range(CHAINS):
        r = _half_rows(hf)
        project(x_ref[0, r, :], _lat_mod(mod_ref), r, (cos_ref[r, :], sin_ref[r, :]), lat_out, False)

    @_on_first_step
    def _():
        project(xc_ref[0], _ctx_mod(modc_ref), slice(0, CTX_LEN), None, ctx_out, not ctx_full)


def _in_proj(l, x, xc, mod, g, w, w_layer, b, cos_t, sin_t, *, ctx_full):
    widths = [(FOURIER_WIDTH, BF16), (CONV_WIDTH, F32), (CONV_WIDTH, F32), (ATTN_QK_WIDTH, BF16),
              (ATTN_QK_WIDTH, BF16), (ATTN_V_WIDTH, BF16)]
    ctx_widths = widths if ctx_full else widths[4:]
    cast_w = w.dtype != BF16
    outs = pl.pallas_call(
        functools.partial(_in_proj_kernel, ctx_full=ctx_full, cast_w=cast_w, layer=l),
        scratch_shapes=[pltpu.VMEM((D_MODEL, G_OFF), BF16)] if cast_w else [],
        grid=(BATCH, NS_LAT),
        in_specs=[_lat_spec(D_MODEL), _ctx_spec(D_MODEL), *_mod_specs(l), _stacked_vec(D_MODEL),
                  _layer_resident((D_MODEL, G_OFF), w_layer), _stacked_vec(D_IN),
                  pl.BlockSpec((TS, LANES), lambda bb, i: (i, 0)),
                  pl.BlockSpec((TS, LANES), lambda bb, i: (i, 0))],
        out_specs=[_lat_spec(wd) for wd, _ in widths] + [_ctx_spec(wd) for wd, _ in ctx_widths],
        out_shape=[_lat_shape(wd, dt) for wd, dt in widths] + [_ctx_shape(wd, dt) for wd, dt in ctx_widths],
        compiler_params=_params(2),
        name="in_proj",
    )(x, xc, mod, mod, g, w, b, cos_t, sin_t)
    return outs[:6], outs[6:]


VT_ROWS = V_DIM + BF16_SUBLANES
TK = TM
QK_LEAD = 1


def _attn_kernel(*refs, lambda_init, with_ctx, n_cast, layer):
    n_in = 8 if with_ctx else 7
    n_out = 2 if with_ctx else 1
    cast_in = refs[n_in:n_in + n_cast]
    cast_out = refs[n_in + n_cast + n_out:n_in + 2 * n_cast + n_out]
    if with_ctx:
        lamv_ref, sub_ref, q_ref, k_ref, v_ref, kc_ref, vc_ref, qc_ref = refs[:n_in]
        o_ref, oc_ref = refs[n_in + n_cast:n_in + n_cast + n_out]
    else:
        lamv_ref, sub_ref, q_ref, k_ref, v_ref, kc_ref, vc_ref = refs[:n_in]
        (o_ref,) = refs[n_in + n_cast:n_in + n_cast + n_out]
    vt_ref = refs[-1]
    for w_ref, wb_ref in zip(cast_in, cast_out):
        wb_ref[...] = w_ref[...].astype(BF16)
    vt_ref[0:V_DIM, 0:SEQ] = v_ref[0].astype(F32).T.astype(BF16)
    vt_ref[0:V_DIM, SEQ:S_ALL] = vc_ref[0].astype(F32).T.astype(BF16)
    r = lax.broadcasted_iota(jnp.int32, (BF16_SUBLANES, S_ALL), 0)
    vt_ref[V_DIM:VT_ROWS, :] = jnp.where(r == 0, 1.0, 0.0).astype(BF16)

    lv = lamv_ref[...]
    lam = (jnp.exp(jnp.sum(lv[0:1] * lv[1:2], axis=-1, keepdims=True))
           - jnp.exp(jnp.sum(lv[2:3] * lv[3:4], axis=-1, keepdims=True)) + lambda_init)
    ctx_chunks = ((SEQ, S_ALL),)
    all_chunks = tuple((lo, lo + TK) for lo in range(0, SEQ, TK)) + ctx_chunks

    def key_chunk(lo, hi):
        return k_ref[0, lo:hi, :] if hi <= SEQ else kc_ref[0, lo - SEQ:hi - SEQ, :]

    def query_tile(t):
        return q_ref[0, t * TM:(t + 1) * TM, :] if t < NT_LAT else qc_ref[0]

    def store_tile(t, val):
        if t < NT_LAT:
            o_ref[0, t * TM:(t + 1) * TM, :] = val
        else:
            oc_ref[0] = val

    def bcast8(c8):
        return jnp.broadcast_to(jnp.max(c8, axis=0, keepdims=True), (SUBLANES, 2 * TM))

    def start_tile(t):
        qt = query_tile(t).astype(F32).T
        row = lax.broadcasted_iota(jnp.int32, qt.shape, 0)
        zero = jnp.zeros_like(qt)
        qq = jnp.concatenate([jnp.where(row < QK_DIM, qt, zero), jnp.where(row >= QK_DIM, qt, zero)],
                             axis=1).astype(BF16)
        return dict(qq=qq, s={}, cmax=[], ref=None, acc=None)

    state = {}

    def emit_scores(t, idx, chunks):
        if idx == 0:
            state[t] = start_tile(t)
        d = state[t]
        lo, hi = chunks[idx]
        sq = _dot(key_chunk(lo, hi), d["qq"]).reshape((hi - lo) // SUBLANES, SUBLANES, 2 * TM)
        d["s"][idx] = sq
        d["cmax"].append(jnp.max(sq, axis=0))

    def emit_mix(t, idx, chunks):
        d = state[t]
        lo, hi = chunks[idx]
        s = d["s"].pop(idx)
        ref = d["ref"]
        new_ref = bcast8(d["cmax"][idx])
        if idx > 0:
            new_ref = jnp.maximum(ref, new_ref)
        e = jnp.exp2(s - new_ref[None]).reshape(hi - lo, 2 * TM).astype(BF16)
        o = _dot(vt_ref[:, lo:hi], e)
        if idx == 0:
            acc = o
        else:
            alpha = jnp.exp2(ref - new_ref)
            acc = (d["acc"].reshape(VT_ROWS // SUBLANES, SUBLANES, 2 * TM) * alpha[None]).reshape(VT_ROWS, 2 * TM) + o
        d["ref"], d["acc"] = new_ref, acc
        if idx == len(chunks) - 1:
            on = acc[0:V_DIM] * (1.0 / acc[V_DIM:V_DIM + 1])
            o_t = on[:, 0:TM] - lam * on[:, TM:2 * TM]
            store_tile(t, (_rms(o_t.T, sub_ref[layer:layer + 1, :]) * (1.0 - lambda_init)).astype(BF16))
            del state[t]

    tiles = [(t, all_chunks) for t in range(NT_LAT)] + ([(NT_LAT, ctx_chunks)] if with_ctx else [])
    items = [(t, idx, chunks) for t, chunks in tiles for idx in range(len(chunks))]
    for g in range(len(items) + QK_LEAD):
        if g < len(items):
            emit_scores(*items[g])
        if g >= QK_LEAD:
            emit_mix(*items[g - QK_LEAD])


def _attention(l, lamv, subln_g, q, k, v, kc, vc, qc, *, lambda_init, with_ctx, cast=()):
    lat = pl.BlockSpec((1, SEQ, V_DIM), lambda b, h: (b, 0, h))
    ctx = pl.BlockSpec((1, CTX_LEN, V_DIM), lambda b, h: (b, 0, h))
    n_steps = BATCH * ATTN_HEADS
    cast_specs = [pl.BlockSpec((w.shape[0] // n_steps, w.shape[1]), lambda b, h: (b * ATTN_HEADS + h, 0))
                  for w in cast]
    assert all(w.shape[0] % (n_steps * BF16_SUBLANES) == 0 for w in cast)
    return pl.pallas_call(
        functools.partial(_attn_kernel, lambda_init=lambda_init, with_ctx=with_ctx, n_cast=len(cast), layer=l),
        grid=(BATCH, ATTN_HEADS),
        in_specs=[_layer_resident((4, QK_DIM), l), _stacked_vec(V_DIM), lat, lat, lat, ctx, ctx]
                 + ([ctx] if with_ctx else []) + cast_specs,
        out_specs=([lat, ctx] if with_ctx else [lat]) + cast_specs,
        out_shape=[_lat_shape(ATTN_V_WIDTH, BF16)] + ([_ctx_shape(ATTN_V_WIDTH, BF16)] if with_ctx else [])
                  + [jax.ShapeDtypeStruct(w.shape, BF16) for w in cast],
        scratch_shapes=[pltpu.VMEM((VT_ROWS, S_ALL), BF16)],
        compiler_params=_params(2),
        name="attention",
    )(lamv, subln_g, q, k, v, kc, vc, *([qc] if with_ctx else []), *cast)


FOURIER_BATCHES = 2


def _dft_cos_sin(n):
    j = np.arange(n, dtype=np.int64)
    ang = 2.0 * np.pi * ((j[:, None] * j[None, :]) % n).astype(np.float64) / n
    return np.cos(ang).astype(np.float32), np.sin(ang).astype(np.float32)


def _seq_dft_mats(n):
    c, s = _dft_cos_sin(n)
    h = n // 2
    rev = np.zeros((h, h), np.float32)
    rev[np.arange(1, h), h - np.arange(1, h)] = 1.0
    nyq = np.zeros((BF16_SUBLANES, n), np.float32)
    nyq[0] = c[h]
    return [c[:h], s[:h], rev, nyq]


def _fourier_kernel(*refs, with_ctx):
    if with_ctx:
        u_ref, uc_ref, bdc_ref, bds_ref = refs[:4]
        lat_mats, ctx_mats, (y_ref, yc_ref) = refs[4:8], refs[8:12], refs[12:]
    else:
        u_ref, bdc_ref, bds_ref = refs[:3]
        lat_mats, (y_ref,) = refs[3:7], refs[7:]

    def mix(u, mats, out_ref, bb, n):
        c_ref, s_ref, rev_ref, nyq_ref = mats
        h = n // 2
        scale = 1.0 / math.sqrt(n * FOURIER_GROUP_DIM)
        tc = _dot(u, bdc_ref[...]).astype(BF16)
        ts = _dot(u, bds_ref[...]).astype(BF16)
        a = _dot(c_ref[...], tc)
        b = _dot(s_ref[...], ts)
        out_ref[bb, 0:h, :] = ((a - b) * scale).astype(BF16)
        upper = _dot(rev_ref[...], ((a + b) * scale).astype(BF16))
        nyq = _dot(nyq_ref[...], tc)[0:1] * scale
        row = lax.broadcasted_iota(jnp.int32, upper.shape, 0)
        out_ref[bb, h:n, :] = jnp.where(row == 0, nyq, upper).astype(BF16)

    for bb in range(FOURIER_BATCHES):
        mix(u_ref[bb], lat_mats, y_ref, bb, SEQ)
        if with_ctx:
            mix(uc_ref[bb], ctx_mats, yc_ref, bb, CTX_LEN)


def _fourier(u, uc, mats, *, with_ctx):
    if not with_ctx:
        mats = mats[:6]
    lat = pl.BlockSpec((FOURIER_BATCHES, SEQ, FOURIER_WIDTH), lambda b: (b, 0, 0))
    ctx = pl.BlockSpec((FOURIER_BATCHES, CTX_LEN, FOURIER_WIDTH), lambda b: (b, 0, 0))
    return pl.pallas_call(
        functools.partial(_fourier_kernel, with_ctx=with_ctx),
        grid=(BATCH // FOURIER_BATCHES,),
        in_specs=([lat, ctx] if with_ctx else [lat]) + [_resident(m.shape) for m in mats],
        out_specs=[lat, ctx] if with_ctx else [lat],
        out_shape=[_lat_shape(FOURIER_WIDTH, BF16)] + ([_ctx_shape(FOURIER_WIDTH, BF16)] if with_ctx else []),
        compiler_params=_params(1),
        name="fourier",
    )(u, *([uc] if with_ctx else []), *mats)


MERGE_NB = 256


def _merge_kernel(*refs, with_ctx, layer):
    (x_ref, mod_ref, g_ref, yf_ref, bg_ref, z_ref, zp_ref, zn_ref, o_ref, cw_ref, cb_ref, wfo_ref, wco_ref,
     wao_ref, win_ref, bin_ref, wout_ref) = refs[:17]
    if with_ctx:
        xc_ref, modc_ref, yfc_ref, bgc_ref, zc_ref, oc_ref, out_ref, outc_ref = refs[17:]
    else:
        (out_ref,) = refs[17:]
    i = pl.program_id(1)
    cw = cw_ref[...]

    def conv_in(z, bg, prev_row, next_row):
        n = z.shape[0]
        row = lax.broadcasted_iota(jnp.int32, z.shape, 0)
        z_prev = jnp.where(row == 0, prev_row, pltpu.roll(z, 1, axis=0))
        z_next = jnp.where(row == n - 1, next_row, pltpu.roll(z, n - 1, axis=0))
        conv = z_prev * cw[0:1] + z * cw[1:2] + z_next * cw[2:3] + cb_ref[layer:layer + 1, :]
        return (bg * conv).astype(BF16)

    def chain(x, mod, y_a_in, y_f_in, y_c_in):
        h = _norm_mod(x, g_ref[layer:layer + 1, :], mod, 0).astype(BF16)
        blocks = []
        for c in range(D_MODEL // MERGE_NB):
            cs = slice(c * MERGE_NB, (c + 1) * MERGE_NB)

            def gate(n):
                lo = G_OFF + n * D_MODEL + c * MERGE_NB
                return _sigmoid(_dot(h, win_ref[:, lo:lo + MERGE_NB]) + bin_ref[layer:layer + 1, lo:lo + MERGE_NB])

            y_a = _dot(y_a_in, wao_ref[:, cs])
            y_f = _dot(y_f_in, wfo_ref[:, cs])
            y_c = _dot(y_c_in, wco_ref[:, cs])
            blocks.append((gate(2) * y_a + gate(0) * y_f + gate(1) * y_c).astype(BF16))
        y = jnp.concatenate(blocks, axis=1)
        return x + mod[:, 2 * D_MODEL:3 * D_MODEL] * _dot(y, wout_ref[...])

    prev_row = jnp.where(i >= 1, zp_ref[0, SUBLANES - 1:SUBLANES, :], 0.0)
    next_row = jnp.where(i < NS_LAT - 1, zn_ref[0, 0:1, :], 0.0)
    yc_in = conv_in(z_ref[0], bg_ref[0], prev_row, next_row)
    for hf in range(CHAINS):
        r = _half_rows(hf)
        out_ref[0, r, :] = chain(x_ref[0, r, :], _lat_mod(mod_ref), o_ref[0, r, :], yf_ref[0, r, :], yc_in[r, :])

    if with_ctx:
        @_on_first_step
        def _():
            ycc_in = conv_in(zc_ref[0], bgc_ref[0], 0.0, 0.0)
            outc_ref[0] = chain(xc_ref[0], _ctx_mod(modc_ref), oc_ref[0], yfc_ref[0], ycc_in)


def _merge(l, x, mod, g, yf, bg, z, o, ctx_parts, cw, cb, wfo, wco, wao, w_in, b_in, wout, *, with_ctx):
    per8 = TS // SUBLANES
    halo_prev = pl.BlockSpec((1, SUBLANES, CONV_WIDTH),
                             lambda b, i: (b, jnp.maximum(i * per8 - 1, 0), 0))
    halo_next = pl.BlockSpec((1, SUBLANES, CONV_WIDTH),
                             lambda b, i: (b, jnp.minimum((i + 1) * per8, SEQ // SUBLANES - 1), 0))
    weights = [cw, cb, wfo, wco, wao, w_in, b_in, wout]
    weight_specs = [_stacked_vec(w.shape[1]) if w.ndim == 2 else _layer_resident(w.shape[1:], l) for w in weights]
    mod_lat, mod_ctx = _mod_specs(l)
    in_specs = ([_lat_spec(D_MODEL), mod_lat, _stacked_vec(D_MODEL), _lat_spec(FOURIER_WIDTH),
                 _lat_spec(CONV_WIDTH), _lat_spec(CONV_WIDTH), halo_prev, halo_next, _lat_spec(ATTN_V_WIDTH)]
                + weight_specs)
    args = [x, mod, g, yf, bg, z, z, z, o, *weights]
    if with_ctx:
        xc, yfc, bgc, zc, oc = ctx_parts
        in_specs += [_ctx_spec(D_MODEL), mod_ctx, _ctx_spec(FOURIER_WIDTH), _ctx_spec(CONV_WIDTH),
                     _ctx_spec(CONV_WIDTH), _ctx_spec(ATTN_V_WIDTH)]
        args += [xc, mod, yfc, bgc, zc, oc]
    return pl.pallas_call(
        functools.partial(_merge_kernel, with_ctx=with_ctx, layer=l),
        grid=(BATCH, NS_LAT),
        in_specs=in_specs,
        out_specs=[_lat_spec(D_MODEL)] + ([_ctx_spec(D_MODEL)] if with_ctx else []),
        out_shape=[_lat_shape(D_MODEL, F32)] + ([_ctx_shape(D_MODEL, F32)] if with_ctx else []),
        compiler_params=_params(2),
        name="merge",
    )(*args)


FFN_NB = 256


def _ffn_kernel(*refs, last, layer):
    x_ref, mod_ref, g_ref, wg_ref, wu_ref, wd_ref, fg_ref = refs[:7]
    if last:
        (out_ref,) = refs[7:]
    else:
        xc_ref, modc_ref, out_ref, outc_ref = refs[7:]

    def chain(x, mod):
        h = _norm_mod(x, g_ref[layer:layer + 1, :], mod, 3).astype(BF16)
        blocks = []
        for c in range(D_FF // FFN_NB):
            cs = slice(c * FFN_NB, (c + 1) * FFN_NB)
            gate = _dot(h, wg_ref[:, cs])
            blocks.append((gate * _sigmoid(gate) * _dot(h, wu_ref[:, cs])).astype(BF16))
        a = jnp.concatenate(blocks, axis=1)
        x2 = x + mod[:, 5 * D_MODEL:6 * D_MODEL] * _dot(a, wd_ref[...])
        return _rms(x2, fg_ref[...]) if last else x2

    for hf in range(CHAINS):
        r = _half_rows(hf)
        out_ref[0, r, :] = chain(x_ref[0, r, :], _lat_mod(mod_ref))

    if not last:
        @_on_first_step
        def _():
            outc_ref[0] = chain(xc_ref[0], _ctx_mod(modc_ref))


def _ffn(l, x1, x1c, mod, g, wg, wu, wd, fg, *, last):
    mod_lat, mod_ctx = _mod_specs(l)
    in_specs = [_lat_spec(D_MODEL), mod_lat, _stacked_vec(D_MODEL), _layer_resident(wg.shape[1:], l),
                _layer_resident(wu.shape[1:], l), _layer_resident(wd.shape[1:], l), _resident((1, D_MODEL))]
    args = [x1, mod, g, wg, wu, wd, fg]
    if not last:
        in_specs += [_ctx_spec(D_MODEL), mod_ctx]
        args += [x1c, mod]
    return pl.pallas_call(
        functools.partial(_ffn_kernel, last=last, layer=l),
        grid=(BATCH, NS_LAT),
        in_specs=in_specs,
        out_specs=[_lat_spec(D_MODEL)] + ([] if last else [_ctx_spec(D_MODEL)]),
        out_shape=[_lat_shape(D_MODEL, F32)] + ([] if last else [_ctx_shape(D_MODEL, F32)]),
        compiler_params=_params(2),
        name="ffn",
    )(*args)


def _rope_tables():
    pos = jnp.arange(SEQ)
    freqs = ROPE_BASE ** (-jnp.arange(0, ROPE_AXIS_DIM, 2, dtype=F32) / ROPE_AXIS_DIM)
    ang_r = (pos // GRID_W).astype(F32)[:, None] * freqs[None, :]
    ang_c = (pos % GRID_W).astype(F32)[:, None] * freqs[None, :]
    cos64 = jnp.concatenate([jnp.cos(ang_r)] * 2 + [jnp.cos(ang_c)] * 2, axis=-1)
    sin64 = jnp.concatenate([-jnp.sin(ang_r), jnp.sin(ang_r), -jnp.sin(ang_c), jnp.sin(ang_c)], axis=-1)
    return jnp.tile(cos64, (1, 2)), jnp.tile(sin64, (1, 2))


def _fourier_mats():
    cg, sg = _dft_cos_sin(FOURIER_GROUP_DIM)
    eye = np.eye(FOURIER_GROUPS, dtype=np.float32)
    mats = [np.kron(eye, cg), np.kron(eye, sg), *_seq_dft_mats(SEQ), *_seq_dft_mats(CTX_LEN)]
    return [jnp.asarray(m).astype(BF16) for m in mats]


def kernel(x, c, ctx, c_ctx, w_ada, b_ada, norm1_g, norm2_g, w_in, b_in, conv_w, conv_b, w_fourier_out,
           w_conv_out, w_attn_out, lambda_q1, lambda_k1, lambda_q2, lambda_k2, subln_g, w_out, w_ffn_gate,
           w_ffn_up, w_ffn_down, final_g):
    cc = jnp.concatenate([c, c_ctx[None, :], jnp.zeros((MOD_ROWS - BATCH - 1, D_MODEL), F32)], axis=0)
    mod = _ada(cc, w_ada, b_ada).reshape(DEPTH * MOD_ROWS, 6 * D_MODEL)
    cos_t, sin_t = _rope_tables()
    mats = _fourier_mats()
    weights_f32 = (w_in, w_fourier_out, w_conv_out, w_attn_out, w_out, w_ffn_gate, w_ffn_up, w_ffn_down)
    lamv = jnp.stack([lambda_q1, lambda_k1, lambda_q2, lambda_k2], axis=1).astype(F32)
    xl, xc = x, ctx

    for l in range(DEPTH):
        last = l == DEPTH - 1
        with_ctx = not last
        lambda_init = 0.8 - 0.6 * math.exp(-0.3 * l)
        (u, bg, z, q, k, v), ctx_p = _in_proj(l, xl, xc, mod, norm1_g, w_in if l == 0 else w_in_b, l, b_in, cos_t,
                                              sin_t, ctx_full=with_ctx)
        uc, bgc, zc, qc, kc, vc = ctx_p if with_ctx else (None, None, None, None) + tuple(ctx_p)
        cast = [a.reshape(-1, a.shape[-1]) for a in weights_f32] if l == 0 else []
        o = _attention(l, lamv, subln_g, q, k, v, kc, vc, qc, lambda_init=lambda_init, with_ctx=with_ctx,
                       cast=cast)
        if l == 0:
            n_o = len(o) - len(cast)
            w_in_b, w_fo_b, w_co_b, w_ao_b, w_out_b, w_g_b, w_u_b, w_d_b = (
                wb.reshape(a.shape) for wb, a in zip(o[n_o:], weights_f32))
            o = o[:n_o]
        yf = _fourier(u, uc, mats, with_ctx=with_ctx)
        ctx_parts = (xc, yf[1], bgc, zc, o[1]) if with_ctx else None
        x1 = _merge(l, xl, mod, norm1_g, yf[0], bg, z, o[0], ctx_parts, conv_w, conv_b, w_fo_b, w_co_b, w_ao_b,
                    w_in_b, b_in, w_out_b, with_ctx=with_ctx)
        out = _ffn(l, x1[0], x1[1] if with_ctx else None, mod, norm2_g, w_g_b, w_u_b, w_d_b, final_g.reshape(1, -1),
                   last=last)
        xl = out[0]
        if with_ctx:
            xc = out[1]
    return xl
```

```python
import functools
import math

import numpy as np
import jax
import jax.numpy as jnp
from jax import lax
from jax.experimental import pallas as pl
from jax.experimental.pallas import tpu as pltpu

D_MODEL = 1024
BATCH = 8
SEQ = 2048
DEPTH = 2
GRID_W = 64
CTX_LEN = 256
FOURIER_GROUPS = 4
FOURIER_GROUP_DIM = 64
FOURIER_WIDTH = FOURIER_GROUPS * FOURIER_GROUP_DIM
CONV_WIDTH = 256
ATTN_HEADS = 4
QK_DIM = 64
V_DIM = 2 * QK_DIM
ATTN_QK_WIDTH = ATTN_HEADS * 2 * QK_DIM
ATTN_V_WIDTH = ATTN_HEADS * V_DIM
ROPE_BASE = 10000.0
ROPE_AXIS_DIM = QK_DIM // 2
N_BRANCH = 3
D_FF = 2816
EPS = 1e-6

F_OFF = 0
CB_OFF = F_OFF + FOURIER_WIDTH
CC_OFF = CB_OFF + CONV_WIDTH
CX_OFF = CC_OFF + CONV_WIDTH
Q_OFF = CX_OFF + CONV_WIDTH
K_OFF = Q_OFF + ATTN_QK_WIDTH
V_OFF = K_OFF + ATTN_QK_WIDTH
G_OFF = V_OFF + ATTN_V_WIDTH
D_IN = G_OFF + N_BRANCH * D_MODEL

S_ALL = SEQ + CTX_LEN
TM = 256
CHAINS = 4
TS = CHAINS * TM
NS_LAT = SEQ // TS
NT_LAT = SEQ // TM
MOD_ROWS = 16
CTX_MOD_ROW = BATCH
ADA_TN = 1536
SUBLANES = 8
LANES = 128
BF16_SUBLANES = 16
V7X_VMEM_LIMIT = 56 * 1024 * 1024
Q_SCALE = (QK_DIM ** -0.5) * math.log2(math.e)

BF16 = jnp.bfloat16
F32 = jnp.float32

assert CTX_LEN == TM and BATCH == SUBLANES and CTX_MOD_ROW % SUBLANES == 0


def _dot(a, b):
    return jnp.dot(a, b, preferred_element_type=F32)


def _rms(x, g):
    return x * lax.rsqrt(jnp.mean(x * x, axis=-1, keepdims=True) + EPS) * g


def _norm_mod(x, g, mod, k):
    shift = mod[:, k * D_MODEL:(k + 1) * D_MODEL]
    scale = mod[:, (k + 1) * D_MODEL:(k + 2) * D_MODEL]
    return _rms(x, g) * (1.0 + scale) + shift


def _sigmoid(t):
    return 0.5 * jnp.tanh(0.5 * t) + 0.5


def _resident(shape):
    nd = len(shape)
    return pl.BlockSpec(shape, lambda *_: (0,) * nd, pipeline_mode=pl.Buffered(1))


def _layer_resident(shape, l):
    nd = len(shape)
    return pl.BlockSpec((None,) + tuple(shape), lambda *_: (l,) + (0,) * nd, pipeline_mode=pl.Buffered(1))


def _params(n_axes):
    return pltpu.CompilerParams(dimension_semantics=("arbitrary",) * n_axes,
                                vmem_limit_bytes=V7X_VMEM_LIMIT)


def _lat_spec(width):
    return pl.BlockSpec((1, TS, width), lambda b, i: (b, i, 0))


def _ctx_spec(width):
    return pl.BlockSpec((1, CTX_LEN, width), lambda b, i: (b, 0, 0))


def _mod_specs(l):
    per_layer = MOD_ROWS // SUBLANES
    return [pl.BlockSpec((SUBLANES, 6 * D_MODEL), lambda b, i: (l * per_layer, 0)),
            pl.BlockSpec((SUBLANES, 6 * D_MODEL), lambda b, i: (l * per_layer + CTX_MOD_ROW // SUBLANES, 0))]


def _lat_mod(mod_ref):
    return mod_ref[pl.ds(pl.program_id(0), 1), :]


def _ctx_mod(modc_ref):
    return modc_ref[0:1, :]


def _stacked_vec(n):
    return _resident((DEPTH, n))


def _half_rows(half):
    return slice(half * TM, (half + 1) * TM)


def _on_first_step(body):
    pl.when(pl.program_id(1) == 0)(body)


def _lat_shape(width, dtype):
    return jax.ShapeDtypeStruct((BATCH, SEQ, width), dtype)


def _ctx_shape(width, dtype):
    return jax.ShapeDtypeStruct((BATCH, CTX_LEN, width), dtype)


def _ada_kernel(cc_ref, w_ref, b_ref, o_ref):
    s = jax.nn.silu(cc_ref[...]).astype(BF16)
    o_ref[0] = _dot(s, w_ref[0].astype(BF16)) + b_ref[0]


def _ada(cc, w_ada, b_ada):
    n_col = 6 * D_MODEL
    return pl.pallas_call(
        _ada_kernel,
        grid=(DEPTH, n_col // ADA_TN),
        in_specs=[
            pl.BlockSpec((MOD_ROWS, D_MODEL), lambda l, j: (0, 0)),
            pl.BlockSpec((1, D_MODEL, ADA_TN), lambda l, j: (l, 0, j)),
            pl.BlockSpec((1, 1, ADA_TN), lambda l, j: (l, 0, j)),
        ],
        out_specs=pl.BlockSpec((1, MOD_ROWS, ADA_TN), lambda l, j: (l, 0, j)),
        out_shape=jax.ShapeDtypeStruct((DEPTH, MOD_ROWS, n_col), F32),
        compiler_params=_params(2),
        name="ada",
    )(cc, w_ada, b_ada.reshape(DEPTH, 1, n_col))


def _rope(t, cos, sin_signed):
    lane = lax.broadcasted_iota(jnp.int32, t.shape, 1)
    first = (lane % (ROPE_AXIS_DIM)) < (ROPE_AXIS_DIM // 2)
    half = ROPE_AXIS_DIM // 2
    partner = jnp.where(first, pltpu.roll(t, LANES - half, axis=1), pltpu.roll(t, half, axis=1))
    return t * cos + partner * sin_signed


def _in_proj_kernel(x_ref, xc_ref, mod_ref, modc_ref, g_ref, w_ref, b_ref, cos_ref, sin_ref, *out_refs,
                    ctx_full, cast_w, layer):
    if cast_w:
        w_f32_ref, w_ref, out_refs = w_ref, out_refs[-1], out_refs[:-1]

        @pl.when(jnp.logical_and(pl.program_id(0) == 0, pl.program_id(1) == 0))
        def _():
            w_ref[...] = w_f32_ref[...].astype(BF16)
    lat_out = out_refs[:6]
    ctx_out = out_refs[6:]

    def project(x, mod, r, rope_tables, outs, attn_only):
        h = _norm_mod(x, g_ref[layer:layer + 1, :], mod, 0).astype(BF16)

        def proj(lo, hi):
            return _dot(h, w_ref[:, lo:hi]) + b_ref[layer:layer + 1, lo:hi]

        if attn_only:
            k_ref, v_ref = outs
        else:
            u_ref, bg_ref, z_ref, q_ref, k_ref, v_ref = outs
            u_ref[0, r, :] = proj(F_OFF, CB_OFF).astype(BF16)
            bg_ref[0, r, :] = proj(CB_OFF, CC_OFF)
            z_ref[0, r, :] = proj(CC_OFF, CX_OFF) * proj(CX_OFF, Q_OFF)
            q = proj(Q_OFF, K_OFF)
        k = proj(K_OFF, V_OFF)
        if rope_tables is None:
            if not attn_only:
                q_ref[0, r, :] = (q * Q_SCALE).astype(BF16)
            k_ref[0, r, :] = k.astype(BF16)
        else:
            cos, sin = rope_tables
            for hd in range(ATTN_HEADS):
                sl = slice(hd * V_DIM, (hd + 1) * V_DIM)
                q_ref[0, r, sl] = (_rope(q[:, sl], cos, sin) * Q_SCALE).astype(BF16)
                k_ref[0, r, sl] = _rope(k[:, sl], cos, sin).astype(BF16)
        v_ref[0, r, :] = proj(V_OFF, G_OFF).astype(BF16)

    for hf in range(CHAINS):
        r = _half_rows(hf)
        project(x_ref[0, r, :], _lat_mod(mod_ref), r, (cos_ref[r, :], sin_ref[r, :]), lat_out, False)

    @_on_first_step
    def _():
        project(xc_ref[0], _ctx_mod(modc_ref), slice(0, CTX_LEN), None, ctx_out, not ctx_full)


def _in_proj(l, x, xc, mod, g, w, w_layer, b, cos_t, sin_t, *, ctx_full):
    widths = [(FOURIER_WIDTH, BF16), (CONV_WIDTH, F32), (CONV_WIDTH, F32), (ATTN_QK_WIDTH, BF16),
              (ATTN_QK_WIDTH, BF16), (ATTN_V_WIDTH, BF16)]
    ctx_widths = widths if ctx_full else widths[4:]
    cast_w = w.dtype != BF16
    outs = pl.pallas_call(
        functools.partial(_in_proj_kernel, ctx_full=ctx_full, cast_w=cast_w, layer=l),
        scratch_shapes=[pltpu.VMEM((D_MODEL, G_OFF), BF16)] if cast_w else [],
        grid=(BATCH, NS_LAT),
        in_specs=[_lat_spec(D_MODEL), _ctx_spec(D_MODEL), *_mod_specs(l), _stacked_vec(D_MODEL),
                  _layer_resident((D_MODEL, G_OFF), w_layer), _stacked_vec(D_IN),
                  pl.BlockSpec((TS, LANES), lambda bb, i: (i, 0)),
                  pl.BlockSpec((TS, LANES), lambda bb, i: (i, 0))],
        out_specs=[_lat_spec(wd) for wd, _ in widths] + [_ctx_spec(wd) for wd, _ in ctx_widths],
        out_shape=[_lat_shape(wd, dt) for wd, dt in widths] + [_ctx_shape(wd, dt) for wd, dt in ctx_widths],
        compiler_params=_params(2),
        name="in_proj",
    )(x, xc, mod, mod, g, w, b, cos_t, sin_t)
    return outs[:6], outs[6:]


VT_ROWS = V_DIM + BF16_SUBLANES
TK = TM
QK_LEAD = 3


def _attn_kernel(*refs, lambda_init, with_ctx, n_cast, layer):
    n_in = 8 if with_ctx else 7
    n_out = 2 if with_ctx else 1
    cast_in = refs[n_in:n_in + n_cast]
    cast_out = refs[n_in + n_cast + n_out:n_in + 2 * n_cast + n_out]
    if with_ctx:
        lamv_ref, sub_ref, q_ref, k_ref, v_ref, kc_ref, vc_ref, qc_ref = refs[:n_in]
        o_ref, oc_ref = refs[n_in + n_cast:n_in + n_cast + n_out]
    else:
        lamv_ref, sub_ref, q_ref, k_ref, v_ref, kc_ref, vc_ref = refs[:n_in]
        (o_ref,) = refs[n_in + n_cast:n_in + n_cast + n_out]
    vt_ref = refs[-1]
    for w_ref, wb_ref in zip(cast_in, cast_out):
        wb_ref[...] = w_ref[...].astype(BF16)
    vt_ref[0:V_DIM, 0:SEQ] = v_ref[0].astype(F32).T.astype(BF16)
    vt_ref[0:V_DIM, SEQ:S_ALL] = vc_ref[0].astype(F32).T.astype(BF16)
    r = lax.broadcasted_iota(jnp.int32, (BF16_SUBLANES, S_ALL), 0)
    vt_ref[V_DIM:VT_ROWS, :] = jnp.where(r == 0, 1.0, 0.0).astype(BF16)

    lv = lamv_ref[...]
    lam = (jnp.exp(jnp.sum(lv[0:1] * lv[1:2], axis=-1, keepdims=True))
           - jnp.exp(jnp.sum(lv[2:3] * lv[3:4], axis=-1, keepdims=True)) + lambda_init)
    ctx_chunks = ((SEQ, S_ALL),)
    all_chunks = tuple((lo, lo + TK) for lo in range(0, SEQ, TK)) + ctx_chunks

    def key_chunk(lo, hi):
        return k_ref[0, lo:hi, :] if hi <= SEQ else kc_ref[0, lo - SEQ:hi - SEQ, :]

    def query_tile(t):
        return q_ref[0, t * TM:(t + 1) * TM, :] if t < NT_LAT else qc_ref[0]

    def store_tile(t, val):
        if t < NT_LAT:
            o_ref[0, t * TM:(t + 1) * TM, :] = val
        else:
            oc_ref[0] = val

    def bcast8(c8):
        return jnp.broadcast_to(jnp.max(c8, axis=0, keepdims=True), (SUBLANES, 2 * TM))

    def start_tile(t):
        qt = query_tile(t).astype(F32).T
        row = lax.broadcasted_iota(jnp.int32, qt.shape, 0)
        zero = jnp.zeros_like(qt)
        qq = jnp.concatenate([jnp.where(row < QK_DIM, qt, zero), jnp.where(row >= QK_DIM, qt, zero)],
                             axis=1).astype(BF16)
        return dict(qq=qq, s={}, cmax=[], ref=None, acc=None)

    state = {}

    def emit_scores(t, idx, chunks):
        if idx == 0:
            state[t] = start_tile(t)
        d = state[t]
        lo, hi = chunks[idx]
        sq = _dot(key_chunk(lo, hi), d["qq"]).reshape((hi - lo) // SUBLANES, SUBLANES, 2 * TM)
        d["s"][idx] = sq
        d["cmax"].append(jnp.max(sq, axis=0))

    def emit_mix(t, idx, chunks):
        d = state[t]
        lo, hi = chunks[idx]
        s = d["s"].pop(idx)
        ref = d["ref"]
        new_ref = bcast8(d["cmax"][idx])
        if idx > 0:
            new_ref = jnp.maximum(ref, new_ref)
        e = jnp.exp2(s - new_ref[None]).reshape(hi - lo, 2 * TM).astype(BF16)
        o = _dot(vt_ref[:, lo:hi], e)
        if idx == 0:
            acc = o
        else:
            alpha = jnp.exp2(ref - new_ref)
            acc = (d["acc"].reshape(VT_ROWS // SUBLANES, SUBLANES, 2 * TM) * alpha[None]).reshape(VT_ROWS, 2 * TM) + o
        d["ref"], d["acc"] = new_ref, acc
        if idx == len(chunks) - 1:
            on = acc[0:V_DIM] * (1.0 / acc[V_DIM:V_DIM + 1])
            o_t = on[:, 0:TM] - lam * on[:, TM:2 * TM]
            store_tile(t, (_rms(o_t.T, sub_ref[layer:layer + 1, :]) * (1.0 - lambda_init)).astype(BF16))
            del state[t]

    tiles = [(t, all_chunks) for t in range(NT_LAT)] + ([(NT_LAT, ctx_chunks)] if with_ctx else [])
    items = [(t, idx, chunks) for t, chunks in tiles for idx in range(len(chunks))]
    for g in range(len(items) + QK_LEAD):
        if g < len(items):
            emit_scores(*items[g])
        if g >= QK_LEAD:
            emit_mix(*items[g - QK_LEAD])


def _attention(l, lamv, subln_g, q, k, v, kc, vc, qc, *, lambda_init, with_ctx, cast=()):
    lat = pl.BlockSpec((1, SEQ, V_DIM), lambda b, h: (b, 0, h))
    ctx = pl.BlockSpec((1, CTX_LEN, V_DIM), lambda b, h: (b, 0, h))
    n_steps = BATCH * ATTN_HEADS
    cast_specs = [pl.BlockSpec((w.shape[0] // n_steps, w.shape[1]), lambda b, h: (b * ATTN_HEADS + h, 0))
                  for w in cast]
    assert all(w.shape[0] % (n_steps * BF16_SUBLANES) == 0 for w in cast)
    return pl.pallas_call(
        functools.partial(_attn_kernel, lambda_init=lambda_init, with_ctx=with_ctx, n_cast=len(cast), layer=l),
        grid=(BATCH, ATTN_HEADS),
        in_specs=[_layer_resident((4, QK_DIM), l), _stacked_vec(V_DIM), lat, lat, lat, ctx, ctx]
                 + ([ctx] if with_ctx else []) + cast_specs,
        out_specs=([lat, ctx] if with_ctx else [lat]) + cast_specs,
        out_shape=[_lat_shape(ATTN_V_WIDTH, BF16)] + ([_ctx_shape(ATTN_V_WIDTH, BF16)] if with_ctx else [])
                  + [jax.ShapeDtypeStruct(w.shape, BF16) for w in cast],
        scratch_shapes=[pltpu.VMEM((VT_ROWS, S_ALL), BF16)],
        compiler_params=_params(2),
        name="attention",
    )(lamv, subln_g, q, k, v, kc, vc, *([qc] if with_ctx else []), *cast)


FOURIER_BATCHES = 2


def _dft_cos_sin(n):
    j = np.arange(n, dtype=np.int64)
    ang = 2.0 * np.pi * ((j[:, None] * j[None, :]) % n).astype(np.float64) / n
    return np.cos(ang).astype(np.float32), np.sin(ang).astype(np.float32)


def _seq_dft_mats(n):
    c, s = _dft_cos_sin(n)
    h = n // 2
    rev = np.zeros((h, h), np.float32)
    rev[np.arange(1, h), h - np.arange(1, h)] = 1.0
    nyq = np.zeros((BF16_SUBLANES, n), np.float32)
    nyq[0] = c[h]
    return [c[:h], s[:h], rev, nyq]


def _fourier_kernel(*refs, with_ctx):
    if with_ctx:
        u_ref, uc_ref, bdc_ref, bds_ref = refs[:4]
        lat_mats, ctx_mats, (y_ref, yc_ref) = refs[4:8], refs[8:12], refs[12:]
    else:
        u_ref, bdc_ref, bds_ref = refs[:3]
        lat_mats, (y_ref,) = refs[3:7], refs[7:]

    def mix(u, mats, out_ref, bb, n):
        c_ref, s_ref, rev_ref, nyq_ref = mats
        h = n // 2
        scale = 1.0 / math.sqrt(n * FOURIER_GROUP_DIM)
        tc = _dot(u, bdc_ref[...]).astype(BF16)
        ts = _dot(u, bds_ref[...]).astype(BF16)
        a = _dot(c_ref[...], tc)
        b = _dot(s_ref[...], ts)
        out_ref[bb, 0:h, :] = ((a - b) * scale).astype(BF16)
        upper = _dot(rev_ref[...], ((a + b) * scale).astype(BF16))
        nyq = _dot(nyq_ref[...], tc)[0:1] * scale
        row = lax.broadcasted_iota(jnp.int32, upper.shape, 0)
        out_ref[bb, h:n, :] = jnp.where(row == 0, nyq, upper).astype(BF16)

    for bb in range(FOURIER_BATCHES):
        mix(u_ref[bb], lat_mats, y_ref, bb, SEQ)
        if with_ctx:
            mix(uc_ref[bb], ctx_mats, yc_ref, bb, CTX_LEN)


def _fourier(u, uc, mats, *, with_ctx):
    if not with_ctx:
        mats = mats[:6]
    lat = pl.BlockSpec((FOURIER_BATCHES, SEQ, FOURIER_WIDTH), lambda b: (b, 0, 0))
    ctx = pl.BlockSpec((FOURIER_BATCHES, CTX_LEN, FOURIER_WIDTH), lambda b: (b, 0, 0))
    return pl.pallas_call(
        functools.partial(_fourier_kernel, with_ctx=with_ctx),
        grid=(BATCH // FOURIER_BATCHES,),
        in_specs=([lat, ctx] if with_ctx else [lat]) + [_resident(m.shape) for m in mats],
        out_specs=[lat, ctx] if with_ctx else [lat],
        out_shape=[_lat_shape(FOURIER_WIDTH, BF16)] + ([_ctx_shape(FOURIER_WIDTH, BF16)] if with_ctx else []),
        compiler_params=_params(1),
        name="fourier",
    )(u, *([uc] if with_ctx else []), *mats)


MERGE_NB = 256


def _merge_kernel(*refs, with_ctx, layer):
    (x_ref, mod_ref, g_ref, yf_ref, bg_ref, z_ref, zp_ref, zn_ref, o_ref, cw_ref, cb_ref, wfo_ref, wco_ref,
     wao_ref, win_ref, bin_ref, wout_ref) = refs[:17]
    if with_ctx:
        xc_ref, modc_ref, yfc_ref, bgc_ref, zc_ref, oc_ref, out_ref, outc_ref = refs[17:]
    else:
        (out_ref,) = refs[17:]
    i = pl.program_id(1)
    cw = cw_ref[...]

    def conv_in(z, bg, prev_row, next_row):
        n = z.shape[0]
        row = lax.broadcasted_iota(jnp.int32, z.shape, 0)
        z_prev = jnp.where(row == 0, prev_row, pltpu.roll(z, 1, axis=0))
        z_next = jnp.where(row == n - 1, next_row, pltpu.roll(z, n - 1, axis=0))
        conv = z_prev * cw[0:1] + z * cw[1:2] + z_next * cw[2:3] + cb_ref[layer:layer + 1, :]
        return (bg * conv).astype(BF16)

    def chain(x, mod, y_a_in, y_f_in, y_c_in):
        h = _norm_mod(x, g_ref[layer:layer + 1, :], mod, 0).astype(BF16)
        blocks = []
        for c in range(D_MODEL // MERGE_NB):
            cs = slice(c * MERGE_NB, (c + 1) * MERGE_NB)

            def gate(n):
                lo = G_OFF + n * D_MODEL + c * MERGE_NB
                return _sigmoid(_dot(h, win_ref[:, lo:lo + MERGE_NB]) + bin_ref[layer:layer + 1, lo:lo + MERGE_NB])

            y_a = _dot(y_a_in, wao_ref[:, cs])
            y_f = _dot(y_f_in, wfo_ref[:, cs])
            y_c = _dot(y_c_in, wco_ref[:, cs])
            blocks.append((gate(2) * y_a + gate(0) * y_f + gate(1) * y_c).astype(BF16))
        y = jnp.concatenate(blocks, axis=1)
        return x + mod[:, 2 * D_MODEL:3 * D_MODEL] * _dot(y, wout_ref[...])

    prev_row = jnp.where(i >= 1, zp_ref[0, SUBLANES - 1:SUBLANES, :], 0.0)
    next_row = jnp.where(i < NS_LAT - 1, zn_ref[0, 0:1, :], 0.0)
    yc_in = conv_in(z_ref[0], bg_ref[0], prev_row, next_row)
    for hf in range(CHAINS):
        r = _half_rows(hf)
        out_ref[0, r, :] = chain(x_ref[0, r, :], _lat_mod(mod_ref), o_ref[0, r, :], yf_ref[0, r, :], yc_in[r, :])

    if with_ctx:
        @_on_first_step
        def _():
            ycc_in = conv_in(zc_ref[0], bgc_ref[0], 0.0, 0.0)
            outc_ref[0] = chain(xc_ref[0], _ctx_mod(modc_ref), oc_ref[0], yfc_ref[0], ycc_in)


def _merge(l, x, mod, g, yf, bg, z, o, ctx_parts, cw, cb, wfo, wco, wao, w_in, b_in, wout, *, with_ctx):
    per8 = TS // SUBLANES
    halo_prev = pl.BlockSpec((1, SUBLANES, CONV_WIDTH),
                             lambda b, i: (b, jnp.maximum(i * per8 - 1, 0), 0))
    halo_next = pl.BlockSpec((1, SUBLANES, CONV_WIDTH),
                             lambda b, i: (b, jnp.minimum((i + 1) * per8, SEQ // SUBLANES - 1), 0))
    weights = [cw, cb, wfo, wco, wao, w_in, b_in, wout]
    weight_specs = [_stacked_vec(w.shape[1]) if w.ndim == 2 else _layer_resident(w.shape[1:], l) for w in weights]
    mod_lat, mod_ctx = _mod_specs(l)
    in_specs = ([_lat_spec(D_MODEL), mod_lat, _stacked_vec(D_MODEL), _lat_spec(FOURIER_WIDTH),
                 _lat_spec(CONV_WIDTH), _lat_spec(CONV_WIDTH), halo_prev, halo_next, _lat_spec(ATTN_V_WIDTH)]
                + weight_specs)
    args = [x, mod, g, yf, bg, z, z, z, o, *weights]
    if with_ctx:
        xc, yfc, bgc, zc, oc = ctx_parts
        in_specs += [_ctx_spec(D_MODEL), mod_ctx, _ctx_spec(FOURIER_WIDTH), _ctx_spec(CONV_WIDTH),
                     _ctx_spec(CONV_WIDTH), _ctx_spec(ATTN_V_WIDTH)]
        args += [xc, mod, yfc, bgc, zc, oc]
    return pl.pallas_call(
        functools.partial(_merge_kernel, with_ctx=with_ctx, layer=l),
        grid=(BATCH, NS_LAT),
        in_specs=in_specs,
        out_specs=[_lat_spec(D_MODEL)] + ([_ctx_spec(D_MODEL)] if with_ctx else []),
        out_shape=[_lat_shape(D_MODEL, F32)] + ([_ctx_shape(D_MODEL, F32)] if with_ctx else []),
        compiler_params=_params(2),
        name="merge",
    )(*args)


FFN_NB = 256


def _ffn_kernel(*refs, last, layer):
    x_ref, mod_ref, g_ref, wg_ref, wu_ref, wd_ref, fg_ref = refs[:7]
    if last:
        (out_ref,) = refs[7:]
    else:
        xc_ref, modc_ref, out_ref, outc_ref = refs[7:]

    def chain(x, mod):
        h = _norm_mod(x, g_ref[layer:layer + 1, :], mod, 3).astype(BF16)
        blocks = []
        for c in range(D_FF // FFN_NB):
            cs = slice(c * FFN_NB, (c + 1) * FFN_NB)
            gate = _dot(h, wg_ref[:, cs])
            blocks.append((gate * _sigmoid(gate) * _dot(h, wu_ref[:, cs])).astype(BF16))
        a = jnp.concatenate(blocks, axis=1)
        x2 = x + mod[:, 5 * D_MODEL:6 * D_MODEL] * _dot(a, wd_ref[...])
        return _rms(x2, fg_ref[...]) if last else x2

    for hf in range(CHAINS):
        r = _half_rows(hf)
        out_ref[0, r, :] = chain(x_ref[0, r, :], _lat_mod(mod_ref))

    if not last:
        @_on_first_step
        def _():
            outc_ref[0] = chain(xc_ref[0], _ctx_mod(modc_ref))


def _ffn(l, x1, x1c, mod, g, wg, wu, wd, fg, *, last):
    mod_lat, mod_ctx = _mod_specs(l)
    in_specs = [_lat_spec(D_MODEL), mod_lat, _stacked_vec(D_MODEL), _layer_resident(wg.shape[1:], l),
                _layer_resident(wu.shape[1:], l), _layer_resident(wd.shape[1:], l), _resident((1, D_MODEL))]
    args = [x1, mod, g, wg, wu, wd, fg]
    if not last:
        in_specs += [_ctx_spec(D_MODEL), mod_ctx]
        args += [x1c, mod]
    return pl.pallas_call(
        functools.partial(_ffn_kernel, last=last, layer=l),
        grid=(BATCH, NS_LAT),
        in_specs=in_specs,
        out_specs=[_lat_spec(D_MODEL)] + ([] if last else [_ctx_spec(D_MODEL)]),
        out_shape=[_lat_shape(D_MODEL, F32)] + ([] if last else [_ctx_shape(D_MODEL, F32)]),
        compiler_params=_params(2),
        name="ffn",
    )(*args)


def _rope_tables():
    pos = jnp.arange(SEQ)
    freqs = ROPE_BASE ** (-jnp.arange(0, ROPE_AXIS_DIM, 2, dtype=F32) / ROPE_AXIS_DIM)
    ang_r = (pos // GRID_W).astype(F32)[:, None] * freqs[None, :]
    ang_c = (pos % GRID_W).astype(F32)[:, None] * freqs[None, :]
    cos64 = jnp.concatenate([jnp.cos(ang_r)] * 2 + [jnp.cos(ang_c)] * 2, axis=-1)
    sin64 = jnp.concatenate([-jnp.sin(ang_r), jnp.sin(ang_r), -jnp.sin(ang_c), jnp.sin(ang_c)], axis=-1)
    return jnp.tile(cos64, (1, 2)), jnp.tile(sin64, (1, 2))


def _fourier_mats():
    cg, sg = _dft_cos_sin(FOURIER_GROUP_DIM)
    eye = np.eye(FOURIER_GROUPS, dtype=np.float32)
    mats = [np.kron(eye, cg), np.kron(eye, sg), *_seq_dft_mats(SEQ), *_seq_dft_mats(CTX_LEN)]
    return [jnp.asarray(m).astype(BF16) for m in mats]


def kernel(x, c, ctx, c_ctx, w_ada, b_ada, norm1_g, norm2_g, w_in, b_in, conv_w, conv_b, w_fourier_out,
           w_conv_out, w_attn_out, lambda_q1, lambda_k1, lambda_q2, lambda_k2, subln_g, w_out, w_ffn_gate,
           w_ffn_up, w_ffn_down, final_g):
    cc = jnp.concatenate([c, c_ctx[None, :], jnp.zeros((MOD_ROWS - BATCH - 1, D_MODEL), F32)], axis=0)
    mod = _ada(cc, w_ada, b_ada).reshape(DEPTH * MOD_ROWS, 6 * D_MODEL)
    cos_t, sin_t = _rope_tables()
    mats = _fourier_mats()
    weights_f32 = (w_in, w_fourier_out, w_conv_out, w_attn_out, w_out, w_ffn_gate, w_ffn_up, w_ffn_down)
    lamv = jnp.stack([lambda_q1, lambda_k1, lambda_q2, lambda_k2], axis=1).astype(F32)
    xl, xc = x, ctx

    for l in range(DEPTH):
        last = l == DEPTH - 1
        with_ctx = not last
        lambda_init = 0.8 - 0.6 * math.exp(-0.3 * l)
        (u, bg, z, q, k, v), ctx_p = _in_proj(l, xl, xc, mod, norm1_g, w_in if l == 0 else w_in_b, l, b_in, cos_t,
                                              sin_t, ctx_full=with_ctx)
        uc, bgc, zc, qc, kc, vc = ctx_p if with_ctx else (None, None, None, None) + tuple(ctx_p)
        cast = [a.reshape(-1, a.shape[-1]) for a in weights_f32] if l == 0 else []
        o = _attention(l, lamv, subln_g, q, k, v, kc, vc, qc, lambda_init=lambda_init, with_ctx=with_ctx,
                       cast=cast)
        if l == 0:
            n_o = len(o) - len(cast)
            w_in_b, w_fo_b, w_co_b, w_ao_b, w_out_b, w_g_b, w_u_b, w_d_b = (
                wb.reshape(a.shape) for wb, a in zip(o[n_o:], weights_f32))
            o = o[:n_o]
        yf = _fourier(u, uc, mats, with_ctx=with_ctx)
        ctx_parts = (xc, yf[1], bgc, zc, o[1]) if with_ctx else None
        x1 = _merge(l, xl, mod, norm1_g, yf[0], bg, z, o[0], ctx_parts, conv_w, conv_b, w_fo_b, w_co_b, w_ao_b,
                    w_in_b, b_in, w_out_b, with_ctx=with_ctx)
        out = _ffn(l, x1[0], x1[1] if with_ctx else None, mod, norm2_g, w_g_b, w_u_b, w_d_b, final_g.reshape(1, -1),
                   last=last)
        xl = out[0]
        if with_ctx:
            xc = out[1]
    return xl
```

```python
import functools
import math

import numpy as np
import jax
import jax.numpy as jnp
from jax import lax
from jax.experimental import pallas as pl
from jax.experimental.pallas import tpu as pltpu

D_MODEL = 1024
BATCH = 8
SEQ = 2048
DEPTH = 2
GRID_W = 64
CTX_LEN = 256
FOURIER_GROUPS = 4
FOURIER_GROUP_DIM = 64
FOURIER_WIDTH = FOURIER_GROUPS * FOURIER_GROUP_DIM
CONV_WIDTH = 256
ATTN_HEADS = 4
QK_DIM = 64
V_DIM = 2 * QK_DIM
ATTN_QK_WIDTH = ATTN_HEADS * 2 * QK_DIM
ATTN_V_WIDTH = ATTN_HEADS * V_DIM
ROPE_BASE = 10000.0
ROPE_AXIS_DIM = QK_DIM // 2
N_BRANCH = 3
D_FF = 2816
EPS = 1e-6

F_OFF = 0
CB_OFF = F_OFF + FOURIER_WIDTH
CC_OFF = CB_OFF + CONV_WIDTH
CX_OFF = CC_OFF + CONV_WIDTH
Q_OFF = CX_OFF + CONV_WIDTH
K_OFF = Q_OFF + ATTN_QK_WIDTH
V_OFF = K_OFF + ATTN_QK_WIDTH
G_OFF = V_OFF + ATTN_V_WIDTH
D_IN = G_OFF + N_BRANCH * D_MODEL

S_ALL = SEQ + CTX_LEN
TM = 256
CHAINS = 4
TS = CHAINS * TM
NS_LAT = SEQ // TS
NT_LAT = SEQ // TM
MOD_ROWS = 16
CTX_MOD_ROW = BATCH
ADA_TN = 1536
SUBLANES = 8
LANES = 128
BF16_SUBLANES = 16
V7X_VMEM_LIMIT = 56 * 1024 * 1024
Q_SCALE = (QK_DIM ** -0.5) * math.log2(math.e)

BF16 = jnp.bfloat16
F32 = jnp.float32

assert CTX_LEN == TM and BATCH == SUBLANES and CTX_MOD_ROW % SUBLANES == 0


def _dot(a, b):
    return jnp.dot(a, b, preferred_element_type=F32)


def _rms(x, g):
    return x * lax.rsqrt(jnp.mean(x * x, axis=-1, keepdims=True) + EPS) * g


def _norm_mod(x, g, mod, k):
    shift = mod[:, k * D_MODEL:(k + 1) * D_MODEL]
    scale = mod[:, (k + 1) * D_MODEL:(k + 2) * D_MODEL]
    return _rms(x, g) * (1.0 + scale) + shift


def _sigmoid(t):
    return 0.5 * jnp.tanh(0.5 * t) + 0.5


def _resident(shape):
    nd = len(shape)
    return pl.BlockSpec(shape, lambda *_: (0,) * nd, pipeline_mode=pl.Buffered(1))


def _layer_resident(shape, l):
    nd = len(shape)
    return pl.BlockSpec((None,) + tuple(shape), lambda *_: (l,) + (0,) * nd, pipeline_mode=pl.Buffered(1))


def _params(n_axes):
    return pltpu.CompilerParams(dimension_semantics=("arbitrary",) * n_axes,
                                vmem_limit_bytes=V7X_VMEM_LIMIT)


def _lat_spec(width):
    return pl.BlockSpec((1, TS, width), lambda b, i: (b, i, 0))


def _ctx_spec(width):
    return pl.BlockSpec((1, CTX_LEN, width), lambda b, i: (b, 0, 0))


def _mod_specs(l):
    per_layer = MOD_ROWS // SUBLANES
    return [pl.BlockSpec((SUBLANES, 6 * D_MODEL), lambda b, i: (l * per_layer, 0)),
            pl.BlockSpec((SUBLANES, 6 * D_MODEL), lambda b, i: (l * per_layer + CTX_MOD_ROW // SUBLANES, 0))]


def _lat_mod(mod_ref):
    return mod_ref[pl.ds(pl.program_id(0), 1), :]


def _ctx_mod(modc_ref):
    return modc_ref[0:1, :]


def _stacked_vec(n):
    return _resident((DEPTH, n))


def _half_rows(half):
    return slice(half * TM, (half + 1) * TM)


def _on_first_step(body):
    pl.when(pl.program_id(1) == 0)(body)


def _lat_shape(width, dtype):
    return jax.ShapeDtypeStruct((BATCH, SEQ, width), dtype)


def _ctx_shape(width, dtype):
    return jax.ShapeDtypeStruct((BATCH, CTX_LEN, width), dtype)


def _ada_kernel(cc_ref, w_ref, b_ref, o_ref):
    s = jax.nn.silu(cc_ref[...]).astype(BF16)
    o_ref[0] = _dot(s, w_ref[0].astype(BF16)) + b_ref[0]


def _ada(cc, w_ada, b_ada):
    n_col = 6 * D_MODEL
    return pl.pallas_call(
        _ada_kernel,
        grid=(DEPTH, n_col // ADA_TN),
        in_specs=[
            pl.BlockSpec((MOD_ROWS, D_MODEL), lambda l, j: (0, 0)),
            pl.BlockSpec((1, D_MODEL, ADA_TN), lambda l, j: (l, 0, j)),
            pl.BlockSpec((1, 1, ADA_TN), lambda l, j: (l, 0, j)),
        ],
        out_specs=pl.BlockSpec((1, MOD_ROWS, ADA_TN), lambda l, j: (l, 0, j)),
        out_shape=jax.ShapeDtypeStruct((DEPTH, MOD_ROWS, n_col), F32),
        compiler_params=_params(2),
        name="ada",
    )(cc, w_ada, b_ada.reshape(DEPTH, 1, n_col))


def _rope(t, cos, sin_signed):
    lane = lax.broadcasted_iota(jnp.int32, t.shape, 1)
    first = (lane % (ROPE_AXIS_DIM)) < (ROPE_AXIS_DIM // 2)
    half = ROPE_AXIS_DIM // 2
    partner = jnp.where(first, pltpu.roll(t, LANES - half, axis=1), pltpu.roll(t, half, axis=1))
    return t * cos + partner * sin_signed


def _in_proj_kernel(x_ref, xc_ref, mod_ref, modc_ref, g_ref, w_ref, b_ref, cos_ref, sin_ref, *out_refs,
                    ctx_full, cast_w, layer):
    if cast_w:
        w_f32_ref, w_ref, out_refs = w_ref, out_refs[-1], out_refs[:-1]

        @pl.when(jnp.logical_and(pl.program_id(0) == 0, pl.program_id(1) == 0))
        def _():
            w_ref[...] = w_f32_ref[...].astype(BF16)
    lat_out = out_refs[:6]
    ctx_out = out_refs[6:]

    def project(x, mod, r, rope_tables, outs, attn_only):
        h = _norm_mod(x, g_ref[layer:layer + 1, :], mod, 0).astype(BF16)

        def proj(lo, hi):
            return _dot(h, w_ref[:, lo:hi]) + b_ref[layer:layer + 1, lo:hi]

        if attn_only:
            k_ref, v_ref = outs
        else:
            u_ref, bg_ref, z_ref, q_ref, k_ref, v_ref = outs
            u_ref[0, r, :] = proj(F_OFF, CB_OFF).astype(BF16)
            bg_ref[0, r, :] = proj(CB_OFF, CC_OFF).astype(BF16)
            z_ref[0, r, :] = (proj(CC_OFF, CX_OFF) * proj(CX_OFF, Q_OFF)).astype(BF16)
            q = proj(Q_OFF, K_OFF)
        k = proj(K_OFF, V_OFF)
        if rope_tables is None:
            if not attn_only:
                q_ref[0, r, :] = (q * Q_SCALE).astype(BF16)
            k_ref[0, r, :] = k.astype(BF16)
        else:
            cos, sin = rope_tables
            for hd in range(ATTN_HEADS):
                sl = slice(hd * V_DIM, (hd + 1) * V_DIM)
                q_ref[0, r, sl] = (_rope(q[:, sl], cos, sin) * Q_SCALE).astype(BF16)
                k_ref[0, r, sl] = _rope(k[:, sl], cos, sin).astype(BF16)
        v_ref[0, r, :] = proj(V_OFF, G_OFF).astype(BF16)

    for hf in range(CHAINS):
        r = _half_rows(hf)
        project(x_ref[0, r, :], _lat_mod(mod_ref), r, (cos_ref[r, :], sin_ref[r, :]), lat_out, False)

    @_on_first_step
    def _():
        project(xc_ref[0], _ctx_mod(modc_ref), slice(0, CTX_LEN), None, ctx_out, not ctx_full)


def _in_proj(l, x, xc, mod, g, w, w_layer, b, cos_t, sin_t, *, ctx_full):
    widths = [(FOURIER_WIDTH, BF16), (CONV_WIDTH, BF16), (CONV_WIDTH, BF16), (ATTN_QK_WIDTH, BF16),
              (ATTN_QK_WIDTH, BF16), (ATTN_V_WIDTH, BF16)]
    ctx_widths = widths if ctx_full else widths[4:]
    cast_w = w.dtype != BF16
    outs = pl.pallas_call(
        functools.partial(_in_proj_kernel, ctx_full=ctx_full, cast_w=cast_w, layer=l),
        scratch_shapes=[pltpu.VMEM((D_MODEL, G_OFF), BF16)] if cast_w else [],
        grid=(BATCH, NS_LAT),
        in_specs=[_lat_spec(D_MODEL), _ctx_spec(D_MODEL), *_mod_specs(l), _stacked_vec(D_MODEL),
                  _layer_resident((D_MODEL, G_OFF), w_layer), _stacked_vec(D_IN),
                  pl.BlockSpec((TS, LANES), lambda bb, i: (i, 0)),
                  pl.BlockSpec((TS, LANES), lambda bb, i: (i, 0))],
        out_specs=[_lat_spec(wd) for wd, _ in widths] + [_ctx_spec(wd) for wd, _ in ctx_widths],
        out_shape=[_lat_shape(wd, dt) for wd, dt in widths] + [_ctx_shape(wd, dt) for wd, dt in ctx_widths],
        compiler_params=_params(2),
        name="in_proj",
    )(x, xc, mod, mod, g, w, b, cos_t, sin_t)
    return outs[:6], outs[6:]


VT_ROWS = V_DIM + BF16_SUBLANES
TK = TM
QK_LEAD = 2


def _attn_kernel(*refs, lambda_init, with_ctx, n_cast, layer):
    n_in = 8 if with_ctx else 7
    n_out = 2 if with_ctx else 1
    cast_in = refs[n_in:n_in + n_cast]
    cast_out = refs[n_in + n_cast + n_out:n_in + 2 * n_cast + n_out]
    if with_ctx:
        lamv_ref, sub_ref, q_ref, k_ref, v_ref, kc_ref, vc_ref, qc_ref = refs[:n_in]
        o_ref, oc_ref = refs[n_in + n_cast:n_in + n_cast + n_out]
    else:
        lamv_ref, sub_ref, q_ref, k_ref, v_ref, kc_ref, vc_ref = refs[:n_in]
        (o_ref,) = refs[n_in + n_cast:n_in + n_cast + n_out]
    vt_ref = refs[-1]
    for w_ref, wb_ref in zip(cast_in, cast_out):
        wb_ref[...] = w_ref[...].astype(BF16)
    vt_ref[0:V_DIM, 0:SEQ] = v_ref[0].astype(F32).T.astype(BF16)
    vt_ref[0:V_DIM, SEQ:S_ALL] = vc_ref[0].astype(F32).T.astype(BF16)
    r = lax.broadcasted_iota(jnp.int32, (BF16_SUBLANES, S_ALL), 0)
    vt_ref[V_DIM:VT_ROWS, :] = jnp.where(r == 0, 1.0, 0.0).astype(BF16)

    lv = lamv_ref[...]
    lam = (jnp.exp(jnp.sum(lv[0:1] * lv[1:2], axis=-1, keepdims=True))
           - jnp.exp(jnp.sum(lv[2:3] * lv[3:4], axis=-1, keepdims=True)) + lambda_init)
    ctx_chunks = ((SEQ, S_ALL),)
    all_chunks = tuple((lo, lo + TK) for lo in range(0, SEQ, TK)) + ctx_chunks

    def key_chunk(lo, hi):
        return k_ref[0, lo:hi, :] if hi <= SEQ else kc_ref[0, lo - SEQ:hi - SEQ, :]

    def query_tile(t):
        return q_ref[0, t * TM:(t + 1) * TM, :] if t < NT_LAT else qc_ref[0]

    def store_tile(t, val):
        if t < NT_LAT:
            o_ref[0, t * TM:(t + 1) * TM, :] = val
        else:
            oc_ref[0] = val

    def bcast8(c8):
        return jnp.broadcast_to(jnp.max(c8, axis=0, keepdims=True), (SUBLANES, 2 * TM))

    def start_tile(t):
        qt = query_tile(t).astype(F32).T
        row = lax.broadcasted_iota(jnp.int32, qt.shape, 0)
        zero = jnp.zeros_like(qt)
        qq = jnp.concatenate([jnp.where(row < QK_DIM, qt, zero), jnp.where(row >= QK_DIM, qt, zero)],
                             axis=1).astype(BF16)
        return dict(qq=qq, s={}, cmax=[], ref=None, acc=None)

    state = {}

    def emit_scores(t, idx, chunks):
        if idx == 0:
            state[t] = start_tile(t)
        d = state[t]
        lo, hi = chunks[idx]
        sq = _dot(key_chunk(lo, hi), d["qq"]).reshape((hi - lo) // SUBLANES, SUBLANES, 2 * TM)
        d["s"][idx] = sq
        d["cmax"].append(jnp.max(sq, axis=0))

    def emit_mix(t, idx, chunks):
        d = state[t]
        lo, hi = chunks[idx]
        s = d["s"].pop(idx)
        ref = d["ref"]
        new_ref = bcast8(d["cmax"][idx])
        if idx > 0:
            new_ref = jnp.maximum(ref, new_ref)
        e = jnp.exp2(s - new_ref[None]).reshape(hi - lo, 2 * TM).astype(BF16)
        o = _dot(vt_ref[:, lo:hi], e)
        if idx == 0:
            acc = o
        else:
            alpha = jnp.exp2(ref - new_ref)
            acc = (d["acc"].reshape(VT_ROWS // SUBLANES, SUBLANES, 2 * TM) * alpha[None]).reshape(VT_ROWS, 2 * TM) + o
        d["ref"], d["acc"] = new_ref, acc
        if idx == len(chunks) - 1:
            on = acc[0:V_DIM] * (1.0 / acc[V_DIM:V_DIM + 1])
            o_t = on[:, 0:TM] - lam * on[:, TM:2 * TM]
            store_tile(t, (_rms(o_t.T, sub_ref[layer:layer + 1, :]) * (1.0 - lambda_init)).astype(BF16))
            del state[t]

    tiles = [(t, all_chunks) for t in range(NT_LAT)] + ([(NT_LAT, ctx_chunks)] if with_ctx else [])
    items = [(t, idx, chunks) for t, chunks in tiles for idx in range(len(chunks))]
    for g in range(len(items) + QK_LEAD):
        if g < len(items):
            emit_scores(*items[g])
        if g >= QK_LEAD:
            emit_mix(*items[g - QK_LEAD])


def _attention(l, lamv, subln_g, q, k, v, kc, vc, qc, *, lambda_init, with_ctx, cast=()):
    lat = pl.BlockSpec((1, SEQ, V_DIM), lambda b, h: (b, 0, h))
    ctx = pl.BlockSpec((1, CTX_LEN, V_DIM), lambda b, h: (b, 0, h))
    n_steps = BATCH * ATTN_HEADS
    cast_specs = [pl.BlockSpec((w.shape[0] // n_steps, w.shape[1]), lambda b, h: (b * ATTN_HEADS + h, 0))
                  for w in cast]
    assert all(w.shape[0] % (n_steps * BF16_SUBLANES) == 0 for w in cast)
    return pl.pallas_call(
        functools.partial(_attn_kernel, lambda_init=lambda_init, with_ctx=with_ctx, n_cast=len(cast), layer=l),
        grid=(BATCH, ATTN_HEADS),
        in_specs=[_layer_resident((4, QK_DIM), l), _stacked_vec(V_DIM), lat, lat, lat, ctx, ctx]
                 + ([ctx] if with_ctx else []) + cast_specs,
        out_specs=([lat, ctx] if with_ctx else [lat]) + cast_specs,
        out_shape=[_lat_shape(ATTN_V_WIDTH, BF16)] + ([_ctx_shape(ATTN_V_WIDTH, BF16)] if with_ctx else [])
                  + [jax.ShapeDtypeStruct(w.shape, BF16) for w in cast],
        scratch_shapes=[pltpu.VMEM((VT_ROWS, S_ALL), BF16)],
        compiler_params=_params(2),
        name="attention",
    )(lamv, subln_g, q, k, v, kc, vc, *([qc] if with_ctx else []), *cast)


FOURIER_BATCHES = 2


def _dft_cos_sin(n):
    j = np.arange(n, dtype=np.int64)
    ang = 2.0 * np.pi * ((j[:, None] * j[None, :]) % n).astype(np.float64) / n
    return np.cos(ang).astype(np.float32), np.sin(ang).astype(np.float32)


def _seq_dft_mats(n):
    c, s = _dft_cos_sin(n)
    h = n // 2
    rev = np.zeros((h, h), np.float32)
    rev[np.arange(1, h), h - np.arange(1, h)] = 1.0
    nyq = np.zeros((BF16_SUBLANES, n), np.float32)
    nyq[0] = c[h]
    return [c[:h], s[:h], rev, nyq]


def _fourier_kernel(*refs, with_ctx):
    if with_ctx:
        u_ref, uc_ref, bdc_ref, bds_ref = refs[:4]
        lat_mats, ctx_mats, (y_ref, yc_ref) = refs[4:8], refs[8:12], refs[12:]
    else:
        u_ref, bdc_ref, bds_ref = refs[:3]
        lat_mats, (y_ref,) = refs[3:7], refs[7:]

    def mix(u, mats, out_ref, bb, n):
        c_ref, s_ref, rev_ref, nyq_ref = mats
        h = n // 2
        scale = 1.0 / math.sqrt(n * FOURIER_GROUP_DIM)
        tc = _dot(u, bdc_ref[...]).astype(BF16)
        ts = _dot(u, bds_ref[...]).astype(BF16)
        a = _dot(c_ref[...], tc)
        b = _dot(s_ref[...], ts)
        out_ref[bb, 0:h, :] = ((a - b) * scale).astype(BF16)
        upper = _dot(rev_ref[...], ((a + b) * scale).astype(BF16))
        nyq = _dot(nyq_ref[...], tc)[0:1] * scale
        row = lax.broadcasted_iota(jnp.int32, upper.shape, 0)
        out_ref[bb, h:n, :] = jnp.where(row == 0, nyq, upper).astype(BF16)

    for bb in range(FOURIER_BATCHES):
        mix(u_ref[bb], lat_mats, y_ref, bb, SEQ)
        if with_ctx:
            mix(uc_ref[bb], ctx_mats, yc_ref, bb, CTX_LEN)


def _fourier(u, uc, mats, *, with_ctx):
    if not with_ctx:
        mats = mats[:6]
    lat = pl.BlockSpec((FOURIER_BATCHES, SEQ, FOURIER_WIDTH), lambda b: (b, 0, 0))
    ctx = pl.BlockSpec((FOURIER_BATCHES, CTX_LEN, FOURIER_WIDTH), lambda b: (b, 0, 0))
    return pl.pallas_call(
        functools.partial(_fourier_kernel, with_ctx=with_ctx),
        grid=(BATCH // FOURIER_BATCHES,),
        in_specs=([lat, ctx] if with_ctx else [lat]) + [_resident(m.shape) for m in mats],
        out_specs=[lat, ctx] if with_ctx else [lat],
        out_shape=[_lat_shape(FOURIER_WIDTH, BF16)] + ([_ctx_shape(FOURIER_WIDTH, BF16)] if with_ctx else []),
        compiler_params=_params(1),
        name="fourier",
    )(u, *([uc] if with_ctx else []), *mats)


MERGE_NB = 256


def _merge_kernel(*refs, with_ctx, layer):
    (x_ref, mod_ref, g_ref, yf_ref, bg_ref, z_ref, zp_ref, zn_ref, o_ref, cw_ref, cb_ref, wfo_ref, wco_ref,
     wao_ref, win_ref, bin_ref, wout_ref) = refs[:17]
    if with_ctx:
        xc_ref, modc_ref, yfc_ref, bgc_ref, zc_ref, oc_ref, out_ref, outc_ref = refs[17:]
    else:
        (out_ref,) = refs[17:]
    i = pl.program_id(1)
    cw = cw_ref[...]

    def conv_in(z, bg, prev_row, next_row):
        n = z.shape[0]
        row = lax.broadcasted_iota(jnp.int32, z.shape, 0)
        z_prev = jnp.where(row == 0, prev_row, pltpu.roll(z, 1, axis=0))
        z_next = jnp.where(row == n - 1, next_row, pltpu.roll(z, n - 1, axis=0))
        conv = z_prev * cw[0:1] + z * cw[1:2] + z_next * cw[2:3] + cb_ref[layer:layer + 1, :]
        return (bg * conv).astype(BF16)

    def chain(x, mod, y_a_in, y_f_in, y_c_in):
        h = _norm_mod(x, g_ref[layer:layer + 1, :], mod, 0).astype(BF16)
        blocks = []
        for c in range(D_MODEL // MERGE_NB):
            cs = slice(c * MERGE_NB, (c + 1) * MERGE_NB)

            def gate(n):
                lo = G_OFF + n * D_MODEL + c * MERGE_NB
                return _sigmoid(_dot(h, win_ref[:, lo:lo + MERGE_NB]) + bin_ref[layer:layer + 1, lo:lo + MERGE_NB])

            y_a = _dot(y_a_in, wao_ref[:, cs])
            y_f = _dot(y_f_in, wfo_ref[:, cs])
            y_c = _dot(y_c_in, wco_ref[:, cs])
            blocks.append((gate(2) * y_a + gate(0) * y_f + gate(1) * y_c).astype(BF16))
        y = jnp.concatenate(blocks, axis=1)
        return x + mod[:, 2 * D_MODEL:3 * D_MODEL] * _dot(y, wout_ref[...])

    prev_row = jnp.where(i >= 1, zp_ref[0, BF16_SUBLANES - 1:BF16_SUBLANES, :].astype(F32), 0.0)
    next_row = jnp.where(i < NS_LAT - 1, zn_ref[0, 0:1, :].astype(F32), 0.0)
    yc_in = conv_in(z_ref[0].astype(F32), bg_ref[0].astype(F32), prev_row, next_row)
    for hf in range(CHAINS):
        r = _half_rows(hf)
        out_ref[0, r, :] = chain(x_ref[0, r, :], _lat_mod(mod_ref), o_ref[0, r, :], yf_ref[0, r, :], yc_in[r, :])

    if with_ctx:
        @_on_first_step
        def _():
            ycc_in = conv_in(zc_ref[0].astype(F32), bgc_ref[0].astype(F32), 0.0, 0.0)
            outc_ref[0] = chain(xc_ref[0], _ctx_mod(modc_ref), oc_ref[0], yfc_ref[0], ycc_in)


def _merge(l, x, mod, g, yf, bg, z, o, ctx_parts, cw, cb, wfo, wco, wao, w_in, b_in, wout, *, with_ctx):
    per8 = TS // BF16_SUBLANES
    halo_prev = pl.BlockSpec((1, BF16_SUBLANES, CONV_WIDTH),
                             lambda b, i: (b, jnp.maximum(i * per8 - 1, 0), 0))
    halo_next = pl.BlockSpec((1, BF16_SUBLANES, CONV_WIDTH),
                             lambda b, i: (b, jnp.minimum((i + 1) * per8, SEQ // BF16_SUBLANES - 1), 0))
    weights = [cw, cb, wfo, wco, wao, w_in, b_in, wout]
    weight_specs = [_stacked_vec(w.shape[1]) if w.ndim == 2 else _layer_resident(w.shape[1:], l) for w in weights]
    mod_lat, mod_ctx = _mod_specs(l)
    in_specs = ([_lat_spec(D_MODEL), mod_lat, _stacked_vec(D_MODEL), _lat_spec(FOURIER_WIDTH),
                 _lat_spec(CONV_WIDTH), _lat_spec(CONV_WIDTH), halo_prev, halo_next, _lat_spec(ATTN_V_WIDTH)]
                + weight_specs)
    args = [x, mod, g, yf, bg, z, z, z, o, *weights]
    if with_ctx:
        xc, yfc, bgc, zc, oc = ctx_parts
        in_specs += [_ctx_spec(D_MODEL), mod_ctx, _ctx_spec(FOURIER_WIDTH), _ctx_spec(CONV_WIDTH),
                     _ctx_spec(CONV_WIDTH), _ctx_spec(ATTN_V_WIDTH)]
        args += [xc, mod, yfc, bgc, zc, oc]
    return pl.pallas_call(
        functools.partial(_merge_kernel, with_ctx=with_ctx, layer=l),
        grid=(BATCH, NS_LAT),
        in_specs=in_specs,
        out_specs=[_lat_spec(D_MODEL)] + ([_ctx_spec(D_MODEL)] if with_ctx else []),
        out_shape=[_lat_shape(D_MODEL, F32)] + ([_ctx_shape(D_MODEL, F32)] if with_ctx else []),
        compiler_params=_params(2),
        name="merge",
    )(*args)


FFN_NB = 256


def _ffn_kernel(*refs, last, layer):
    x_ref, mod_ref, g_ref, wg_ref, wu_ref, wd_ref, fg_ref = refs[:7]
    if last:
        (out_ref,) = refs[7:]
    else:
        xc_ref, modc_ref, out_ref, outc_ref = refs[7:]

    def chain(x, mod):
        h = _norm_mod(x, g_ref[layer:layer + 1, :], mod, 3).astype(BF16)
        blocks = []
        for c in range(D_FF // FFN_NB):
            cs = slice(c * FFN_NB, (c + 1) * FFN_NB)
            gate = _dot(h, wg_ref[:, cs])
            blocks.append((gate * _sigmoid(gate) * _dot(h, wu_ref[:, cs])).astype(BF16))
        a = jnp.concatenate(blocks, axis=1)
        x2 = x + mod[:, 5 * D_MODEL:6 * D_MODEL] * _dot(a, wd_ref[...])
        return _rms(x2, fg_ref[...]) if last else x2

    for hf in range(CHAINS):
        r = _half_rows(hf)
        out_ref[0, r, :] = chain(x_ref[0, r, :], _lat_mod(mod_ref))

    if not last:
        @_on_first_step
        def _():
            outc_ref[0] = chain(xc_ref[0], _ctx_mod(modc_ref))


def _ffn(l, x1, x1c, mod, g, wg, wu, wd, fg, *, last):
    mod_lat, mod_ctx = _mod_specs(l)
    in_specs = [_lat_spec(D_MODEL), mod_lat, _stacked_vec(D_MODEL), _layer_resident(wg.shape[1:], l),
                _layer_resident(wu.shape[1:], l), _layer_resident(wd.shape[1:], l), _resident((1, D_MODEL))]
    args = [x1, mod, g, wg, wu, wd, fg]
    if not last:
        in_specs += [_ctx_spec(D_MODEL), mod_ctx]
        args += [x1c, mod]
    return pl.pallas_call(
        functools.partial(_ffn_kernel, last=last, layer=l),
        grid=(BATCH, NS_LAT),
        in_specs=in_specs,
        out_specs=[_lat_spec(D_MODEL)] + ([] if last else [_ctx_spec(D_MODEL)]),
        out_shape=[_lat_shape(D_MODEL, F32)] + ([] if last else [_ctx_shape(D_MODEL, F32)]),
        compiler_params=_params(2),
        name="ffn",
    )(*args)


def _rope_tables():
    pos = jnp.arange(SEQ)
    freqs = ROPE_BASE ** (-jnp.arange(0, ROPE_AXIS_DIM, 2, dtype=F32) / ROPE_AXIS_DIM)
    ang_r = (pos // GRID_W).astype(F32)[:, None] * freqs[None, :]
    ang_c = (pos % GRID_W).astype(F32)[:, None] * freqs[None, :]
    cos64 = jnp.concatenate([jnp.cos(ang_r)] * 2 + [jnp.cos(ang_c)] * 2, axis=-1)
    sin64 = jnp.concatenate([-jnp.sin(ang_r), jnp.sin(ang_r), -jnp.sin(ang_c), jnp.sin(ang_c)], axis=-1)
    return jnp.tile(cos64, (1, 2)), jnp.tile(sin64, (1, 2))


def _fourier_mats():
    cg, sg = _dft_cos_sin(FOURIER_GROUP_DIM)
    eye = np.eye(FOURIER_GROUPS, dtype=np.float32)
    mats = [np.kron(eye, cg), np.kron(eye, sg), *_seq_dft_mats(SEQ), *_seq_dft_mats(CTX_LEN)]
    return [jnp.asarray(m).astype(BF16) for m in mats]


def kernel(x, c, ctx, c_ctx, w_ada, b_ada, norm1_g, norm2_g, w_in, b_in, conv_w, conv_b, w_fourier_out,
           w_conv_out, w_attn_out, lambda_q1, lambda_k1, lambda_q2, lambda_k2, subln_g, w_out, w_ffn_gate,
           w_ffn_up, w_ffn_down, final_g):
    cc = jnp.concatenate([c, c_ctx[None, :], jnp.zeros((MOD_ROWS - BATCH - 1, D_MODEL), F32)], axis=0)
    mod = _ada(cc, w_ada, b_ada).reshape(DEPTH * MOD_ROWS, 6 * D_MODEL)
    cos_t, sin_t = _rope_tables()
    mats = _fourier_mats()
    weights_f32 = (w_in, w_fourier_out, w_conv_out, w_attn_out, w_out, w_ffn_gate, w_ffn_up, w_ffn_down)
    lamv = jnp.stack([lambda_q1, lambda_k1, lambda_q2, lambda_k2], axis=1).astype(F32)
    xl, xc = x, ctx

    for l in range(DEPTH):
        last = l == DEPTH - 1
        with_ctx = not last
        lambda_init = 0.8 - 0.6 * math.exp(-0.3 * l)
        (u, bg, z, q, k, v), ctx_p = _in_proj(l, xl, xc, mod, norm1_g, w_in if l == 0 else w_in_b, l, b_in, cos_t,
                                              sin_t, ctx_full=with_ctx)
        uc, bgc, zc, qc, kc, vc = ctx_p if with_ctx else (None, None, None, None) + tuple(ctx_p)
        cast = [a.reshape(-1, a.shape[-1]) for a in weights_f32] if l == 0 else []
        o = _attention(l, lamv, subln_g, q, k, v, kc, vc, qc, lambda_init=lambda_init, with_ctx=with_ctx,
                       cast=cast)
        if l == 0:
            n_o = len(o) - len(cast)
            w_in_b, w_fo_b, w_co_b, w_ao_b, w_out_b, w_g_b, w_u_b, w_d_b = (
                wb.reshape(a.shape) for wb, a in zip(o[n_o:], weights_f32))
            o = o[:n_o]
        yf = _fourier(u, uc, mats, with_ctx=with_ctx)
        ctx_parts = (xc, yf[1], bgc, zc, o[1]) if with_ctx else None
        x1 = _merge(l, xl, mod, norm1_g, yf[0], bg, z, o[0], ctx_parts, conv_w, conv_b, w_fo_b, w_co_b, w_ao_b,
                    w_in_b, b_in, w_out_b, with_ctx=with_ctx)
        out = _ffn(l, x1[0], x1[1] if with_ctx else None, mod, norm2_g, w_g_b, w_u_b, w_d_b, final_g.reshape(1, -1),
                   last=last)
        xl = out[0]
        if with_ctx:
            xc = out[1]
    return xl
```
